```python
import math
import jax
import jax.numpy as jnp
from jax import lax
import numpy as np

D_MODEL = 2048
BATCH = 4
SEQ = 2048
DEPTH = 2
DEC_BATCH = 8
DEC_SEQ = 1
PAST_LEN = 16384
PAGE_SIZE = 128

HEAD_DIM = 128
MOBA_HEADS = 4
MOBA_BLOCK = 256
MOBA_TOPK = 3
NSA_HEADS = 4
NSA_KV_HEADS = 1
NSA_GROUP = NSA_HEADS // NSA_KV_HEADS
CMP_STRIDE = 16
CMP_LEN = 2 * CMP_STRIDE
SLC_BLOCK = 64
SLC_TOPK = 16
WINDOW = 512
LRU_WIDTH = D_MODEL // 2
LRU_BLOCKS = 8
LRU_BLOCK = LRU_WIDTH // LRU_BLOCKS
LRU_C = 8.0
CONV_W = 4
N_EXPERTS = 16
N_GROUPS = 4
EXPERTS_PER_GROUP = N_EXPERTS // N_GROUPS
TOP_K = 2
D_EXPERT = D_MODEL // 2
T5_BUCKETS = 32
T5_MAX_DIST = 128
N_BIAS_HEADS = MOBA_HEADS + NSA_HEADS
GATHER_Q_BLK = 32
WIN_Q_BLK = 128
EPS = 1e-6
NEG = -1e30

MOBA_W = MOBA_HEADS * HEAD_DIM
NSA_W = NSA_HEADS * HEAD_DIM
NSA_KV_W = NSA_KV_HEADS * HEAD_DIM
IN_WIDTHS = (MOBA_W, MOBA_W, MOBA_W, NSA_W, NSA_KV_W, NSA_KV_W, NSA_KV_W, NSA_KV_W, NSA_KV_W, NSA_KV_W,
             3 * NSA_HEADS, LRU_WIDTH, LRU_WIDTH, D_MODEL, D_MODEL, D_MODEL)
D_IN = 3 * MOBA_W + NSA_W + 6 * NSA_KV_W + 3 * NSA_HEADS + 2 * LRU_WIDTH + 3 * D_MODEL
KV_WIDTHS = (MOBA_W, MOBA_W, NSA_KV_W, NSA_KV_W, NSA_KV_W, NSA_KV_W)
KV_DIM = 2 * MOBA_W + 4 * NSA_KV_W
WIN_DIM = 2 * NSA_KV_W

kernel_name = 'hybrid_moba_nsa_rglru_moe_decode_step'


def _split(z, widths):
    outs, s = [], 0
    for w in widths:
        outs.append(z[..., s:s + w])
        s += w
    return outs


def _heads(a, n):
    return a.reshape(a.shape[0], a.shape[1], n, HEAD_DIM)


def _rms(x, g):
    xf = x.astype(jnp.float32)
    y = xf * lax.rsqrt(jnp.mean(xf * xf, axis=-1, keepdims=True) + EPS)
    return (y * g.astype(jnp.float32)).astype(x.dtype)


def _masked_softmax(logits, mask):
    logits = jnp.where(mask, logits.astype(jnp.float32), NEG)
    p = jax.nn.softmax(logits, axis=-1)
    return jnp.where(mask, p, 0.0)


def _t5_bucket(dist):
    n = jnp.maximum(dist, 0)
    exact = T5_BUCKETS // 2
    nf = jnp.maximum(n, 1).astype(jnp.float32)
    large = exact + (jnp.log(nf / exact) / math.log(T5_MAX_DIST / exact) * (T5_BUCKETS - exact)).astype(jnp.int32)
    return jnp.where(n < exact, n, jnp.minimum(large, T5_BUCKETS - 1))


def _over_query_blocks(fn, blk, *qside):
    T = qside[0].shape[1]
    C = blk if T % blk == 0 else T
    n = T // C
    if n == 1:
        return fn(0, *qside)
    chunks = [jnp.swapaxes(a.reshape(a.shape[0], n, C, *a.shape[2:]), 0, 1) for a in qside]
    out = lax.map(lambda args: fn(args[0], *args[1:]), (jnp.arange(n) * C, *chunks))
    out = jnp.swapaxes(out, 0, 1)
    return out.reshape(out.shape[0], T, *out.shape[3:])


def _moba(q, k, v, offset, bias_tab):
    B, T, H, _ = q.shape
    L = k.shape[1]
    nb = -(-L // MOBA_BLOCK)
    pad = ((0, 0), (0, nb * MOBA_BLOCK - L), (0, 0), (0, 0))
    kb = jnp.pad(k, pad).reshape(B, nb, MOBA_BLOCK, H, HEAD_DIM)
    vb = jnp.pad(v, pad).reshape(B, nb, MOBA_BLOCK, H, HEAD_DIM)
    k_mean = jnp.mean(kb.astype(jnp.float32), axis=2)
    qpos = offset + jnp.arange(T)
    own = qpos // MOBA_BLOCK
    score = jnp.einsum('bthd,bnhd->bthn', q.astype(jnp.float32), k_mean)
    past = jnp.arange(nb)[None, :] < own[:, None]
    score = jnp.where(past[None, :, None, :], score, NEG)
    _, top = lax.top_k(score, min(MOBA_TOPK, nb))
    own_b = jnp.broadcast_to(own[None, :, None, None], (B, T, H, 1))
    sel = jnp.concatenate([top, own_b], axis=-1)
    sel_ok = jnp.concatenate([top < own_b, jnp.ones_like(own_b, dtype=bool)], axis=-1)
    kb = kb.transpose(0, 3, 1, 2, 4)
    vb = vb.transpose(0, 3, 1, 2, 4)
    tab = bias_tab[:, :MOBA_HEADS].T
    bi = jnp.arange(B)[:, None, None, None]
    hi = jnp.arange(H)[None, None, :, None]
    scale = HEAD_DIM ** -0.5

    def attend(qs, q_c, sel_c, ok_c):
        C = q_c.shape[1]
        S = sel_c.shape[-1]
        kg = kb[bi, hi, sel_c]
        vg = vb[bi, hi, sel_c]
        qp = offset + qs + jnp.arange(C)
        dist = qp[None, :, None, None, None] - (sel_c[..., None] * MOBA_BLOCK + jnp.arange(MOBA_BLOCK))
        logits = jnp.einsum('bchd,bchskd->bchsk', q_c, kg).astype(jnp.float32) * scale
        logits = logits + tab[hi[..., None], _t5_bucket(dist)]
        mask = ok_c[..., None] & (dist >= 0)
        p = _masked_softmax(logits.reshape(B, C, H, S * MOBA_BLOCK), mask.reshape(B, C, H, S * MOBA_BLOCK))
        p = p.reshape(B, C, H, S, MOBA_BLOCK)
        return jnp.einsum('bchsk,bchskd->bchd', p.astype(v.dtype), vg)

    return _over_query_blocks(attend, GATHER_Q_BLK, q, sel, sel_ok)


def _nsa(q, kc, vc, ks, vs, kw, vw, gates, offset, w_start, pe_k, pe_v, w_ck, w_cv, bias_tab):
    B, T, H, _ = q.shape
    L = kc.shape[1]
    KV, G = NSA_KV_HEADS, NSA_GROUP
    scale = HEAD_DIM ** -0.5
    qg = q.reshape(B, T, KV, G, HEAD_DIM)
    qpos = offset + jnp.arange(T)
    tab = bias_tab[:, MOBA_HEADS:].T.reshape(KV, G, T5_BUCKETS)

    n_ch = L // CMP_STRIDE
    n_cmp = n_ch - 1

    def compress(x, pe, w):
        ch = x[:, :n_ch * CMP_STRIDE].reshape(B, n_ch, CMP_STRIDE, KV, HEAD_DIM)
        blk = jnp.concatenate([ch[:, :-1], ch[:, 1:]], axis=2) + pe[:, None, :]
        return jnp.einsum('bnlgd,lde->bnge', blk, w)

    k_cmp = compress(kc, pe_k, w_ck)
    v_cmp = compress(vc, pe_v, w_cv)
    c_end = jnp.arange(n_cmp) * CMP_STRIDE + CMP_LEN - 1
    d_cmp = qpos[:, None] - c_end[None, :]
    logits = jnp.einsum('btgjd,bngd->btgjn', qg, k_cmp).astype(jnp.float32) * scale
    logits = logits + tab[:, :, _t5_bucket(d_cmp)].transpose(2, 0, 1, 3)
    p_cmp = _masked_softmax(logits, (d_cmp >= 0)[:, None, None, :])
    o_cmp = jnp.einsum('btgjn,bngd->btgjd', p_cmp.astype(vc.dtype), v_cmp)

    n_slc = -(-L // SLC_BLOCK)
    c_start = jnp.arange(n_cmp) * CMP_STRIDE
    s_start = jnp.arange(n_slc) * SLC_BLOCK
    overlap = ((c_start[:, None] < s_start[None, :] + SLC_BLOCK)
               & (c_start[:, None] + CMP_LEN > s_start[None, :])).astype(jnp.float32)
    imp = jnp.einsum('btgjn,ns->btgs', p_cmp, overlap)
    own = qpos // SLC_BLOCK
    past = jnp.arange(n_slc)[None, :] < own[:, None]
    imp = jnp.where(past[None, :, None, :], imp, NEG)
    _, top = lax.top_k(imp, min(SLC_TOPK - 1, n_slc))
    own_b = jnp.broadcast_to(own[None, :, None, None], (B, T, KV, 1))
    sel = jnp.concatenate([top, own_b], axis=-1)
    sel_ok = jnp.concatenate([top < own_b, jnp.ones_like(own_b, dtype=bool)], axis=-1)
    pad = ((0, 0), (0, n_slc * SLC_BLOCK - L), (0, 0), (0, 0))
    ksb = jnp.pad(ks, pad).reshape(B, n_slc, SLC_BLOCK, KV, HEAD_DIM).transpose(0, 3, 1, 2, 4)
    vsb = jnp.pad(vs, pad).reshape(B, n_slc, SLC_BLOCK, KV, HEAD_DIM).transpose(0, 3, 1, 2, 4)
    bi = jnp.arange(B)[:, None, None, None]
    gi = jnp.arange(KV)[None, None, :, None]
    gi6 = jnp.arange(KV)[None, None, :, None, None, None]
    ji6 = jnp.arange(G)[None, None, None, :, None, None]

    def attend_slc(qs, q_c, sel_c, ok_c):
        C = q_c.shape[1]
        S = sel_c.shape[-1]
        kg = ksb[bi, gi, sel_c]
        vg = vsb[bi, gi, sel_c]
        qp = offset + qs + jnp.arange(C)
        dist = qp[None, :, None, None, None] - (sel_c[..., None] * SLC_BLOCK + jnp.arange(SLC_BLOCK))
        logits = jnp.einsum('bcgjd,bcgskd->bcgjsk', q_c, kg).astype(jnp.float32) * scale
        logits = logits + tab[gi6, ji6, _t5_bucket(dist)[:, :, :, None]]
        mask = jnp.broadcast_to((ok_c[..., None] & (dist >= 0))[:, :, :, None], logits.shape)
        shp = (B, C, KV, G, S * SLC_BLOCK)
        p = _masked_softmax(logits.reshape(shp), mask.reshape(shp)).reshape(logits.shape)
        return jnp.einsum('bcgjsk,bcgskd->bcgjd', p.astype(vs.dtype), vg)

    o_slc = _over_query_blocks(attend_slc, GATHER_Q_BLK, qg, sel, sel_ok)

    kwp = jnp.pad(kw, ((0, 0), (WINDOW, 0), (0, 0), (0, 0)))
    vwp = jnp.pad(vw, ((0, 0), (WINDOW, 0), (0, 0), (0, 0)))

    def attend_win(qs, q_c):
        C = q_c.shape[1]
        start = offset + qs - w_start
        kseg = lax.dynamic_slice_in_dim(kwp, start, WINDOW + C, axis=1)
        vseg = lax.dynamic_slice_in_dim(vwp, start, WINDOW + C, axis=1)
        dist = jnp.arange(C)[:, None] + WINDOW - jnp.arange(WINDOW + C)[None, :]
        kpos = offset + qs - WINDOW + jnp.arange(WINDOW + C)
        mask = (dist >= 0) & (dist < WINDOW) & (kpos >= w_start)[None, :]
        logits = jnp.einsum('bcgjd,bkgd->bcgjk', q_c, kseg).astype(jnp.float32) * scale
        logits = logits + tab[:, :, _t5_bucket(dist)].transpose(2, 0, 1, 3)
        p = _masked_softmax(logits, mask[:, None, None, :])
        return jnp.einsum('bcgjk,bkgd->bcgjd', p.astype(vw.dtype), vseg)

    o_win = _over_query_blocks(attend_win, WIN_Q_BLK, qg)

    g = jax.nn.sigmoid(gates.astype(jnp.float32)).reshape(B, T, KV, G, 3).astype(q.dtype)
    o = g[..., 0:1] * o_cmp + g[..., 1:2] * o_slc + g[..., 2:3] * o_win
    return o.reshape(B, T, NSA_W)


def _rglru(xb, gb, conv0, h0, conv_w, conv_b, w_r, b_r, w_i, b_i, lam):
    B, T, _ = xb.shape
    xc = jnp.concatenate([conv0, xb], axis=1)
    u = conv_b
    for j in range(CONV_W):
        u = u + xc[:, j:j + T] * conv_w[j]
    ub = u.reshape(B, T, LRU_BLOCKS, LRU_BLOCK)
    r = jax.nn.sigmoid(jnp.einsum('btnk,nkj->btnj', ub, w_r).reshape(B, T, LRU_WIDTH) + b_r)
    i = jax.nn.sigmoid(jnp.einsum('btnk,nkj->btnj', ub, w_i).reshape(B, T, LRU_WIDTH) + b_i)
    log_a = -LRU_C * r.astype(jnp.float32) * jax.nn.softplus(-lam.astype(jnp.float32))
    a = jnp.exp(log_a)
    inp = jnp.sqrt(1.0 - jnp.exp(2.0 * log_a)) * (i * u).astype(jnp.float32)

    def step(h, at_xt):
        a_t, x_t = at_xt
        h = a_t * h + x_t
        return h, h

    h_last, hs = lax.scan(step, h0.astype(jnp.float32), (jnp.swapaxes(a, 0, 1), jnp.swapaxes(inp, 0, 1)))
    y = jnp.swapaxes(hs, 0, 1).astype(xb.dtype) * jax.nn.gelu(gb)
    return y, h_last.astype(h0.dtype), xc[:, xc.shape[1] - (CONV_W - 1):]


def _moe(h, w_router, b_router, w_g, w_u, w_d):
    B, T, D = h.shape
    hf = h.reshape(B * T, D)
    probs = jax.nn.softmax((hf @ w_router + b_router).astype(jnp.float32), axis=-1)
    grp = probs.reshape(-1, N_GROUPS, EXPERTS_PER_GROUP)
    g_score = lax.top_k(grp, TOP_K)[0].sum(-1)
    g_best = jnp.argmax(g_score, axis=-1)
    in_grp = jnp.take_along_axis(grp, g_best[:, None, None], axis=1)[:, 0]
    w_top, i_top = lax.top_k(in_grp, TOP_K)
    w_top = w_top / jnp.sum(w_top, axis=-1, keepdims=True)
    e_idx = g_best[:, None] * EXPERTS_PER_GROUP + i_top
    comb = jnp.sum(jax.nn.one_hot(e_idx, N_EXPERTS, dtype=jnp.float32) * w_top[..., None], axis=1).astype(h.dtype)
    y = jnp.zeros_like(hf)
    for e in range(N_EXPERTS):
        act = jax.nn.silu(hf @ w_g[e]) * (hf @ w_u[e])
        y = y + comb[:, e:e + 1] * (act @ w_d[e])
    return y.reshape(B, T, D)


def _layer(x, c, past_kv, win_buf, h0, conv0, lp, t5_bias, w_router, b_router):
    B, T, _ = x.shape
    offset = past_kv.shape[1]
    mod = jax.nn.silu(c) @ lp['w_ada'] + lp['b_ada']
    sh1, sc1, g1, sh2, sc2, g2 = [m[:, None, :] for m in jnp.split(mod, 6, axis=-1)]
    h = _rms(x, lp['norm_mix']) * (1.0 + sc1) + sh1
    z = h @ lp['w_in']
    mq, mk, mv, nq, ck, cv, sk, sv, wk, wv, ng, lx, lg, ga, gb, gc = _split(z, IN_WIDTHS)
    new_kv = jnp.concatenate([mk, mv, ck, cv, sk, sv], axis=-1)
    fmk, fmv, fck, fcv, fsk, fsv = _split(jnp.concatenate([past_kv, new_kv], axis=1), KV_WIDTHS)
    full_win = jnp.concatenate([win_buf, jnp.concatenate([wk, wv], axis=-1)], axis=1)
    fwk, fwv = _split(full_win, (NSA_KV_W, NSA_KV_W))
    o_a = _moba(_heads(mq, MOBA_HEADS), _heads(fmk, MOBA_HEADS), _heads(fmv, MOBA_HEADS), offset, t5_bias)
    o_b = _nsa(_heads(nq, NSA_HEADS), _heads(fck, NSA_KV_HEADS), _heads(fcv, NSA_KV_HEADS),
               _heads(fsk, NSA_KV_HEADS), _heads(fsv, NSA_KV_HEADS), _heads(fwk, NSA_KV_HEADS), _heads(fwv, NSA_KV_HEADS),
               ng, offset, offset - win_buf.shape[1], lp['cmp_pos_k'], lp['cmp_pos_v'], lp['w_cmp_k'], lp['w_cmp_v'], t5_bias)
    o_c, h_last, conv_last = _rglru(lx, lg, conv0, h0, lp['conv_w'], lp['conv_b'], lp['w_rg'], lp['b_rg'],
                                    lp['w_ig'], lp['b_ig'], lp['lru_lambda'])
    merged = (jax.nn.sigmoid(ga) * (o_a.reshape(B, T, MOBA_W) @ lp['w_br_moba'])
              + jax.nn.sigmoid(gb) * (o_b @ lp['w_br_nsa'])
              + jax.nn.sigmoid(gc) * (o_c @ lp['w_br_lru']))
    x = x + g1 * (merged @ lp['w_out'])
    h2 = _rms(x, lp['norm_ffn']) * (1.0 + sc2) + sh2
    x = x + g2 * _moe(h2, w_router, b_router, lp['w_e_gate'], lp['w_e_up'], lp['w_e_down'])
    n_keep = min(WINDOW, full_win.shape[1])
    return x, new_kv, full_win[:, full_win.shape[1] - n_keep:], h_last, conv_last


def setup_inputs(seed: int = 0) -> dict:
    key = jax.random.key(seed)
    ks = iter(jax.random.split(key, 48))
    f32 = jnp.float32

    def nrm(shape, scale):
        return jax.random.normal(next(ks), shape, f32) * scale

    n_pages = PAST_LEN // PAGE_SIZE
    n_used = DEC_BATCH * n_pages
    n_pool = n_used + n_used // 4
    page_table = jax.random.permutation(next(ks), n_pool)[:n_used].reshape(DEC_BATCH, n_pages).astype(jnp.int32)
    win_len = min(WINDOW, PAST_LEN)
    u = jax.random.uniform(next(ks), (DEPTH, LRU_WIDTH), f32, 0.9, 0.999)
    a0 = u ** (1.0 / LRU_C)
    lru_lambda = jnp.log(a0) - jnp.log1p(-a0)
    return {
        'x_prompt': nrm((BATCH, SEQ, D_MODEL), 1.0),
        'x_sample': nrm((DEC_BATCH, DEC_SEQ, D_MODEL), 1.0),
        'cache_kv': nrm((DEPTH, n_pool, PAGE_SIZE, KV_DIM), 1.0),
        'cache_win': nrm((DEPTH, DEC_BATCH, win_len, WIN_DIM), 1.0),
        'state_lru_h': nrm((DEPTH, DEC_BATCH, LRU_WIDTH), 0.5),
        'state_lru_conv': nrm((DEPTH, DEC_BATCH, CONV_W - 1, LRU_WIDTH), 1.0),
        'page_table': page_table,
        'c_prompt': nrm((BATCH, D_MODEL), 1.0),
        'c_sample': nrm((DEC_BATCH, D_MODEL), 1.0),
        'w_ada': nrm((DEPTH, D_MODEL, 6 * D_MODEL), 0.5 * D_MODEL ** -0.5),
        'b_ada': nrm((DEPTH, 6 * D_MODEL), 0.02),
        'norm_mix': 1.0 + nrm((DEPTH, D_MODEL), 0.05),
        'norm_ffn': 1.0 + nrm((DEPTH, D_MODEL), 0.05),
        'w_in': nrm((DEPTH, D_MODEL, D_IN), D_MODEL ** -0.5),
        'cmp_pos_k': nrm((DEPTH, CMP_LEN, HEAD_DIM), 0.5),
        'cmp_pos_v': nrm((DEPTH, CMP_LEN, HEAD_DIM), 0.5),
        'w_cmp_k': nrm((DEPTH, CMP_LEN, HEAD_DIM, HEAD_DIM), (CMP_LEN * HEAD_DIM) ** -0.5),
        'w_cmp_v': nrm((DEPTH, CMP_LEN, HEAD_DIM, HEAD_DIM), (CMP_LEN * HEAD_DIM) ** -0.5),
        'conv_w': nrm((DEPTH, CONV_W, LRU_WIDTH), CONV_W ** -0.5),
        'conv_b': nrm((DEPTH, LRU_WIDTH), 0.02),
        'w_rg': nrm((DEPTH, LRU_BLOCKS, LRU_BLOCK, LRU_BLOCK), LRU_BLOCK ** -0.5),
        'b_rg': nrm((DEPTH, LRU_WIDTH), 0.1),
        'w_ig': nrm((DEPTH, LRU_BLOCKS, LRU_BLOCK, LRU_BLOCK), LRU_BLOCK ** -0.5),
        'b_ig': nrm((DEPTH, LRU_WIDTH), 0.1),
        'lru_lambda': lru_lambda,
        'w_br_moba': nrm((DEPTH, MOBA_W, D_MODEL), MOBA_W ** -0.5),
        'w_br_nsa': nrm((DEPTH, NSA_W, D_MODEL), NSA_W ** -0.5),
        'w_br_lru': nrm((DEPTH, LRU_WIDTH, D_MODEL), LRU_WIDTH ** -0.5),
        'w_out': nrm((DEPTH, D_MODEL, D_MODEL), D_MODEL ** -0.5),
        'w_e_gate': nrm((DEPTH, N_EXPERTS, D_MODEL, D_EXPERT), D_MODEL ** -0.5),
        'w_e_up': nrm((DEPTH, N_EXPERTS, D_MODEL, D_EXPERT), D_MODEL ** -0.5),
        'w_e_down': nrm((DEPTH, N_EXPERTS, D_EXPERT, D_MODEL), D_EXPERT ** -0.5),
        't5_bias': nrm((T5_BUCKETS, N_BIAS_HEADS), 0.5),
        'w_router': nrm((D_MODEL, N_EXPERTS), D_MODEL ** -0.5),
        'b_router': nrm((N_EXPERTS,), 0.01),
        'norm_final': 1.0 + nrm((D_MODEL,), 0.05),
    }


def reference(x_prompt, x_sample, cache_kv, cache_win, state_lru_h, state_lru_conv, page_table, c_prompt, c_sample,
              w_ada, b_ada, norm_mix, norm_ffn, w_in, cmp_pos_k, cmp_pos_v, w_cmp_k, w_cmp_v, conv_w, conv_b,
              w_rg, b_rg, w_ig, b_ig, lru_lambda, w_br_moba, w_br_nsa, w_br_lru, w_out, w_e_gate, w_e_up, w_e_down,
              t5_bias, w_router, b_router, norm_final):
    n_p, n_s = x_prompt.shape[0], x_sample.shape[0]
    dt = x_prompt.dtype
    xp, xs = x_prompt, x_sample
    kv_p, kv_s, win_p, win_s, h_p, h_s, conv_p, conv_s = [], [], [], [], [], [], [], []
    for l in range(DEPTH):
        lp = {'w_ada': w_ada[l], 'b_ada': b_ada[l], 'norm_mix': norm_mix[l], 'norm_ffn': norm_ffn[l],
              'w_in': w_in[l], 'cmp_pos_k': cmp_pos_k[l], 'cmp_pos_v': cmp_pos_v[l],
              'w_cmp_k': w_cmp_k[l], 'w_cmp_v': w_cmp_v[l], 'conv_w': conv_w[l], 'conv_b': conv_b[l],
              'w_rg': w_rg[l], 'b_rg': b_rg[l], 'w_ig': w_ig[l], 'b_ig': b_ig[l], 'lru_lambda': lru_lambda[l],
              'w_br_moba': w_br_moba[l], 'w_br_nsa': w_br_nsa[l], 'w_br_lru': w_br_lru[l], 'w_out': w_out[l],
              'w_e_gate': w_e_gate[l], 'w_e_up': w_e_up[l], 'w_e_down': w_e_down[l]}
        xp, kv_new, win_new, h_new, conv_new = _layer(
            xp, c_prompt, jnp.zeros((n_p, 0, KV_DIM), dt), jnp.zeros((n_p, 0, WIN_DIM), dt),
            jnp.zeros((n_p, LRU_WIDTH), dt), jnp.zeros((n_p, CONV_W - 1, LRU_WIDTH), dt),
            lp, t5_bias, w_router, b_router)
        kv_p.append(kv_new)
        win_p.append(win_new)
        h_p.append(h_new)
        conv_p.append(conv_new)
        past = cache_kv[l][page_table].reshape(n_s, -1, KV_DIM)
        xs, kv_new, win_new, h_new, conv_new = _layer(
            xs, c_sample, past, cache_win[l], state_lru_h[l], state_lru_conv[l],
            lp, t5_bias, w_router, b_router)
        kv_s.append(kv_new)
        win_s.append(win_new)
        h_s.append(h_new)
        conv_s.append(conv_new)
    y_prompt = _rms(xp, norm_final)
    y_sample = _rms(xs, norm_final)
    kv_rows_prompt = jnp.stack(kv_p)
    kv_rows_sample = jnp.stack(kv_s)
    win_prompt = jnp.stack(win_p)
    win_sample = jnp.stack(win_s)
    lru_h_prompt = jnp.stack(h_p)
    lru_h_sample = jnp.stack(h_s)
    lru_conv_prompt = jnp.stack(conv_p)
    lru_conv_sample = jnp.stack(conv_s)
    return (y_prompt, y_sample, kv_rows_prompt, kv_rows_sample, win_prompt, win_sample,
            lru_h_prompt, lru_h_sample, lru_conv_prompt, lru_conv_sample)
```

```python
import functools
import math

import jax
import jax.numpy as jnp
from jax import lax
from jax.experimental import pallas as pl
from jax.experimental.pallas import tpu as pltpu

D_MODEL = 2048
DEPTH = 2
PAGE_SIZE = 128
HEAD_DIM = 128
MOBA_HEADS = 4
MOBA_BLOCK = 256
MOBA_TOPK = 3
NSA_HEADS = 4
CMP_STRIDE = 16
CMP_LEN = 32
SLC_BLOCK = 64
SLC_TOPK = 16
WINDOW = 512
LRU_WIDTH = D_MODEL // 2
LRU_BLOCKS = 8
LRU_BLOCK = LRU_WIDTH // LRU_BLOCKS
LRU_C = 8.0
CONV_W = 4
N_EXPERTS = 16
N_GROUPS = 4
EXPERTS_PER_GROUP = N_EXPERTS // N_GROUPS
TOP_K = 2
D_EXPERT = D_MODEL // 2
T5_BUCKETS = 32
T5_MAX_DIST = 128
EPS = 1e-6
NEG = -1e30

MOBA_W = MOBA_HEADS * HEAD_DIM
NSA_W = NSA_HEADS * HEAD_DIM
KV_DIM = 2 * MOBA_W + 4 * HEAD_DIM
WIN_DIM = 2 * HEAD_DIM
IN_WIDTHS = (MOBA_W, MOBA_W, MOBA_W, NSA_W, HEAD_DIM, HEAD_DIM, HEAD_DIM, HEAD_DIM, HEAD_DIM, HEAD_DIM,
             3 * NSA_HEADS, LRU_WIDTH, LRU_WIDTH, D_MODEL, D_MODEL, D_MODEL)
COL_NG = 3 * MOBA_W + NSA_W + 6 * HEAD_DIM
COL_LX = COL_NG + 3 * NSA_HEADS
ZA_W = COL_NG + HEAD_DIM
SCALE = HEAD_DIM ** -0.5

LANE = 128
ATT_TILE = 256
MOE_TILE = 256
VMEM_LIMIT = 56 * 1024 * 1024

F32 = jnp.float32
BF16 = jnp.bfloat16


def _cparams(sem):
    return pltpu.CompilerParams(dimension_semantics=sem, vmem_limit_bytes=VMEM_LIMIT)


def _mm_kernel(x_ref, w_ref, o_ref):
    o_ref[...] = jnp.dot(x_ref[...].astype(BF16), w_ref[...].astype(BF16), preferred_element_type=F32)


def _matmul(x, w, tm, tn):
    M, K = x.shape
    N = w.shape[1]
    assert M % tm == 0 and N % tn == 0
    return pl.pallas_call(
        _mm_kernel,
        grid=(N // tn, M // tm),
        in_specs=[pl.BlockSpec((tm, K), lambda j, i: (i, 0)),
                  pl.BlockSpec((K, tn), lambda j, i: (0, j))],
        out_specs=pl.BlockSpec((tm, tn), lambda j, i: (i, j)),
        out_shape=jax.ShapeDtypeStruct((M, N), F32),
        compiler_params=_cparams(("parallel", "parallel")),
        name="matmul",
    )(x, w)


def _rank_select(score, own, nblk, topk):
    lane = lax.broadcasted_iota(jnp.int32, score.shape, 1)
    past = lane < own
    s = jnp.where(past, score, NEG)
    rank = jnp.zeros(score.shape, jnp.int32)
    for j in range(nblk):
        col = s[:, j:j + 1]
        beats = (col > s) | ((col == s) & (lane > j))
        rank = rank + beats.astype(jnp.int32)
    return (past & (rank < topk)) | (lane == own)


def _moba_select_kernel(q_ref, k_ref, sel_ref):
    T = q_ref.shape[1]
    nb = T // MOBA_BLOCK
    k = k_ref[0]
    kmean = jnp.sum(k.reshape(nb, MOBA_BLOCK, HEAD_DIM), axis=1) * (1.0 / MOBA_BLOCK)
    kmean = jnp.concatenate([kmean, jnp.zeros((LANE - nb, HEAD_DIM), F32)], axis=0)
    score = lax.dot_general(q_ref[0].astype(BF16), kmean.astype(BF16), (((1,), (1,)), ((), ())),
                            preferred_element_type=F32)
    own = lax.broadcasted_iota(jnp.int32, (T, 1), 0) // MOBA_BLOCK
    sel = _rank_select(score, own, nb, MOBA_TOPK)
    sel_ref[0, 0] = sel.astype(BF16)


def _moba_select(za, B, T):
    return pl.pallas_call(
        _moba_select_kernel,
        grid=(B, MOBA_HEADS),
        in_specs=[pl.BlockSpec((1, T, HEAD_DIM), lambda b, h: (b, 0, h)),
                  pl.BlockSpec((1, T, HEAD_DIM), lambda b, h: (b, 0, MOBA_HEADS + h))],
        out_specs=pl.BlockSpec((1, 1, T, LANE), lambda b, h: (b, h, 0, 0)),
        out_shape=jax.ShapeDtypeStruct((B, MOBA_HEADS, T, LANE), BF16),
        compiler_params=_cparams(("parallel", "parallel")),
        name="moba_select",
    )(za, za)


def _flash_kernel(*refs, G, window_tiles, use_sel):
    if use_sel:
        q_ref, k_ref, v_ref, bias_ref, sel_ref, e_ref, o_ref, m_scr, l_scr, acc_scr = refs
    else:
        q_ref, k_ref, v_ref, bias_ref, o_ref, m_scr, l_scr, acc_scr = refs
    qi = pl.program_id(2)
    kj = pl.program_id(3)
    nk = pl.num_programs(3)
    tq, tk = ATT_TILE, ATT_TILE

    @pl.when(kj == 0)
    def _():
        m_scr[...] = jnp.full(m_scr.shape, NEG, F32)
        l_scr[...] = jnp.zeros(l_scr.shape, F32)
        acc_scr[...] = jnp.zeros(acc_scr.shape, F32)

    lo = jnp.maximum(qi - window_tiles, 0) if window_tiles is not None else 0

    @pl.when((kj >= lo) & (kj <= qi))
    def _():
        k = k_ref[0].astype(BF16)
        v = v_ref[0].astype(BF16)
        row = lax.broadcasted_iota(jnp.int32, (tq, tk), 0)
        col = lax.broadcasted_iota(jnp.int32, (tq, tk), 1)
        dist = (qi - kj) * tq + row - col
        mask = dist >= 0
        if window_tiles is not None:
            mask = mask & (dist < WINDOW)
        if use_sel:
            hit = jnp.dot(sel_ref[0, 0], e_ref[...], preferred_element_type=F32)
            mask = mask & (hit > 0.5)
        for g in range(G):
            q = q_ref[0, :, g * HEAD_DIM:(g + 1) * HEAD_DIM].astype(BF16)
            s = lax.dot_general(q, k, (((1,), (1,)), ((), ())), preferred_element_type=F32)
            s = s * SCALE + bias_ref[g, 0]
            s = jnp.where(mask, s, NEG)
            m_prev = m_scr[g]
            m_new = jnp.maximum(m_prev, jnp.max(s, axis=1, keepdims=True))
            p = jnp.where(mask, jnp.exp(s - m_new), 0.0)
            alpha = jnp.exp(m_prev - m_new)
            l_scr[g] = alpha * l_scr[g] + jnp.sum(p, axis=1, keepdims=True)
            acc_scr[g] = alpha * acc_scr[g] + jnp.dot(p.astype(BF16), v, preferred_element_type=F32)
            m_scr[g] = m_new

    @pl.when(kj == nk - 1)
    def _():
        for g in range(G):
            l = l_scr[g]
            o = acc_scr[g] / jnp.where(l > 0.0, l, 1.0)
            o_ref[0, :, g * HEAD_DIM:(g + 1) * HEAD_DIM] = o


def _flash(za, bias, *, B, T, n_kv, G, q_col, k_col, v_col, bias_blk, window_tiles=None, sel=None, emat=None):
    tq = tk = ATT_TILE
    nq, nk = T // tq, T // tk
    use_sel = sel is not None

    def kv_blk(i, j):
        lo = jnp.maximum(i - window_tiles, 0) if window_tiles is not None else 0
        return jnp.clip(j, lo, i)

    in_specs = [
        pl.BlockSpec((1, tq, G * HEAD_DIM), lambda b, h, i, j: (b, i, q_col // G + h)),
        pl.BlockSpec((1, tk, HEAD_DIM), lambda b, h, i, j: (b, kv_blk(i, j), k_col + h)),
        pl.BlockSpec((1, tk, HEAD_DIM), lambda b, h, i, j: (b, kv_blk(i, j), v_col + h)),
        pl.BlockSpec((G, 1, tq, tk), lambda b, h, i, j: (bias_blk + h, jnp.clip(i - j, 0, 2), 0, 0)),
    ]
    args = [za, za, za, bias]
    if use_sel:
        in_specs += [pl.BlockSpec((1, 1, tq, LANE), lambda b, h, i, j: (b, h, i, 0)),
                     pl.BlockSpec((LANE, tk), lambda b, h, i, j: (0, kv_blk(i, j)))]
        args += [sel, emat]
    return pl.pallas_call(
        functools.partial(_flash_kernel, G=G, window_tiles=window_tiles, use_sel=use_sel),
        grid=(B, n_kv, nq, nk),
        in_specs=in_specs,
        out_specs=pl.BlockSpec((1, tq, G * HEAD_DIM), lambda b, h, i, j: (b, i, h)),
        out_shape=jax.ShapeDtypeStruct((B, T, n_kv * G * HEAD_DIM), F32),
        scratch_shapes=[pltpu.VMEM((G, tq, 1), F32), pltpu.VMEM((G, tq, 1), F32),
                        pltpu.VMEM((G, tq, HEAD_DIM), F32)],
        compiler_params=_cparams(("parallel", "parallel", "parallel", "arbitrary")),
        name="flash",
    )(*args)


def _cmp_kernel(ck_ref, cv_ref, q_ref, pek_ref, pev_ref, wk_ref, wv_ref, bias_ref, ovl_ref,
                o_ref, sel_ref, kc_scr, vc_scr):
    T = ck_ref.shape[1]
    n_ch = T // CMP_STRIDE
    i = pl.program_id(1)
    tq = ATT_TILE

    @pl.when(i == 0)
    def _():
        for src, pe, w, dst in ((ck_ref, pek_ref, wk_ref, kc_scr), (cv_ref, pev_ref, wv_ref, vc_scr)):
            first = jnp.zeros((n_ch, HEAD_DIM), F32)
            second = jnp.zeros((n_ch, HEAD_DIM), F32)
            for l in range(CMP_STRIDE):
                x = src[0, pl.ds(l, n_ch, stride=CMP_STRIDE), :]
                l2 = CMP_STRIDE + l
                first = first + jnp.dot((x + pe[l:l + 1, :]).astype(BF16), w[l * HEAD_DIM:(l + 1) * HEAD_DIM, :],
                                        preferred_element_type=F32)
                second = second + jnp.dot((x + pe[l2:l2 + 1, :]).astype(BF16),
                                          w[l2 * HEAD_DIM:(l2 + 1) * HEAD_DIM, :], preferred_element_type=F32)
            dst[...] = (first + pltpu.roll(second, n_ch - 1, 0)).astype(BF16)

    kc = kc_scr[...]
    vc = vc_scr[...]
    t = i * tq + lax.broadcasted_iota(jnp.int32, (tq, n_ch), 0)
    n = lax.broadcasted_iota(jnp.int32, (tq, n_ch), 1)
    vis = (n * CMP_STRIDE + (CMP_LEN - 1)) <= t
    imp = jnp.zeros((tq, LANE), F32)
    ovl = ovl_ref[...]
    for g in range(NSA_HEADS):
        q = q_ref[0, :, g * HEAD_DIM:(g + 1) * HEAD_DIM].astype(BF16)
        s = lax.dot_general(q, kc, (((1,), (1,)), ((), ())), preferred_element_type=F32)
        s = s * SCALE + bias_ref[g]
        s = jnp.where(vis, s, NEG)
        m = jnp.max(s, axis=1, keepdims=True)
        e = jnp.where(vis, jnp.exp(s - m), 0.0)
        l = jnp.sum(e, axis=1, keepdims=True)
        p = e / jnp.where(l > 0.0, l, 1.0)
        pb = p.astype(BF16)
        o_ref[0, :, g * HEAD_DIM:(g + 1) * HEAD_DIM] = jnp.dot(pb, vc, preferred_element_type=F32)
        imp = imp + jnp.dot(pb, ovl, preferred_element_type=F32)
    own = (i * tq + lax.broadcasted_iota(jnp.int32, (tq, 1), 0)) // SLC_BLOCK
    sel = _rank_select(imp, own, T // SLC_BLOCK, SLC_TOPK - 1)
    sel_ref[0, 0] = sel.astype(BF16)


def _cmp_branch(za, pe_k, pe_v, w_k, w_v, bias_cmp, overlap, B, T):
    assert T // CMP_STRIDE == LANE
    tq = ATT_TILE
    col = 3 * MOBA_HEADS + NSA_HEADS
    return pl.pallas_call(
        _cmp_kernel,
        grid=(B, T // tq),
        in_specs=[
            pl.BlockSpec((1, T, HEAD_DIM), lambda b, i: (b, 0, col)),
            pl.BlockSpec((1, T, HEAD_DIM), lambda b, i: (b, 0, col + 1)),
            pl.BlockSpec((1, tq, NSA_W), lambda b, i: (b, i, 3)),
            pl.BlockSpec((CMP_LEN, HEAD_DIM), lambda b, i: (0, 0)),
            pl.BlockSpec((CMP_LEN, HEAD_DIM), lambda b, i: (0, 0)),
            pl.BlockSpec((CMP_LEN * HEAD_DIM, HEAD_DIM), lambda b, i: (0, 0)),
            pl.BlockSpec((CMP_LEN * HEAD_DIM, HEAD_DIM), lambda b, i: (0, 0)),
            pl.BlockSpec((NSA_HEADS, tq, LANE), lambda b, i: (0, i, 0)),
            pl.BlockSpec((LANE, LANE), lambda b, i: (0, 0)),
        ],
        out_specs=[pl.BlockSpec((1, tq, NSA_W), lambda b, i: (b, i, 0)),
                   pl.BlockSpec((1, 1, tq, LANE), lambda b, i: (b, 0, i, 0))],
        out_shape=[jax.ShapeDtypeStruct((B, T, NSA_W), F32),
                   jax.ShapeDtypeStruct((B, 1, T, LANE), BF16)],
        scratch_shapes=[pltpu.VMEM((LANE, HEAD_DIM), BF16), pltpu.VMEM((LANE, HEAD_DIM), BF16)],
        compiler_params=_cparams(("parallel", "arbitrary")),
        name="nsa_cmp",
    )(za, za, za, pe_k, pe_v, w_k, w_v, bias_cmp, overlap)


def _gelu_tanh(x):
    return 0.5 * x * (1.0 + jnp.tanh(math.sqrt(2.0 / math.pi) * (x + 0.044715 * (x * x * x))))


def _softplus(x):
    return jnp.maximum(x, 0.0) + jnp.log1p(jnp.exp(-jnp.abs(x)))


def _lru_kernel(x_ref, g_ref, c0_ref, h0_ref, cw_ref, cb_ref, wr_ref, br_ref, wi_ref, bi_ref, lam_ref,
                y_ref, hl_ref, cl_ref, a_scr, x_scr):
    T = x_ref.shape[1]
    x = x_ref[0]
    row = lax.broadcasted_iota(jnp.int32, (T, LRU_BLOCK), 0)
    c0 = c0_ref[0]

    def shifted(d):
        r = pltpu.roll(x, d, 0)
        for t in range(d):
            r = jnp.where(row == t, c0[t + CONV_W - 1 - d:t + CONV_W - d, :], r)
        return r

    u = cb_ref[...] + shifted(3) * cw_ref[0:1, :]
    u = u + shifted(2) * cw_ref[1:2, :]
    u = u + shifted(1) * cw_ref[2:3, :]
    u = u + x * cw_ref[3:4, :]
    ub = u.astype(BF16)
    r = jax.nn.sigmoid(jnp.dot(ub, wr_ref[0], preferred_element_type=F32) + br_ref[...])
    ig = jax.nn.sigmoid(jnp.dot(ub, wi_ref[0], preferred_element_type=F32) + bi_ref[...])
    log_a = (-LRU_C * r) * _softplus(-lam_ref[...])
    a = jnp.exp(log_a)
    b = jnp.sqrt(1.0 - jnp.exp(2.0 * log_a)) * (ig * u)
    sub = row % 8
    for d in (1, 2, 4):
        ok = sub >= d
        b = jnp.where(ok, a * pltpu.roll(b, d, 0) + b, b)
        a = jnp.where(ok, a * pltpu.roll(a, d, 0), a)
    a_scr[...] = a
    x_scr[...] = b

    def group(k, carry):
        s = pl.multiple_of(k * 8, 8)
        h = x_scr[pl.ds(s, 8), :] + a_scr[pl.ds(s, 8), :] * carry
        x_scr[pl.ds(s, 8), :] = h
        return h[7:8, :]

    h_last = lax.fori_loop(0, T // 8, group, h0_ref[0], unroll=8)
    y_ref[0] = x_scr[...] * _gelu_tanh(g_ref[0])
    hl_ref[0] = h_last
    cl_ref[0] = x[T - (CONV_W - 1):, :]


def _rglru(zb, conv0, h0, conv_w, conv_b, w_r, b_r, w_i, b_i, lam, B, T):
    nblk = LRU_BLOCKS
    vec = lambda a: a.reshape(1, LRU_WIDTH)
    row_spec = pl.BlockSpec((1, LRU_BLOCK), lambda b, n: (0, n))
    return pl.pallas_call(
        _lru_kernel,
        grid=(B, nblk),
        in_specs=[
            pl.BlockSpec((1, T, LRU_BLOCK), lambda b, n: (b, 0, n)),
            pl.BlockSpec((1, T, LRU_BLOCK), lambda b, n: (b, 0, nblk + n)),
            pl.BlockSpec((1, CONV_W - 1, LRU_BLOCK), lambda b, n: (b, 0, n)),
            pl.BlockSpec((1, 1, LRU_BLOCK), lambda b, n: (b, 0, n)),
            pl.BlockSpec((CONV_W, LRU_BLOCK), lambda b, n: (0, n)),
            row_spec,
            pl.BlockSpec((1, LRU_BLOCK, LRU_BLOCK), lambda b, n: (n, 0, 0)),
            row_spec,
            pl.BlockSpec((1, LRU_BLOCK, LRU_BLOCK), lambda b, n: (n, 0, 0)),
            row_spec,
            row_spec,
        ],
        out_specs=[pl.BlockSpec((1, T, LRU_BLOCK), lambda b, n: (b, 0, n)),
                   pl.BlockSpec((1, 1, LRU_BLOCK), lambda b, n: (b, 0, n)),
                   pl.BlockSpec((1, CONV_W - 1, LRU_BLOCK), lambda b, n: (b, 0, n))],
        out_shape=[jax.ShapeDtypeStruct((B, T, LRU_WIDTH), F32),
                   jax.ShapeDtypeStruct((B, 1, LRU_WIDTH), F32),
                   jax.ShapeDtypeStruct((B, CONV_W - 1, LRU_WIDTH), F32)],
        scratch_shapes=[pltpu.VMEM((T, LRU_BLOCK), F32), pltpu.VMEM((T, LRU_BLOCK), F32)],
        compiler_params=_cparams(("parallel", "parallel")),
        name="rglru",
    )(zb, zb, conv0, h0.reshape(B, 1, LRU_WIDTH), conv_w, vec(conv_b), w_r.astype(BF16), vec(b_r),
      w_i.astype(BF16), vec(b_i), vec(lam))


def _moe_kernel(eid_ref, nt_ref, x_ref, wg_ref, wu_ref, wd_ref, cw_ref, o_ref):
    t = pl.program_id(0)

    @pl.when(t < nt_ref[0])
    def _():
        x = x_ref[...]
        g = jnp.dot(x, wg_ref[0], preferred_element_type=F32)
        u = jnp.dot(x, wu_ref[0], preferred_element_type=F32)
        act = (g * jax.nn.sigmoid(g)) * u
        y = jnp.dot(act.astype(BF16), wd_ref[0], preferred_element_type=F32)
        o_ref[...] = y * cw_ref[...]

    @pl.when(t >= nt_ref[0])
    def _():
        o_ref[...] = jnp.zeros(o_ref.shape, F32)


def _moe_ffn(x_pad, cw_pad, tile_eid, n_tiles, w_g, w_u, w_d):
    P, D = x_pad.shape
    tm = MOE_TILE
    grid_spec = pltpu.PrefetchScalarGridSpec(
        num_scalar_prefetch=2,
        grid=(P // tm,),
        in_specs=[
            pl.BlockSpec((tm, D), lambda t, eid, nt: (t, 0)),
            pl.BlockSpec((1, D, D_EXPERT), lambda t, eid, nt: (eid[t], 0, 0)),
            pl.BlockSpec((1, D, D_EXPERT), lambda t, eid, nt: (eid[t], 0, 0)),
            pl.BlockSpec((1, D_EXPERT, D), lambda t, eid, nt: (eid[t], 0, 0)),
            pl.BlockSpec((tm, 1), lambda t, eid, nt: (t, 0)),
        ],
        out_specs=pl.BlockSpec((tm, D), lambda t, eid, nt: (t, 0)),
    )
    return pl.pallas_call(
        _moe_kernel,
        grid_spec=grid_spec,
        out_shape=jax.ShapeDtypeStruct((P, D), F32),
        compiler_params=_cparams(("arbitrary",)),
        name="moe_ffn",
    )(tile_eid, n_tiles, x_pad, w_g, w_u, w_d, cw_pad)


def _route(logits):
    probs = jax.nn.softmax(logits.astype(F32), axis=-1)
    grp = probs.reshape(-1, N_GROUPS, EXPERTS_PER_GROUP)
    g_score = lax.top_k(grp, TOP_K)[0].sum(-1)
    g_best = jnp.argmax(g_score, axis=-1)
    in_grp = jnp.take_along_axis(grp, g_best[:, None, None], axis=1)[:, 0]
    w_top, i_top = lax.top_k(in_grp, TOP_K)
    w_top = w_top / jnp.sum(w_top, axis=-1, keepdims=True)
    e_idx = g_best[:, None] * EXPERTS_PER_GROUP + i_top
    return e_idx.astype(jnp.int32), w_top


def _moe_prompt(h2, w_router_pad, b_router, w_g, w_u, w_d):
    N, D = h2.shape
    tm = MOE_TILE
    logits = _matmul(h2, w_router_pad, 512, LANE)[:, :N_EXPERTS] + b_router
    e_idx, w_top = _route(logits)
    flat_e = e_idx.reshape(-1)
    onehot = (flat_e[:, None] == jnp.arange(N_EXPERTS)[None, :]).astype(jnp.int32)
    within = jnp.take_along_axis(jnp.cumsum(onehot, axis=0), flat_e[:, None], axis=1)[:, 0] - 1
    sizes = jnp.sum(onehot, axis=0)
    padded = ((sizes + tm - 1) // tm) * tm
    pend = jnp.cumsum(padded)
    pstart = pend - padded
    dest = pstart[flat_e] + within
    P = 2 * N + N_EXPERTS * tm
    src_tok = jnp.zeros((P,), jnp.int32).at[dest].set(jnp.arange(2 * N, dtype=jnp.int32) // TOP_K)
    cw_pad = jnp.zeros((P,), F32).at[dest].set(w_top.reshape(-1))
    x_pad = h2.astype(BF16)[src_tok]
    tile_start = jnp.arange(P // tm, dtype=jnp.int32) * tm
    tile_eid = jnp.minimum(jnp.searchsorted(pend, tile_start, side='right'), N_EXPERTS - 1).astype(jnp.int32)
    n_tiles = (pend[-1] // tm).astype(jnp.int32).reshape(1)
    out = _moe_ffn(x_pad, cw_pad.reshape(P, 1), tile_eid, n_tiles, w_g, w_u, w_d)
    pos = dest.reshape(N, TOP_K)
    return out[pos[:, 0]] + out[pos[:, 1]]


def _t5_bucket(dist):
    n = jnp.maximum(dist, 0)
    exact = T5_BUCKETS // 2
    nf = jnp.maximum(n, 1).astype(F32)
    large = exact + (jnp.log(nf / exact) / math.log(T5_MAX_DIST / exact) * (T5_BUCKETS - exact)).astype(jnp.int32)
    return jnp.where(n < exact, n, jnp.minimum(large, T5_BUCKETS - 1))


def _rms(x, g):
    return x * lax.rsqrt(jnp.mean(x * x, axis=-1, keepdims=True) + EPS) * g


def _split(z, widths):
    outs, s = [], 0
    for w in widths:
        outs.append(z[..., s:s + w])
        s += w
    return outs


def _prompt_tables(t5_bias, T):
    t = ATT_TILE
    i = jnp.arange(t)
    dist = jnp.arange(3)[:, None, None] * t + i[None, :, None] - i[None, None, :]
    bias_tiles = t5_bias.T[:, _t5_bucket(dist)]
    pos = jnp.arange(T)
    lane = jnp.arange(LANE)
    e_moba = (pos[None, :] // MOBA_BLOCK == lane[:, None]).astype(BF16)
    e_slc = (pos[None, :] // SLC_BLOCK == lane[:, None]).astype(BF16)
    d_cmp = pos[:, None] - (lane[None, :] * CMP_STRIDE + CMP_LEN - 1)
    bias_cmp = t5_bias.T[MOBA_HEADS:, _t5_bucket(d_cmp)]
    c_start = lane * CMP_STRIDE
    s_start = lane * SLC_BLOCK
    n_cmp = T // CMP_STRIDE - 1
    overlap = ((c_start[:, None] < s_start[None, :] + SLC_BLOCK) & (c_start[:, None] + CMP_LEN > s_start[None, :])
               & (lane[:, None] < n_cmp) & (lane[None, :] < T // SLC_BLOCK)).astype(BF16)
    return bias_tiles, e_moba, e_slc, bias_cmp, overlap


def _layer_prompt(x, mod, lp, tabs, w_router_pad, b_router):
    B, T, D = x.shape
    N = B * T
    bias_tiles, e_moba, e_slc, bias_cmp, overlap = tabs
    sh1, sc1, g1, sh2, sc2, g2 = [m[:, None, :] for m in jnp.split(mod, 6, axis=-1)]
    h = (_rms(x, lp['norm_mix']) * (1.0 + sc1) + sh1).astype(BF16).reshape(N, D)
    za = _matmul(h, lp['w_in'][:, :ZA_W].astype(BF16), 256, ZA_W).reshape(B, T, ZA_W)
    zb = _matmul(h, lp['w_in'][:, COL_LX:].astype(BF16), 512, 2048).reshape(B, T, -1)
    new_kv = jnp.concatenate([za[..., MOBA_W:3 * MOBA_W], za[..., 3 * MOBA_W + NSA_W:COL_NG - WIN_DIM]], axis=-1)
    win = za[:, T - WINDOW:, COL_NG - WIN_DIM:COL_NG]
    ng = za[..., COL_NG:COL_NG + 3 * NSA_HEADS]

    sel_moba = _moba_select(za, B, T)
    o_a = _flash(za, bias_tiles, B=B, T=T, n_kv=MOBA_HEADS, G=1, q_col=0, k_col=4, v_col=8, bias_blk=0,
                 sel=sel_moba, emat=e_moba)
    o_cmp, sel_slc = _cmp_branch(za, lp['cmp_pos_k'], lp['cmp_pos_v'],
                                 lp['w_cmp_k'].reshape(-1, HEAD_DIM).astype(BF16),
                                 lp['w_cmp_v'].reshape(-1, HEAD_DIM).astype(BF16), bias_cmp, overlap, B, T)
    o_slc = _flash(za, bias_tiles, B=B, T=T, n_kv=1, G=NSA_HEADS, q_col=12, k_col=18, v_col=19, bias_blk=1,
                   sel=sel_slc, emat=e_slc)
    o_win = _flash(za, bias_tiles, B=B, T=T, n_kv=1, G=NSA_HEADS, q_col=12, k_col=20, v_col=21, bias_blk=1,
                   window_tiles=WINDOW // ATT_TILE)
    gt = jax.nn.sigmoid(ng).reshape(B, T, NSA_HEADS, 3)
    o_b = (gt[..., 0:1] * o_cmp.reshape(B, T, NSA_HEADS, HEAD_DIM)
           + gt[..., 1:2] * o_slc.reshape(B, T, NSA_HEADS, HEAD_DIM)
           + gt[..., 2:3] * o_win.reshape(B, T, NSA_HEADS, HEAD_DIM)).reshape(B, T, NSA_W)

    o_c, h_last, conv_last = _rglru(zb, jnp.zeros((B, CONV_W - 1, LRU_WIDTH), F32), jnp.zeros((B, LRU_WIDTH), F32),
                                    lp['conv_w'], lp['conv_b'], lp['w_rg'], lp['b_rg'], lp['w_ig'], lp['b_ig'],
                                    lp['lru_lambda'], B, T)
    ga, gb, gc = _split(zb[..., 2 * LRU_WIDTH:], (D, D, D))
    pa = _matmul(o_a.astype(BF16).reshape(N, -1), lp['w_br_moba'].astype(BF16), 512, D)
    pb = _matmul(o_b.astype(BF16).reshape(N, -1), lp['w_br_nsa'].astype(BF16), 512, D)
    pc = _matmul(o_c.astype(BF16).reshape(N, -1), lp['w_br_lru'].astype(BF16), 512, D)
    merged = (jax.nn.sigmoid(ga).reshape(N, D) * pa + jax.nn.sigmoid(gb).reshape(N, D) * pb
              + jax.nn.sigmoid(gc).reshape(N, D) * pc)
    x = x + g1 * _matmul(merged.astype(BF16), lp['w_out'].astype(BF16), 512, D).reshape(B, T, D)
    h2 = (_rms(x, lp['norm_ffn']) * (1.0 + sc2) + sh2).reshape(N, D)
    moe = _moe_prompt(h2, w_router_pad, b_router, lp['w_e_gate'].astype(BF16), lp['w_e_up'].astype(BF16),
                      lp['w_e_down'].astype(BF16))
    x = x + g2 * moe.reshape(B, T, D)
    return x, new_kv, win, h_last.reshape(B, LRU_WIDTH), conv_last


def _dot_hi(a, b):
    return jnp.matmul(a, b)


def _softmax_rows(logits, mask):
    logits = jnp.where(mask, logits, NEG)
    p = jax.nn.softmax(logits, axis=-1)
    return jnp.where(mask, p, 0.0)


def _layer_sample(x, mod, cache_kv, page_table, win_buf, h0, conv0, lp, t5_bias, w_router, b_router):
    B, D = x.shape
    n_pages = page_table.shape[1]
    past = n_pages * PAGE_SIZE
    sh1, sc1, g1, sh2, sc2, g2 = jnp.split(mod, 6, axis=-1)
    h = _rms(x, lp['norm_mix']) * (1.0 + sc1) + sh1
    z = _dot_hi(h, lp['w_in'])
    mq, mk, mv, nq, ck, cv, sk, sv, wk, wv, ng, lx, lg, ga, gb, gc = _split(z, IN_WIDTHS)
    new_kv = jnp.concatenate([mk, mv, ck, cv, sk, sv], axis=-1)
    tab = t5_bias.T
    bidx = jnp.arange(B)

    nb = past // MOBA_BLOCK
    pages_per_blk = MOBA_BLOCK // PAGE_SIZE
    k_all = cache_kv[:, :, :MOBA_W].reshape(-1, PAGE_SIZE, MOBA_HEADS, HEAD_DIM)
    v_all = cache_kv[:, :, MOBA_W:2 * MOBA_W].reshape(-1, PAGE_SIZE, MOBA_HEADS, HEAD_DIM)
    k_mean = jnp.mean(k_all[page_table].reshape(B, nb, MOBA_BLOCK, MOBA_HEADS, HEAD_DIM), axis=2)
    q = mq.reshape(B, MOBA_HEADS, HEAD_DIM)
    score = jnp.einsum('bhd,bnhd->bhn', q, k_mean)
    _, top = lax.top_k(score, MOBA_TOPK)
    pg = page_table[bidx[:, None, None, None], top[..., None] * pages_per_blk + jnp.arange(pages_per_blk)]
    hidx = jnp.arange(MOBA_HEADS)[None, :, None, None]
    kg = k_all[pg, :, hidx].reshape(B, MOBA_HEADS, MOBA_TOPK * MOBA_BLOCK, HEAD_DIM)
    vg = v_all[pg, :, hidx].reshape(B, MOBA_HEADS, MOBA_TOPK * MOBA_BLOCK, HEAD_DIM)
    kpos = (top[..., None] * MOBA_BLOCK + jnp.arange(MOBA_BLOCK)).reshape(B, MOBA_HEADS, -1)
    kg = jnp.concatenate([kg, mk.reshape(B, MOBA_HEADS, 1, HEAD_DIM)], axis=2)
    vg = jnp.concatenate([vg, mv.reshape(B, MOBA_HEADS, 1, HEAD_DIM)], axis=2)
    dist = past - jnp.concatenate([kpos, jnp.full((B, MOBA_HEADS, 1), past)], axis=2)
    logits = jnp.einsum('bhd,bhkd->bhk', q, kg) * SCALE
    logits = logits + jnp.take_along_axis(jnp.broadcast_to(tab[None, :MOBA_HEADS], (B, MOBA_HEADS, T5_BUCKETS)),
                                          _t5_bucket(dist), axis=2)
    p = jax.nn.softmax(logits, axis=-1)
    o_a = jnp.einsum('bhk,bhkd->bhd', p, vg).reshape(B, MOBA_W)

    tab_n = tab[MOBA_HEADS:]
    qn = nq.reshape(B, NSA_HEADS, HEAD_DIM)
    n_ch = past // CMP_STRIDE
    n_cmp = n_ch - 1
    c0 = 2 * MOBA_W

    def compress(col, pe, w):
        rows = cache_kv[:, :, col:col + HEAD_DIM][page_table].reshape(B, n_ch, CMP_STRIDE, HEAD_DIM)
        first = jnp.einsum('bcld,lde->bce', rows + pe[:CMP_STRIDE], w[:CMP_STRIDE])
        second = jnp.einsum('bcld,lde->bce', rows + pe[CMP_STRIDE:], w[CMP_STRIDE:])
        return first[:, :-1] + second[:, 1:]

    k_cmp = compress(c0, lp['cmp_pos_k'], lp['w_cmp_k'])
    v_cmp = compress(c0 + HEAD_DIM, lp['cmp_pos_v'], lp['w_cmp_v'])
    d_cmp = past - (jnp.arange(n_cmp) * CMP_STRIDE + CMP_LEN - 1)
    logits = jnp.einsum('bhd,bnd->bhn', qn, k_cmp) * SCALE + tab_n[:, _t5_bucket(d_cmp)][None]
    p_cmp = _softmax_rows(logits, jnp.broadcast_to(d_cmp >= 0, logits.shape))
    o_cmp = jnp.einsum('bhn,bnd->bhd', p_cmp, v_cmp)

    n_slc = past // SLC_BLOCK
    c_start = jnp.arange(n_cmp) * CMP_STRIDE
    s_start = jnp.arange(n_slc) * SLC_BLOCK
    overlap = ((c_start[:, None] < s_start[None, :] + SLC_BLOCK)
               & (c_start[:, None] + CMP_LEN > s_start[None, :])).astype(F32)
    imp = jnp.einsum('bhn,ns->bs', p_cmp, overlap)
    _, top = lax.top_k(imp, SLC_TOPK - 1)
    per_page = PAGE_SIZE // SLC_BLOCK
    pg = page_table[bidx[:, None], top // per_page]
    r0 = (top % per_page) * SLC_BLOCK
    rows = r0[..., None] + jnp.arange(SLC_BLOCK)
    ks = cache_kv[pg[..., None], rows, c0 + 2 * HEAD_DIM:c0 + 3 * HEAD_DIM].reshape(B, -1, HEAD_DIM)
    vs = cache_kv[pg[..., None], rows, c0 + 3 * HEAD_DIM:c0 + 4 * HEAD_DIM].reshape(B, -1, HEAD_DIM)
    ks = jnp.concatenate([ks, sk[:, None, :]], axis=1)
    vs = jnp.concatenate([vs, sv[:, None, :]], axis=1)
    kpos = (top[..., None] * SLC_BLOCK + jnp.arange(SLC_BLOCK)).reshape(B, -1)
    dist = past - jnp.concatenate([kpos, jnp.full((B, 1), past)], axis=1)
    logits = jnp.einsum('bhd,bkd->bhk', qn, ks) * SCALE + tab_n[:, _t5_bucket(dist)].transpose(1, 0, 2)
    p = jax.nn.softmax(logits, axis=-1)
    o_slc = jnp.einsum('bhk,bkd->bhd', p, vs)

    n_win = win_buf.shape[1]
    full_win = jnp.concatenate([win_buf, jnp.concatenate([wk, wv], axis=-1)[:, None, :]], axis=1)
    dist = n_win - jnp.arange(n_win + 1)
    mask = (dist >= 0) & (dist < WINDOW)
    logits = jnp.einsum('bhd,bkd->bhk', qn, full_win[..., :HEAD_DIM]) * SCALE
    logits = logits + tab_n[:, _t5_bucket(dist)][None]
    p = _softmax_rows(logits, jnp.broadcast_to(mask, logits.shape))
    o_win = jnp.einsum('bhk,bkd->bhd', p, full_win[..., HEAD_DIM:])
    gt = jax.nn.sigmoid(ng).reshape(B, NSA_HEADS, 3)
    o_b = (gt[..., 0:1] * o_cmp + gt[..., 1:2] * o_slc + gt[..., 2:3] * o_win).reshape(B, NSA_W)

    xc = jnp.concatenate([conv0, lx[:, None, :]], axis=1)
    u = lp['conv_b']
    for j in range(CONV_W):
        u = u + xc[:, j] * lp['conv_w'][j]
    ub = u.reshape(B, LRU_BLOCKS, LRU_BLOCK)
    r = jax.nn.sigmoid(jnp.einsum('bnk,nkj->bnj', ub, lp['w_rg']).reshape(B, LRU_WIDTH) + lp['b_rg'])
    ig = jax.nn.sigmoid(jnp.einsum('bnk,nkj->bnj', ub, lp['w_ig']).reshape(B, LRU_WIDTH) + lp['b_ig'])
    log_a = -LRU_C * r * jax.nn.softplus(-lp['lru_lambda'])
    h_new = jnp.exp(log_a) * h0 + jnp.sqrt(1.0 - jnp.exp(2.0 * log_a)) * (ig * u)
    o_c = h_new * jax.nn.gelu(lg)

    merged = (jax.nn.sigmoid(ga) * _dot_hi(o_a, lp['w_br_moba']) + jax.nn.sigmoid(gb) * _dot_hi(o_b, lp['w_br_nsa'])
              + jax.nn.sigmoid(gc) * _dot_hi(o_c, lp['w_br_lru']))
    x = x + g1 * _dot_hi(merged, lp['w_out'])
    h2 = _rms(x, lp['norm_ffn']) * (1.0 + sc2) + sh2
    e_idx, w_top = _route(_dot_hi(h2, w_router) + b_router)
    comb = jnp.sum(jax.nn.one_hot(e_idx, N_EXPERTS, dtype=F32) * w_top[..., None], axis=1)
    y = jnp.zeros_like(h2)
    for e in range(N_EXPERTS):
        act = jax.nn.silu(_dot_hi(h2, lp['w_e_gate'][e])) * _dot_hi(h2, lp['w_e_up'][e])
        y = y + comb[:, e:e + 1] * _dot_hi(act, lp['w_e_down'][e])
    x = x + g2 * y
    return x, new_kv, full_win[:, full_win.shape[1] - min(WINDOW, full_win.shape[1]):], h_new, xc[:, 1:]


def kernel(x_prompt, x_sample, cache_kv, cache_win, state_lru_h, state_lru_conv, page_table, c_prompt, c_sample,
           w_ada, b_ada, norm_mix, norm_ffn, w_in, cmp_pos_k, cmp_pos_v, w_cmp_k, w_cmp_v, conv_w, conv_b,
           w_rg, b_rg, w_ig, b_ig, lru_lambda, w_br_moba, w_br_nsa, w_br_lru, w_out, w_e_gate, w_e_up, w_e_down,
           t5_bias, w_router, b_router, norm_final):
    n_p, T, D = x_prompt.shape
    n_s = x_sample.shape[0]
    xp = x_prompt
    xs = x_sample.reshape(n_s, D)
    tabs = _prompt_tables(t5_bias, T)
    w_router_pad = jnp.pad(w_router, ((0, 0), (0, LANE - N_EXPERTS)))
    c_all = jax.nn.silu(jnp.concatenate([c_prompt, c_sample], axis=0))
    c_all = jnp.pad(c_all, ((0, 16 - n_p - n_s), (0, 0)))
    outs = [[] for _ in range(8)]
    for l in range(DEPTH):
        lp = {'norm_mix': norm_mix[l], 'norm_ffn': norm_ffn[l], 'w_in': w_in[l],
              'cmp_pos_k': cmp_pos_k[l], 'cmp_pos_v': cmp_pos_v[l], 'w_cmp_k': w_cmp_k[l], 'w_cmp_v': w_cmp_v[l],
              'conv_w': conv_w[l], 'conv_b': conv_b[l], 'w_rg': w_rg[l], 'b_rg': b_rg[l], 'w_ig': w_ig[l],
              'b_ig': b_ig[l], 'lru_lambda': lru_lambda[l], 'w_br_moba': w_br_moba[l], 'w_br_nsa': w_br_nsa[l],
              'w_br_lru': w_br_lru[l], 'w_out': w_out[l], 'w_e_gate': w_e_gate[l], 'w_e_up': w_e_up[l],
              'w_e_down': w_e_down[l]}
        mod = _matmul(c_all, w_ada[l], 16, 2048) + b_ada[l]
        xp, kv_p, win_p, h_p, conv_p = _layer_prompt(xp, mod[:n_p], lp, tabs, w_router_pad, b_router)
        xs, kv_s, win_s, h_s, conv_s = _layer_sample(xs, mod[n_p:n_p + n_s], cache_kv[l], page_table, cache_win[l],
                                                     state_lru_h[l], state_lru_conv[l], lp, t5_bias, w_router,
                                                     b_router)
        for lst, val in zip(outs, (kv_p, kv_s[:, None, :], win_p, win_s, h_p, h_s, conv_p, conv_s)):
            lst.append(val)
    y_prompt = _rms(xp, norm_final)
    y_sample = _rms(xs, norm_final).reshape(n_s, 1, D)
    return (y_prompt, y_sample) + tuple(jnp.stack(o) for o in outs)
```

```python
import functools
import math

import jax
import jax.numpy as jnp
from jax import lax
from jax.experimental import pallas as pl
from jax.experimental.pallas import tpu as pltpu

D_MODEL = 2048
DEPTH = 2
PAGE_SIZE = 128
HEAD_DIM = 128
MOBA_HEADS = 4
MOBA_BLOCK = 256
MOBA_TOPK = 3
NSA_HEADS = 4
CMP_STRIDE = 16
CMP_LEN = 32
SLC_BLOCK = 64
SLC_TOPK = 16
WINDOW = 512
LRU_WIDTH = D_MODEL // 2
LRU_BLOCKS = 8
LRU_BLOCK = LRU_WIDTH // LRU_BLOCKS
LRU_C = 8.0
CONV_W = 4
N_EXPERTS = 16
N_GROUPS = 4
EXPERTS_PER_GROUP = N_EXPERTS // N_GROUPS
TOP_K = 2
D_EXPERT = D_MODEL // 2
T5_BUCKETS = 32
T5_MAX_DIST = 128
EPS = 1e-6
NEG = -1e30

MOBA_W = MOBA_HEADS * HEAD_DIM
NSA_W = NSA_HEADS * HEAD_DIM
KV_DIM = 2 * MOBA_W + 4 * HEAD_DIM
WIN_DIM = 2 * HEAD_DIM
IN_WIDTHS = (MOBA_W, MOBA_W, MOBA_W, NSA_W, HEAD_DIM, HEAD_DIM, HEAD_DIM, HEAD_DIM, HEAD_DIM, HEAD_DIM,
             3 * NSA_HEADS, LRU_WIDTH, LRU_WIDTH, D_MODEL, D_MODEL, D_MODEL)
COL_NG = 3 * MOBA_W + NSA_W + 6 * HEAD_DIM
COL_LX = COL_NG + 3 * NSA_HEADS
ZA_W = COL_NG + HEAD_DIM
SCALE = HEAD_DIM ** -0.5

LANE = 128
ATT_TILE = 256
MOE_TILE = 256
VMEM_LIMIT = 56 * 1024 * 1024

F32 = jnp.float32
BF16 = jnp.bfloat16


def _cparams(sem):
    return pltpu.CompilerParams(dimension_semantics=sem, vmem_limit_bytes=VMEM_LIMIT)


def _mm_kernel(x_ref, w_ref, o_ref):
    o_ref[...] = jnp.dot(x_ref[...].astype(BF16), w_ref[...].astype(BF16), preferred_element_type=F32)


def _matmul(x, w, tm, tn):
    M, K = x.shape
    N = w.shape[1]
    assert M % tm == 0 and N % tn == 0
    return pl.pallas_call(
        _mm_kernel,
        grid=(N // tn, M // tm),
        in_specs=[pl.BlockSpec((tm, K), lambda j, i: (i, 0)),
                  pl.BlockSpec((K, tn), lambda j, i: (0, j))],
        out_specs=pl.BlockSpec((tm, tn), lambda j, i: (i, j)),
        out_shape=jax.ShapeDtypeStruct((M, N), F32),
        compiler_params=_cparams(("parallel", "parallel")),
        name="matmul",
    )(x, w)


def _rank_select(score, own, nblk, topk):
    lane = lax.broadcasted_iota(jnp.int32, score.shape, 1)
    past = lane < own
    s = jnp.where(past, score, NEG)
    rank = jnp.zeros(score.shape, jnp.int32)
    for j in range(nblk):
        col = s[:, j:j + 1]
        beats = (col > s) | ((col == s) & (lane > j))
        rank = rank + beats.astype(jnp.int32)
    return (past & (rank < topk)) | (lane == own)


def _moba_select_kernel(q_ref, k_ref, sel_ref):
    T = q_ref.shape[1]
    nb = T // MOBA_BLOCK
    k = k_ref[0]
    kmean = jnp.sum(k.reshape(nb, MOBA_BLOCK, HEAD_DIM), axis=1) * (1.0 / MOBA_BLOCK)
    kmean = jnp.concatenate([kmean, jnp.zeros((LANE - nb, HEAD_DIM), F32)], axis=0)
    score = lax.dot_general(q_ref[0].astype(BF16), kmean.astype(BF16), (((1,), (1,)), ((), ())),
                            preferred_element_type=F32)
    own = lax.broadcasted_iota(jnp.int32, (T, 1), 0) // MOBA_BLOCK
    sel = _rank_select(score, own, nb, MOBA_TOPK)
    sel_ref[0, 0] = sel.astype(BF16)


def _moba_select(za, B, T):
    return pl.pallas_call(
        _moba_select_kernel,
        grid=(B, MOBA_HEADS),
        in_specs=[pl.BlockSpec((1, T, HEAD_DIM), lambda b, h: (b, 0, h)),
                  pl.BlockSpec((1, T, HEAD_DIM), lambda b, h: (b, 0, MOBA_HEADS + h))],
        out_specs=pl.BlockSpec((1, 1, T, LANE), lambda b, h: (b, h, 0, 0)),
        out_shape=jax.ShapeDtypeStruct((B, MOBA_HEADS, T, LANE), BF16),
        compiler_params=_cparams(("parallel", "parallel")),
        name="moba_select",
    )(za, za)


def _flash_kernel(*refs, G, window_tiles, use_sel):
    if use_sel:
        q_ref, k_ref, v_ref, bias_ref, sel_ref, e_ref, o_ref, m_scr, l_scr, acc_scr = refs
    else:
        q_ref, k_ref, v_ref, bias_ref, o_ref, m_scr, l_scr, acc_scr = refs
    qi = pl.program_id(2)
    kj = pl.program_id(3)
    nk = pl.num_programs(3)
    tq, tk = ATT_TILE, ATT_TILE

    @pl.when(kj == 0)
    def _():
        m_scr[...] = jnp.full(m_scr.shape, NEG, F32)
        l_scr[...] = jnp.zeros(l_scr.shape, F32)
        acc_scr[...] = jnp.zeros(acc_scr.shape, F32)

    lo = jnp.maximum(qi - window_tiles, 0) if window_tiles is not None else 0

    @pl.when((kj >= lo) & (kj <= qi))
    def _():
        k = k_ref[0].astype(BF16)
        v = v_ref[0].astype(BF16)
        row = lax.broadcasted_iota(jnp.int32, (tq, tk), 0)
        col = lax.broadcasted_iota(jnp.int32, (tq, tk), 1)
        dist = (qi - kj) * tq + row - col
        mask = dist >= 0
        if window_tiles is not None:
            mask = mask & (dist < WINDOW)
        if use_sel:
            hit = jnp.dot(sel_ref[0, 0], e_ref[...], preferred_element_type=F32)
            mask = mask & (hit > 0.5)
        for g in range(G):
            q = q_ref[0, :, g * HEAD_DIM:(g + 1) * HEAD_DIM].astype(BF16)
            s = lax.dot_general(q, k, (((1,), (1,)), ((), ())), preferred_element_type=F32)
            s = s * SCALE + bias_ref[g, 0]
            s = jnp.where(mask, s, NEG)
            m_prev = m_scr[g]
            m_new = jnp.maximum(m_prev, jnp.max(s, axis=1, keepdims=True))
            p = jnp.where(mask, jnp.exp(s - m_new), 0.0)
            alpha = jnp.exp(m_prev - m_new)
            l_scr[g] = alpha * l_scr[g] + jnp.sum(p, axis=1, keepdims=True)
            acc_scr[g] = alpha * acc_scr[g] + jnp.dot(p.astype(BF16), v, preferred_element_type=F32)
            m_scr[g] = m_new

    @pl.when(kj == nk - 1)
    def _():
        for g in range(G):
            l = l_scr[g]
            o = acc_scr[g] / jnp.where(l > 0.0, l, 1.0)
            o_ref[0, :, g * HEAD_DIM:(g + 1) * HEAD_DIM] = o


def _flash(za, bias, *, B, T, n_kv, G, q_col, k_col, v_col, bias_blk, window_tiles=None, sel=None, emat=None):
    tq = tk = ATT_TILE
    nq, nk = T // tq, T // tk
    use_sel = sel is not None

    def kv_blk(i, j):
        lo = jnp.maximum(i - window_tiles, 0) if window_tiles is not None else 0
        return jnp.clip(j, lo, i)

    in_specs = [
        pl.BlockSpec((1, tq, G * HEAD_DIM), lambda b, h, i, j: (b, i, q_col // G + h)),
        pl.BlockSpec((1, tk, HEAD_DIM), lambda b, h, i, j: (b, kv_blk(i, j), k_col + h)),
        pl.BlockSpec((1, tk, HEAD_DIM), lambda b, h, i, j: (b, kv_blk(i, j), v_col + h)),
        pl.BlockSpec((G, 1, tq, tk), lambda b, h, i, j: (bias_blk + h, jnp.clip(i - j, 0, 2), 0, 0)),
    ]
    args = [za, za, za, bias]
    if use_sel:
        in_specs += [pl.BlockSpec((1, 1, tq, LANE), lambda b, h, i, j: (b, h, i, 0)),
                     pl.BlockSpec((LANE, tk), lambda b, h, i, j: (0, kv_blk(i, j)))]
        args += [sel, emat]
    return pl.pallas_call(
        functools.partial(_flash_kernel, G=G, window_tiles=window_tiles, use_sel=use_sel),
        grid=(B, n_kv, nq, nk),
        in_specs=in_specs,
        out_specs=pl.BlockSpec((1, tq, G * HEAD_DIM), lambda b, h, i, j: (b, i, h)),
        out_shape=jax.ShapeDtypeStruct((B, T, n_kv * G * HEAD_DIM), F32),
        scratch_shapes=[pltpu.VMEM((G, tq, 1), F32), pltpu.VMEM((G, tq, 1), F32),
                        pltpu.VMEM((G, tq, HEAD_DIM), F32)],
        compiler_params=_cparams(("parallel", "parallel", "parallel", "arbitrary")),
        name="flash",
    )(*args)


def _cmp_kernel(ck_ref, cv_ref, q_ref, pek_ref, pev_ref, wk_ref, wv_ref, bias_ref, ovl_ref,
                o_ref, sel_ref, kc_scr, vc_scr):
    T = ck_ref.shape[1]
    n_ch = T // CMP_STRIDE
    i = pl.program_id(1)
    tq = ATT_TILE

    @pl.when(i == 0)
    def _():
        for src, pe, w, dst in ((ck_ref, pek_ref, wk_ref, kc_scr), (cv_ref, pev_ref, wv_ref, vc_scr)):
            first = jnp.zeros((n_ch, HEAD_DIM), F32)
            second = jnp.zeros((n_ch, HEAD_DIM), F32)
            for l in range(CMP_STRIDE):
                x = src[0, pl.ds(l, n_ch, stride=CMP_STRIDE), :]
                l2 = CMP_STRIDE + l
                first = first + jnp.dot((x + pe[l:l + 1, :]).astype(BF16), w[l * HEAD_DIM:(l + 1) * HEAD_DIM, :],
                                        preferred_element_type=F32)
                second = second + jnp.dot((x + pe[l2:l2 + 1, :]).astype(BF16),
                                          w[l2 * HEAD_DIM:(l2 + 1) * HEAD_DIM, :], preferred_element_type=F32)
            dst[...] = (first + pltpu.roll(second, n_ch - 1, 0)).astype(BF16)

    kc = kc_scr[...]
    vc = vc_scr[...]
    t = i * tq + lax.broadcasted_iota(jnp.int32, (tq, n_ch), 0)
    n = lax.broadcasted_iota(jnp.int32, (tq, n_ch), 1)
    vis = (n * CMP_STRIDE + (CMP_LEN - 1)) <= t
    imp = jnp.zeros((tq, LANE), F32)
    ovl = ovl_ref[...]
    for g in range(NSA_HEADS):
        q = q_ref[0, :, g * HEAD_DIM:(g + 1) * HEAD_DIM].astype(BF16)
        s = lax.dot_general(q, kc, (((1,), (1,)), ((), ())), preferred_element_type=F32)
        s = s * SCALE + bias_ref[g]
        s = jnp.where(vis, s, NEG)
        m = jnp.max(s, axis=1, keepdims=True)
        e = jnp.where(vis, jnp.exp(s - m), 0.0)
        l = jnp.sum(e, axis=1, keepdims=True)
        p = e / jnp.where(l > 0.0, l, 1.0)
        pb = p.astype(BF16)
        o_ref[0, :, g * HEAD_DIM:(g + 1) * HEAD_DIM] = jnp.dot(pb, vc, preferred_element_type=F32)
        imp = imp + jnp.dot(pb, ovl, preferred_element_type=F32)
    own = (i * tq + lax.broadcasted_iota(jnp.int32, (tq, 1), 0)) // SLC_BLOCK
    sel = _rank_select(imp, own, T // SLC_BLOCK, SLC_TOPK - 1)
    sel_ref[0, 0] = sel.astype(BF16)


def _cmp_branch(za, pe_k, pe_v, w_k, w_v, bias_cmp, overlap, B, T):
    assert T // CMP_STRIDE == LANE
    tq = ATT_TILE
    col = 3 * MOBA_HEADS + NSA_HEADS
    return pl.pallas_call(
        _cmp_kernel,
        grid=(B, T // tq),
        in_specs=[
            pl.BlockSpec((1, T, HEAD_DIM), lambda b, i: (b, 0, col)),
            pl.BlockSpec((1, T, HEAD_DIM), lambda b, i: (b, 0, col + 1)),
            pl.BlockSpec((1, tq, NSA_W), lambda b, i: (b, i, 3)),
            pl.BlockSpec((CMP_LEN, HEAD_DIM), lambda b, i: (0, 0)),
            pl.BlockSpec((CMP_LEN, HEAD_DIM), lambda b, i: (0, 0)),
            pl.BlockSpec((CMP_LEN * HEAD_DIM, HEAD_DIM), lambda b, i: (0, 0)),
            pl.BlockSpec((CMP_LEN * HEAD_DIM, HEAD_DIM), lambda b, i: (0, 0)),
            pl.BlockSpec((NSA_HEADS, tq, LANE), lambda b, i: (0, i, 0)),
            pl.BlockSpec((LANE, LANE), lambda b, i: (0, 0)),
        ],
        out_specs=[pl.BlockSpec((1, tq, NSA_W), lambda b, i: (b, i, 0)),
                   pl.BlockSpec((1, 1, tq, LANE), lambda b, i: (b, 0, i, 0))],
        out_shape=[jax.ShapeDtypeStruct((B, T, NSA_W), F32),
                   jax.ShapeDtypeStruct((B, 1, T, LANE), BF16)],
        scratch_shapes=[pltpu.VMEM((LANE, HEAD_DIM), BF16), pltpu.VMEM((LANE, HEAD_DIM), BF16)],
        compiler_params=_cparams(("parallel", "arbitrary")),
        name="nsa_cmp",
    )(za, za, za, pe_k, pe_v, w_k, w_v, bias_cmp, overlap)


def _gelu_tanh(x):
    return 0.5 * x * (1.0 + jnp.tanh(math.sqrt(2.0 / math.pi) * (x + 0.044715 * (x * x * x))))


def _softplus(x):
    return jnp.maximum(x, 0.0) + jnp.log1p(jnp.exp(-jnp.abs(x)))


def _lru_kernel(x_ref, g_ref, c0_ref, h0_ref, cw_ref, cb_ref, wr_ref, br_ref, wi_ref, bi_ref, lam_ref,
                y_ref, hl_ref, cl_ref, a_scr, x_scr):
    T = x_ref.shape[1]
    x = x_ref[0]
    row = lax.broadcasted_iota(jnp.int32, (T, LRU_BLOCK), 0)
    c0 = c0_ref[0]

    def shifted(d):
        r = pltpu.roll(x, d, 0)
        for t in range(d):
            r = jnp.where(row == t, c0[t + CONV_W - 1 - d:t + CONV_W - d, :], r)
        return r

    u = cb_ref[...] + shifted(3) * cw_ref[0:1, :]
    u = u + shifted(2) * cw_ref[1:2, :]
    u = u + shifted(1) * cw_ref[2:3, :]
    u = u + x * cw_ref[3:4, :]
    ub = u.astype(BF16)
    r = jax.nn.sigmoid(jnp.dot(ub, wr_ref[0], preferred_element_type=F32) + br_ref[...])
    ig = jax.nn.sigmoid(jnp.dot(ub, wi_ref[0], preferred_element_type=F32) + bi_ref[...])
    log_a = (-LRU_C * r) * _softplus(-lam_ref[...])
    a = jnp.exp(log_a)
    b = jnp.sqrt(1.0 - jnp.exp(2.0 * log_a)) * (ig * u)
    sub = row % 8
    for d in (1, 2, 4):
        ok = sub >= d
        b = jnp.where(ok, a * pltpu.roll(b, d, 0) + b, b)
        a = jnp.where(ok, a * pltpu.roll(a, d, 0), a)
    a_scr[...] = a
    x_scr[...] = b

    def group(k, carry):
        s = pl.multiple_of(k * 8, 8)
        h = x_scr[pl.ds(s, 8), :] + a_scr[pl.ds(s, 8), :] * carry
        x_scr[pl.ds(s, 8), :] = h
        return h[7:8, :]

    h_last = lax.fori_loop(0, T // 8, group, h0_ref[0], unroll=8)
    y_ref[0] = x_scr[...] * _gelu_tanh(g_ref[0])
    hl_ref[0] = h_last
    cl_ref[0] = x[T - (CONV_W - 1):, :]


def _rglru(zb, conv0, h0, conv_w, conv_b, w_r, b_r, w_i, b_i, lam, B, T):
    nblk = LRU_BLOCKS
    vec = lambda a: a.reshape(1, LRU_WIDTH)
    row_spec = pl.BlockSpec((1, LRU_BLOCK), lambda b, n: (0, n))
    return pl.pallas_call(
        _lru_kernel,
        grid=(B, nblk),
        in_specs=[
            pl.BlockSpec((1, T, LRU_BLOCK), lambda b, n: (b, 0, n)),
            pl.BlockSpec((1, T, LRU_BLOCK), lambda b, n: (b, 0, nblk + n)),
            pl.BlockSpec((1, CONV_W - 1, LRU_BLOCK), lambda b, n: (b, 0, n)),
            pl.BlockSpec((1, 1, LRU_BLOCK), lambda b, n: (b, 0, n)),
            pl.BlockSpec((CONV_W, LRU_BLOCK), lambda b, n: (0, n)),
            row_spec,
            pl.BlockSpec((1, LRU_BLOCK, LRU_BLOCK), lambda b, n: (n, 0, 0)),
            row_spec,
            pl.BlockSpec((1, LRU_BLOCK, LRU_BLOCK), lambda b, n: (n, 0, 0)),
            row_spec,
            row_spec,
        ],
        out_specs=[pl.BlockSpec((1, T, LRU_BLOCK), lambda b, n: (b, 0, n)),
                   pl.BlockSpec((1, 1, LRU_BLOCK), lambda b, n: (b, 0, n)),
                   pl.BlockSpec((1, CONV_W - 1, LRU_BLOCK), lambda b, n: (b, 0, n))],
        out_shape=[jax.ShapeDtypeStruct((B, T, LRU_WIDTH), F32),
                   jax.ShapeDtypeStruct((B, 1, LRU_WIDTH), F32),
                   jax.ShapeDtypeStruct((B, CONV_W - 1, LRU_WIDTH), F32)],
        scratch_shapes=[pltpu.VMEM((T, LRU_BLOCK), F32), pltpu.VMEM((T, LRU_BLOCK), F32)],
        compiler_params=_cparams(("parallel", "parallel")),
        name="rglru",
    )(zb, zb, conv0, h0.reshape(B, 1, LRU_WIDTH), conv_w, vec(conv_b), w_r.astype(BF16), vec(b_r),
      w_i.astype(BF16), vec(b_i), vec(lam))


def _moe_kernel(eid_ref, nt_ref, x_ref, wg_ref, wu_ref, wd_ref, cw_ref, o_ref):
    t = pl.program_id(0)

    @pl.when(t < nt_ref[0])
    def _():
        x = x_ref[...]
        g = jnp.dot(x, wg_ref[0], preferred_element_type=F32)
        u = jnp.dot(x, wu_ref[0], preferred_element_type=F32)
        act = (g * jax.nn.sigmoid(g)) * u
        y = jnp.dot(act.astype(BF16), wd_ref[0], preferred_element_type=F32)
        o_ref[...] = y * cw_ref[...]

    @pl.when(t >= nt_ref[0])
    def _():
        o_ref[...] = jnp.zeros(o_ref.shape, F32)


def _moe_ffn(x_pad, cw_pad, tile_eid, n_tiles, w_g, w_u, w_d):
    P, D = x_pad.shape
    tm = MOE_TILE
    grid_spec = pltpu.PrefetchScalarGridSpec(
        num_scalar_prefetch=2,
        grid=(P // tm,),
        in_specs=[
            pl.BlockSpec((tm, D), lambda t, eid, nt: (t, 0)),
            pl.BlockSpec((1, D, D_EXPERT), lambda t, eid, nt: (eid[t], 0, 0)),
            pl.BlockSpec((1, D, D_EXPERT), lambda t, eid, nt: (eid[t], 0, 0)),
            pl.BlockSpec((1, D_EXPERT, D), lambda t, eid, nt: (eid[t], 0, 0)),
            pl.BlockSpec((tm, 1), lambda t, eid, nt: (t, 0)),
        ],
        out_specs=pl.BlockSpec((tm, D), lambda t, eid, nt: (t, 0)),
    )
    return pl.pallas_call(
        _moe_kernel,
        grid_spec=grid_spec,
        out_shape=jax.ShapeDtypeStruct((P, D), F32),
        compiler_params=_cparams(("arbitrary",)),
        name="moe_ffn",
    )(tile_eid, n_tiles, x_pad, w_g, w_u, w_d, cw_pad)


def _route(logits):
    probs = jax.nn.softmax(logits.astype(F32), axis=-1)
    grp = probs.reshape(-1, N_GROUPS, EXPERTS_PER_GROUP)
    g_score = lax.top_k(grp, TOP_K)[0].sum(-1)
    g_best = jnp.argmax(g_score, axis=-1)
    pick = g_best[:, None, None] == jnp.arange(N_GROUPS)[None, :, None]
    in_grp = jnp.sum(jnp.where(pick, grp, 0.0), axis=1)
    w_top, i_top = lax.top_k(in_grp, TOP_K)
    w_top = w_top / jnp.sum(w_top, axis=-1, keepdims=True)
    e_idx = g_best[:, None] * EXPERTS_PER_GROUP + i_top
    return e_idx.astype(jnp.int32), w_top


def _moe_prompt(h2, w_router_pad, b_router, w_g, w_u, w_d):
    N, D = h2.shape
    tm = MOE_TILE
    logits = _matmul(h2, w_router_pad, 512, LANE)[:, :N_EXPERTS] + b_router
    e_idx, w_top = _route(logits)
    flat_e = e_idx.reshape(-1)
    onehot = (flat_e[:, None] == jnp.arange(N_EXPERTS)[None, :]).astype(jnp.int32)
    within = jnp.sum(onehot * jnp.cumsum(onehot, axis=0), axis=1) - 1
    sizes = jnp.sum(onehot, axis=0)
    padded = ((sizes + tm - 1) // tm) * tm
    pend = jnp.cumsum(padded)
    pstart = pend - padded
    dest = jnp.sum(onehot * pstart[None, :], axis=1) + within
    P = 2 * N + N_EXPERTS * tm
    src_tok = jnp.zeros((P,), jnp.int32).at[dest].set(jnp.arange(2 * N, dtype=jnp.int32) // TOP_K)
    cw_pad = jnp.zeros((P,), F32).at[dest].set(w_top.reshape(-1))
    x_pad = h2.astype(BF16)[src_tok]
    tile_start = jnp.arange(P // tm, dtype=jnp.int32) * tm
    tile_eid = jnp.minimum(jnp.sum(tile_start[:, None] >= pend[None, :], axis=1), N_EXPERTS - 1).astype(jnp.int32)
    n_tiles = (pend[-1] // tm).astype(jnp.int32).reshape(1)
    out = _moe_ffn(x_pad, cw_pad.reshape(P, 1), tile_eid, n_tiles, w_g, w_u, w_d)
    pos = dest.reshape(N, TOP_K)
    return out[pos[:, 0]] + out[pos[:, 1]]


def _t5_bucket(dist):
    n = jnp.maximum(dist, 0)
    exact = T5_BUCKETS // 2
    nf = jnp.maximum(n, 1).astype(F32)
    large = exact + (jnp.log(nf / exact) / math.log(T5_MAX_DIST / exact) * (T5_BUCKETS - exact)).astype(jnp.int32)
    return jnp.where(n < exact, n, jnp.minimum(large, T5_BUCKETS - 1))


def _bias_lookup(tab, dist):
    bucket = _t5_bucket(dist)
    out = jnp.zeros((tab.shape[0],) + bucket.shape, F32)
    for b in range(T5_BUCKETS):
        out = jnp.where(bucket[None] == b, tab[:, b].reshape((-1,) + (1,) * bucket.ndim), out)
    return out


def _rms(x, g):
    return x * lax.rsqrt(jnp.mean(x * x, axis=-1, keepdims=True) + EPS) * g


def _split(z, widths):
    outs, s = [], 0
    for w in widths:
        outs.append(z[..., s:s + w])
        s += w
    return outs


def _prompt_tables(t5_bias, T):
    t = ATT_TILE
    i = jnp.arange(t)
    dist = jnp.arange(3)[:, None, None] * t + i[None, :, None] - i[None, None, :]
    bias_tiles = _bias_lookup(t5_bias.T, dist)
    pos = jnp.arange(T)
    lane = jnp.arange(LANE)
    e_moba = (pos[None, :] // MOBA_BLOCK == lane[:, None]).astype(BF16)
    e_slc = (pos[None, :] // SLC_BLOCK == lane[:, None]).astype(BF16)
    d_cmp = pos[:, None] - (lane[None, :] * CMP_STRIDE + CMP_LEN - 1)
    bias_cmp = _bias_lookup(t5_bias.T[MOBA_HEADS:], d_cmp)
    c_start = lane * CMP_STRIDE
    s_start = lane * SLC_BLOCK
    n_cmp = T // CMP_STRIDE - 1
    overlap = ((c_start[:, None] < s_start[None, :] + SLC_BLOCK) & (c_start[:, None] + CMP_LEN > s_start[None, :])
               & (lane[:, None] < n_cmp) & (lane[None, :] < T // SLC_BLOCK)).astype(BF16)
    return bias_tiles, e_moba, e_slc, bias_cmp, overlap


def _layer_prompt(x, mod, lp, tabs, w_router_pad, b_router):
    B, T, D = x.shape
    N = B * T
    bias_tiles, e_moba, e_slc, bias_cmp, overlap = tabs
    sh1, sc1, g1, sh2, sc2, g2 = [m[:, None, :] for m in jnp.split(mod, 6, axis=-1)]
    h = (_rms(x, lp['norm_mix']) * (1.0 + sc1) + sh1).astype(BF16).reshape(N, D)
    za = _matmul(h, lp['w_in'][:, :ZA_W].astype(BF16), 256, ZA_W).reshape(B, T, ZA_W)
    zb = _matmul(h, lp['w_in'][:, COL_LX:].astype(BF16), 512, 2048).reshape(B, T, -1)
    new_kv = jnp.concatenate([za[..., MOBA_W:3 * MOBA_W], za[..., 3 * MOBA_W + NSA_W:COL_NG - WIN_DIM]], axis=-1)
    win = za[:, T - WINDOW:, COL_NG - WIN_DIM:COL_NG]
    ng = za[..., COL_NG:COL_NG + 3 * NSA_HEADS]

    sel_moba = _moba_select(za, B, T)
    o_a = _flash(za, bias_tiles, B=B, T=T, n_kv=MOBA_HEADS, G=1, q_col=0, k_col=4, v_col=8, bias_blk=0,
                 sel=sel_moba, emat=e_moba)
    o_cmp, sel_slc = _cmp_branch(za, lp['cmp_pos_k'], lp['cmp_pos_v'],
                                 lp['w_cmp_k'].reshape(-1, HEAD_DIM).astype(BF16),
                                 lp['w_cmp_v'].reshape(-1, HEAD_DIM).astype(BF16), bias_cmp, overlap, B, T)
    o_slc = _flash(za, bias_tiles, B=B, T=T, n_kv=1, G=NSA_HEADS, q_col=12, k_col=18, v_col=19, bias_blk=1,
                   sel=sel_slc, emat=e_slc)
    o_win = _flash(za, bias_tiles, B=B, T=T, n_kv=1, G=NSA_HEADS, q_col=12, k_col=20, v_col=21, bias_blk=1,
                   window_tiles=WINDOW // ATT_TILE)
    gt = jax.nn.sigmoid(ng).reshape(B, T, NSA_HEADS, 3)
    o_b = (gt[..., 0:1] * o_cmp.reshape(B, T, NSA_HEADS, HEAD_DIM)
           + gt[..., 1:2] * o_slc.reshape(B, T, NSA_HEADS, HEAD_DIM)
           + gt[..., 2:3] * o_win.reshape(B, T, NSA_HEADS, HEAD_DIM)).reshape(B, T, NSA_W)

    o_c, h_last, conv_last = _rglru(zb, jnp.zeros((B, CONV_W - 1, LRU_WIDTH), F32), jnp.zeros((B, LRU_WIDTH), F32),
                                    lp['conv_w'], lp['conv_b'], lp['w_rg'], lp['b_rg'], lp['w_ig'], lp['b_ig'],
                                    lp['lru_lambda'], B, T)
    ga, gb, gc = _split(zb[..., 2 * LRU_WIDTH:], (D, D, D))
    pa = _matmul(o_a.astype(BF16).reshape(N, -1), lp['w_br_moba'].astype(BF16), 512, D)
    pb = _matmul(o_b.astype(BF16).reshape(N, -1), lp['w_br_nsa'].astype(BF16), 512, D)
    pc = _matmul(o_c.astype(BF16).reshape(N, -1), lp['w_br_lru'].astype(BF16), 512, D)
    merged = (jax.nn.sigmoid(ga).reshape(N, D) * pa + jax.nn.sigmoid(gb).reshape(N, D) * pb
              + jax.nn.sigmoid(gc).reshape(N, D) * pc)
    x = x + g1 * _matmul(merged.astype(BF16), lp['w_out'].astype(BF16), 512, D).reshape(B, T, D)
    h2 = (_rms(x, lp['norm_ffn']) * (1.0 + sc2) + sh2).reshape(N, D)
    moe = _moe_prompt(h2, w_router_pad, b_router, lp['w_e_gate'].astype(BF16), lp['w_e_up'].astype(BF16),
                      lp['w_e_down'].astype(BF16))
    x = x + g2 * moe.reshape(B, T, D)
    return x, new_kv, win, h_last.reshape(B, LRU_WIDTH), conv_last


def _dot_hi(a, b):
    return jnp.matmul(a, b)


def _softmax_rows(logits, mask):
    logits = jnp.where(mask, logits, NEG)
    p = jax.nn.softmax(logits, axis=-1)
    return jnp.where(mask, p, 0.0)


def _layer_sample(x, mod, cache_kv, page_table, win_buf, h0, conv0, lp, t5_bias, w_router, b_router):
    B, D = x.shape
    n_pages = page_table.shape[1]
    past = n_pages * PAGE_SIZE
    sh1, sc1, g1, sh2, sc2, g2 = jnp.split(mod, 6, axis=-1)
    h = _rms(x, lp['norm_mix']) * (1.0 + sc1) + sh1
    z = _dot_hi(h, lp['w_in'])
    mq, mk, mv, nq, ck, cv, sk, sv, wk, wv, ng, lx, lg, ga, gb, gc = _split(z, IN_WIDTHS)
    new_kv = jnp.concatenate([mk, mv, ck, cv, sk, sv], axis=-1)
    tab = t5_bias.T
    bidx = jnp.arange(B)

    nb = past // MOBA_BLOCK
    pages_per_blk = MOBA_BLOCK // PAGE_SIZE
    k_all = cache_kv[:, :, :MOBA_W].reshape(-1, PAGE_SIZE, MOBA_HEADS, HEAD_DIM)
    v_all = cache_kv[:, :, MOBA_W:2 * MOBA_W].reshape(-1, PAGE_SIZE, MOBA_HEADS, HEAD_DIM)
    k_mean = jnp.mean(k_all[page_table].reshape(B, nb, MOBA_BLOCK, MOBA_HEADS, HEAD_DIM), axis=2)
    q = mq.reshape(B, MOBA_HEADS, HEAD_DIM)
    score = jnp.einsum('bhd,bnhd->bhn', q, k_mean)
    _, top = lax.top_k(score, MOBA_TOPK)
    pg = page_table[bidx[:, None, None, None], top[..., None] * pages_per_blk + jnp.arange(pages_per_blk)]
    hidx = jnp.arange(MOBA_HEADS)[None, :, None, None]
    kg = k_all[pg, :, hidx].reshape(B, MOBA_HEADS, MOBA_TOPK * MOBA_BLOCK, HEAD_DIM)
    vg = v_all[pg, :, hidx].reshape(B, MOBA_HEADS, MOBA_TOPK * MOBA_BLOCK, HEAD_DIM)
    kpos = (top[..., None] * MOBA_BLOCK + jnp.arange(MOBA_BLOCK)).reshape(B, MOBA_HEADS, -1)
    kg = jnp.concatenate([kg, mk.reshape(B, MOBA_HEADS, 1, HEAD_DIM)], axis=2)
    vg = jnp.concatenate([vg, mv.reshape(B, MOBA_HEADS, 1, HEAD_DIM)], axis=2)
    dist = past - jnp.concatenate([kpos, jnp.full((B, MOBA_HEADS, 1), past)], axis=2)
    logits = jnp.einsum('bhd,bhkd->bhk', q, kg) * SCALE
    bucket = _t5_bucket(dist)
    for bkt in range(T5_BUCKETS):
        logits = logits + jnp.where(bucket == bkt, tab[None, :MOBA_HEADS, bkt, None], 0.0)
    p = jax.nn.softmax(logits, axis=-1)
    o_a = jnp.einsum('bhk,bhkd->bhd', p, vg).reshape(B, MOBA_W)

    tab_n = tab[MOBA_HEADS:]
    qn = nq.reshape(B, NSA_HEADS, HEAD_DIM)
    n_ch = past // CMP_STRIDE
    n_cmp = n_ch - 1
    c0 = 2 * MOBA_W

    def compress(col, pe, w):
        rows = cache_kv[:, :, col:col + HEAD_DIM][page_table].reshape(B, n_ch, CMP_STRIDE, HEAD_DIM)
        first = jnp.einsum('bcld,lde->bce', rows + pe[:CMP_STRIDE], w[:CMP_STRIDE])
        second = jnp.einsum('bcld,lde->bce', rows + pe[CMP_STRIDE:], w[CMP_STRIDE:])
        return first[:, :-1] + second[:, 1:]

    k_cmp = compress(c0, lp['cmp_pos_k'], lp['w_cmp_k'])
    v_cmp = compress(c0 + HEAD_DIM, lp['cmp_pos_v'], lp['w_cmp_v'])
    d_cmp = past - (jnp.arange(n_cmp) * CMP_STRIDE + CMP_LEN - 1)
    logits = jnp.einsum('bhd,bnd->bhn', qn, k_cmp) * SCALE + _bias_lookup(tab_n, d_cmp)[None]
    p_cmp = _softmax_rows(logits, jnp.broadcast_to(d_cmp >= 0, logits.shape))
    o_cmp = jnp.einsum('bhn,bnd->bhd', p_cmp, v_cmp)

    n_slc = past // SLC_BLOCK
    c_start = jnp.arange(n_cmp) * CMP_STRIDE
    s_start = jnp.arange(n_slc) * SLC_BLOCK
    overlap = ((c_start[:, None] < s_start[None, :] + SLC_BLOCK)
               & (c_start[:, None] + CMP_LEN > s_start[None, :])).astype(F32)
    imp = jnp.einsum('bhn,ns->bs', p_cmp, overlap)
    _, top = lax.top_k(imp, SLC_TOPK - 1)
    per_page = PAGE_SIZE // SLC_BLOCK
    pg = page_table[bidx[:, None], top // per_page]
    halves = cache_kv[:, :, c0 + 2 * HEAD_DIM:].reshape(-1, SLC_BLOCK, 2 * HEAD_DIM)[pg * per_page + top % per_page]
    ks = halves[..., :HEAD_DIM].reshape(B, -1, HEAD_DIM)
    vs = halves[..., HEAD_DIM:].reshape(B, -1, HEAD_DIM)
    ks = jnp.concatenate([ks, sk[:, None, :]], axis=1)
    vs = jnp.concatenate([vs, sv[:, None, :]], axis=1)
    kpos = (top[..., None] * SLC_BLOCK + jnp.arange(SLC_BLOCK)).reshape(B, -1)
    dist = past - jnp.concatenate([kpos, jnp.full((B, 1), past)], axis=1)
    logits = jnp.einsum('bhd,bkd->bhk', qn, ks) * SCALE + _bias_lookup(tab_n, dist).transpose(1, 0, 2)
    p = jax.nn.softmax(logits, axis=-1)
    o_slc = jnp.einsum('bhk,bkd->bhd', p, vs)

    n_win = win_buf.shape[1]
    full_win = jnp.concatenate([win_buf, jnp.concatenate([wk, wv], axis=-1)[:, None, :]], axis=1)
    dist = n_win - jnp.arange(n_win + 1)
    mask = (dist >= 0) & (dist < WINDOW)
    logits = jnp.einsum('bhd,bkd->bhk', qn, full_win[..., :HEAD_DIM]) * SCALE
    logits = logits + _bias_lookup(tab_n, dist)[None]
    p = _softmax_rows(logits, jnp.broadcast_to(mask, logits.shape))
    o_win = jnp.einsum('bhk,bkd->bhd', p, full_win[..., HEAD_DIM:])
    gt = jax.nn.sigmoid(ng).reshape(B, NSA_HEADS, 3)
    o_b = (gt[..., 0:1] * o_cmp + gt[..., 1:2] * o_slc + gt[..., 2:3] * o_win).reshape(B, NSA_W)

    xc = jnp.concatenate([conv0, lx[:, None, :]], axis=1)
    u = lp['conv_b']
    for j in range(CONV_W):
        u = u + xc[:, j] * lp['conv_w'][j]
    ub = u.reshape(B, LRU_BLOCKS, LRU_BLOCK)
    r = jax.nn.sigmoid(jnp.einsum('bnk,nkj->bnj', ub, lp['w_rg']).reshape(B, LRU_WIDTH) + lp['b_rg'])
    ig = jax.nn.sigmoid(jnp.einsum('bnk,nkj->bnj', ub, lp['w_ig']).reshape(B, LRU_WIDTH) + lp['b_ig'])
    log_a = -LRU_C * r * jax.nn.softplus(-lp['lru_lambda'])
    h_new = jnp.exp(log_a) * h0 + jnp.sqrt(1.0 - jnp.exp(2.0 * log_a)) * (ig * u)
    o_c = h_new * jax.nn.gelu(lg)

    merged = (jax.nn.sigmoid(ga) * _dot_hi(o_a, lp['w_br_moba']) + jax.nn.sigmoid(gb) * _dot_hi(o_b, lp['w_br_nsa'])
              + jax.nn.sigmoid(gc) * _dot_hi(o_c, lp['w_br_lru']))
    x = x + g1 * _dot_hi(merged, lp['w_out'])
    h2 = _rms(x, lp['norm_ffn']) * (1.0 + sc2) + sh2
    e_idx, w_top = _route(_dot_hi(h2, w_router) + b_router)
    comb = jnp.sum(jax.nn.one_hot(e_idx, N_EXPERTS, dtype=F32) * w_top[..., None], axis=1)
    y = jnp.zeros_like(h2)
    for e in range(N_EXPERTS):
        act = jax.nn.silu(_dot_hi(h2, lp['w_e_gate'][e])) * _dot_hi(h2, lp['w_e_up'][e])
        y = y + comb[:, e:e + 1] * _dot_hi(act, lp['w_e_down'][e])
    x = x + g2 * y
    return x, new_kv, full_win[:, full_win.shape[1] - min(WINDOW, full_win.shape[1]):], h_new, xc[:, 1:]


def kernel(x_prompt, x_sample, cache_kv, cache_win, state_lru_h, state_lru_conv, page_table, c_prompt, c_sample,
           w_ada, b_ada, norm_mix, norm_ffn, w_in, cmp_pos_k, cmp_pos_v, w_cmp_k, w_cmp_v, conv_w, conv_b,
           w_rg, b_rg, w_ig, b_ig, lru_lambda, w_br_moba, w_br_nsa, w_br_lru, w_out, w_e_gate, w_e_up, w_e_down,
           t5_bias, w_router, b_router, norm_final):
    n_p, T, D = x_prompt.shape
    n_s = x_sample.shape[0]
    xp = x_prompt
    xs = x_sample.reshape(n_s, D)
    tabs = _prompt_tables(t5_bias, T)
    w_router_pad = jnp.pad(w_router, ((0, 0), (0, LANE - N_EXPERTS)))
    c_all = jax.nn.silu(jnp.concatenate([c_prompt, c_sample], axis=0))
    c_all = jnp.pad(c_all, ((0, 16 - n_p - n_s), (0, 0)))
    outs = [[] for _ in range(8)]
    for l in range(DEPTH):
        lp = {'norm_mix': norm_mix[l], 'norm_ffn': norm_ffn[l], 'w_in': w_in[l],
              'cmp_pos_k': cmp_pos_k[l], 'cmp_pos_v': cmp_pos_v[l], 'w_cmp_k': w_cmp_k[l], 'w_cmp_v': w_cmp_v[l],
              'conv_w': conv_w[l], 'conv_b': conv_b[l], 'w_rg': w_rg[l], 'b_rg': b_rg[l], 'w_ig': w_ig[l],
              'b_ig': b_ig[l], 'lru_lambda': lru_lambda[l], 'w_br_moba': w_br_moba[l], 'w_br_nsa': w_br_nsa[l],
              'w_br_lru': w_br_lru[l], 'w_out': w_out[l], 'w_e_gate': w_e_gate[l], 'w_e_up': w_e_up[l],
              'w_e_down': w_e_down[l]}
        mod = _matmul(c_all, w_ada[l], 16, 2048) + b_ada[l]
        xp, kv_p, win_p, h_p, conv_p = _layer_prompt(xp, mod[:n_p], lp, tabs, w_router_pad, b_router)
        xs, kv_s, win_s, h_s, conv_s = _layer_sample(xs, mod[n_p:n_p + n_s], cache_kv[l], page_table, cache_win[l],
                                                     state_lru_h[l], state_lru_conv[l], lp, t5_bias, w_router,
                                                     b_router)
        for lst, val in zip(outs, (kv_p, kv_s[:, None, :], win_p, win_s, h_p, h_s, conv_p, conv_s)):
            lst.append(val)
    y_prompt = _rms(xp, norm_final)
    y_sample = _rms(xs, norm_final).reshape(n_s, 1, D)
    return (y_prompt, y_sample) + tuple(jnp.stack(o) for o in outs)
```

```python
import functools
import math

import jax
import jax.numpy as jnp
from jax import lax
from jax.experimental import pallas as pl
from jax.experimental.pallas import tpu as pltpu

D_MODEL = 2048
DEPTH = 2
PAGE_SIZE = 128
HEAD_DIM = 128
MOBA_HEADS = 4
MOBA_BLOCK = 256
MOBA_TOPK = 3
NSA_HEADS = 4
CMP_STRIDE = 16
CMP_LEN = 32
SLC_BLOCK = 64
SLC_TOPK = 16
WINDOW = 512
LRU_WIDTH = D_MODEL // 2
LRU_BLOCKS = 8
LRU_BLOCK = LRU_WIDTH // LRU_BLOCKS
LRU_C = 8.0
CONV_W = 4
N_EXPERTS = 16
N_GROUPS = 4
EXPERTS_PER_GROUP = N_EXPERTS // N_GROUPS
TOP_K = 2
D_EXPERT = D_MODEL // 2
T5_BUCKETS = 32
T5_MAX_DIST = 128
EPS = 1e-6
NEG = -1e30

MOBA_W = MOBA_HEADS * HEAD_DIM
NSA_W = NSA_HEADS * HEAD_DIM
KV_DIM = 2 * MOBA_W + 4 * HEAD_DIM
WIN_DIM = 2 * HEAD_DIM
IN_WIDTHS = (MOBA_W, MOBA_W, MOBA_W, NSA_W, HEAD_DIM, HEAD_DIM, HEAD_DIM, HEAD_DIM, HEAD_DIM, HEAD_DIM,
             3 * NSA_HEADS, LRU_WIDTH, LRU_WIDTH, D_MODEL, D_MODEL, D_MODEL)
COL_NG = 3 * MOBA_W + NSA_W + 6 * HEAD_DIM
COL_LX = COL_NG + 3 * NSA_HEADS
ZA_W = COL_NG + HEAD_DIM
SCALE = HEAD_DIM ** -0.5

LANE = 128
ATT_TILE = 512
MOE_TILE = 256
VMEM_LIMIT = 56 * 1024 * 1024

F32 = jnp.float32
BF16 = jnp.bfloat16


def _cparams(sem):
    return pltpu.CompilerParams(dimension_semantics=sem, vmem_limit_bytes=VMEM_LIMIT)


def _mm_kernel(x_ref, w_ref, o_ref):
    o_ref[...] = jnp.dot(x_ref[...].astype(BF16), w_ref[...].astype(BF16), preferred_element_type=F32)


def _matmul(x, w, tm, tn):
    M, K = x.shape
    N = w.shape[1]
    assert M % tm == 0 and N % tn == 0
    return pl.pallas_call(
        _mm_kernel,
        grid=(N // tn, M // tm),
        in_specs=[pl.BlockSpec((tm, K), lambda j, i: (i, 0)),
                  pl.BlockSpec((K, tn), lambda j, i: (0, j))],
        out_specs=pl.BlockSpec((tm, tn), lambda j, i: (i, j)),
        out_shape=jax.ShapeDtypeStruct((M, N), F32),
        compiler_params=_cparams(("parallel", "parallel")),
        name="matmul",
    )(x, w)


def _merge_kernel(oa_ref, ocmp_ref, oslc_ref, owin_ref, ng_ref, oc_ref, ga_ref, gb_ref, gc_ref,
                  wa_ref, wb_ref, wc_ref, o_ref):
    gt = jax.nn.sigmoid(ng_ref[...])
    heads = []
    for g in range(NSA_HEADS):
        sl = slice(g * HEAD_DIM, (g + 1) * HEAD_DIM)
        heads.append(gt[:, 3 * g:3 * g + 1] * ocmp_ref[:, sl] + gt[:, 3 * g + 1:3 * g + 2] * oslc_ref[:, sl]
                     + gt[:, 3 * g + 2:3 * g + 3] * owin_ref[:, sl])
    ob = jnp.concatenate(heads, axis=1).astype(BF16)
    pa = jnp.dot(oa_ref[...].astype(BF16), wa_ref[...], preferred_element_type=F32)
    pb = jnp.dot(ob, wb_ref[...], preferred_element_type=F32)
    pc = jnp.dot(oc_ref[...].astype(BF16), wc_ref[...], preferred_element_type=F32)
    merged = (jax.nn.sigmoid(ga_ref[...]) * pa + jax.nn.sigmoid(gb_ref[...]) * pb
              + jax.nn.sigmoid(gc_ref[...]) * pc)
    o_ref[...] = merged.astype(BF16)


def _merge(o_a, o_cmp, o_slc, o_win, za, o_c, zb, w_a, w_b, w_c, tm):
    N = o_a.shape[0]
    D = D_MODEL
    row = lambda w, c: pl.BlockSpec((tm, w), lambda i: (i, c))
    full = lambda a: pl.BlockSpec(a.shape, lambda i: (0, 0))
    return pl.pallas_call(
        _merge_kernel,
        grid=(N // tm,),
        in_specs=[row(MOBA_W, 0), row(NSA_W, 0), row(NSA_W, 0), row(NSA_W, 0), row(LANE, COL_NG // LANE),
                  row(LRU_WIDTH, 0), row(D, 1), row(D, 2), row(D, 3), full(w_a), full(w_b), full(w_c)],
        out_specs=pl.BlockSpec((tm, D), lambda i: (i, 0)),
        out_shape=jax.ShapeDtypeStruct((N, D), BF16),
        compiler_params=_cparams(("parallel",)),
        name="merge",
    )(o_a, o_cmp, o_slc, o_win, za, o_c, zb, zb, zb, w_a, w_b, w_c)


def _outproj_kernel(m_ref, w_ref, x_ref, g1_ref, sc2_ref, sh2_ref, nf_ref, wr_ref, br_ref,
                    xo_ref, h2_ref, lg_ref):
    y = jnp.dot(m_ref[...], w_ref[...], preferred_element_type=F32)
    x = x_ref[...] + g1_ref[0] * y
    xo_ref[...] = x
    h2 = x * lax.rsqrt(jnp.mean(x * x, axis=-1, keepdims=True) + EPS) * nf_ref[...]
    h2 = (h2 * (1.0 + sc2_ref[0]) + sh2_ref[0]).astype(BF16)
    h2_ref[...] = h2
    lg_ref[...] = jnp.dot(h2, wr_ref[...], preferred_element_type=F32) + br_ref[...]


def _outproj(merged, w_out, x, mod3, norm_ffn, w_router_pad, b_router_pad, tm, rows_per_mod, mod_base):
    N, D = x.shape
    R = mod3.shape[1]
    mspec = lambda k: pl.BlockSpec((1, R, D), lambda i: ((mod_base + (i * tm) // rows_per_mod) * 6 + k, 0, 0))
    full = lambda a: pl.BlockSpec(a.shape, lambda i: (0, 0))
    return pl.pallas_call(
        _outproj_kernel,
        grid=(N // tm,),
        in_specs=[pl.BlockSpec((tm, D), lambda i: (i, 0)), full(w_out), pl.BlockSpec((tm, D), lambda i: (i, 0)),
                  mspec(2), mspec(4), mspec(3), full(norm_ffn), full(w_router_pad), full(b_router_pad)],
        out_specs=[pl.BlockSpec((tm, D), lambda i: (i, 0)), pl.BlockSpec((tm, D), lambda i: (i, 0)),
                   pl.BlockSpec((tm, LANE), lambda i: (i, 0))],
        out_shape=[jax.ShapeDtypeStruct((N, D), F32), jax.ShapeDtypeStruct((N, D), BF16),
                   jax.ShapeDtypeStruct((N, LANE), F32)],
        compiler_params=_cparams(("parallel",)),
        name="outproj",
    )(merged, w_out, x, mod3, mod3, mod3, norm_ffn, w_router_pad, b_router_pad)


def _rank_select(score, own, nblk, topk):
    lane = lax.broadcasted_iota(jnp.int32, score.shape, 1)
    past = lane < own
    s = jnp.where(past, score, NEG)
    rank = jnp.zeros(score.shape, jnp.int32)
    for j in range(nblk):
        col = s[:, j:j + 1]
        beats = (col > s) | ((col == s) & (lane > j))
        rank = rank + beats.astype(jnp.int32)
    return (past & (rank < topk)) | (lane == own)


def _moba_select_kernel(q_ref, k_ref, sel_ref):
    T = q_ref.shape[1]
    nb = T // MOBA_BLOCK
    k = k_ref[0]
    kmean = jnp.sum(k.reshape(nb, MOBA_BLOCK, HEAD_DIM), axis=1) * (1.0 / MOBA_BLOCK)
    kmean = jnp.concatenate([kmean, jnp.zeros((LANE - nb, HEAD_DIM), F32)], axis=0)
    score = lax.dot_general(q_ref[0].astype(BF16), kmean.astype(BF16), (((1,), (1,)), ((), ())),
                            preferred_element_type=F32)
    own = lax.broadcasted_iota(jnp.int32, (T, 1), 0) // MOBA_BLOCK
    sel = _rank_select(score, own, nb, MOBA_TOPK)
    sel_ref[0, 0] = sel.astype(BF16)


def _moba_select(za, B, T):
    return pl.pallas_call(
        _moba_select_kernel,
        grid=(B, MOBA_HEADS),
        in_specs=[pl.BlockSpec((1, T, HEAD_DIM), lambda b, h: (b, 0, h)),
                  pl.BlockSpec((1, T, HEAD_DIM), lambda b, h: (b, 0, MOBA_HEADS + h))],
        out_specs=pl.BlockSpec((1, 1, T, LANE), lambda b, h: (b, h, 0, 0)),
        out_shape=jax.ShapeDtypeStruct((B, MOBA_HEADS, T, LANE), BF16),
        compiler_params=_cparams(("parallel", "parallel")),
        name="moba_select",
    )(za, za)


def _flash_kernel(*refs, G, window_tiles, use_sel, per_head_kv):
    if use_sel:
        q_ref, k_ref, v_ref, bias_ref, sel_ref, e_ref, o_ref, m_scr, l_scr, acc_scr = refs
    else:
        q_ref, k_ref, v_ref, bias_ref, o_ref, m_scr, l_scr, acc_scr = refs
    qi = pl.program_id(1)
    kj = pl.program_id(2)
    nk = pl.num_programs(2)
    tq, tk = ATT_TILE, ATT_TILE

    @pl.when(kj == 0)
    def _():
        m_scr[...] = jnp.full(m_scr.shape, NEG, F32)
        l_scr[...] = jnp.zeros(l_scr.shape, F32)
        acc_scr[...] = jnp.zeros(acc_scr.shape, F32)

    lo = jnp.maximum(qi - window_tiles, 0) if window_tiles is not None else 0

    @pl.when((kj >= lo) & (kj <= qi))
    def _():
        row = lax.broadcasted_iota(jnp.int32, (tq, tk), 0)
        col = lax.broadcasted_iota(jnp.int32, (tq, tk), 1)
        dist = (qi - kj) * tq + row - col
        band = dist >= 0
        if window_tiles is not None:
            band = band & (dist < WINDOW)
        mask = band
        for g in range(G):
            kv = slice(g * HEAD_DIM, (g + 1) * HEAD_DIM) if per_head_kv else slice(0, HEAD_DIM)
            if g == 0 or per_head_kv:
                k = k_ref[0, :, kv].astype(BF16)
                v = v_ref[0, :, kv].astype(BF16)
                if use_sel:
                    hit = jnp.dot(sel_ref[0, g], e_ref[...], preferred_element_type=F32)
                    mask = band & (hit > 0.5)
            q = q_ref[0, :, g * HEAD_DIM:(g + 1) * HEAD_DIM].astype(BF16)
            s = lax.dot_general(q, k, (((1,), (1,)), ((), ())), preferred_element_type=F32)
            s = s * SCALE + bias_ref[g, 0]
            s = jnp.where(mask, s, NEG)
            m_prev = m_scr[g]
            m_new = jnp.maximum(m_prev, jnp.max(s, axis=1, keepdims=True))
            p = jnp.where(mask, jnp.exp(s - m_new), 0.0)
            alpha = jnp.exp(m_prev - m_new)
            l_scr[g] = alpha * l_scr[g] + jnp.sum(p, axis=1, keepdims=True)
            acc_scr[g] = alpha * acc_scr[g] + jnp.dot(p.astype(BF16), v, preferred_element_type=F32)
            m_scr[g] = m_new

    @pl.when(kj == nk - 1)
    def _():
        for g in range(G):
            l = l_scr[g]
            o = acc_scr[g] / jnp.where(l > 0.0, l, 1.0)
            o_ref[0, :, g * HEAD_DIM:(g + 1) * HEAD_DIM] = o


def _flash(za, bias, *, B, T, q_col, k_col, v_col, bias_blk, per_head_kv, window_tiles=None, sel=None, emat=None):
    tq = tk = ATT_TILE
    nq, nk = T // tq, T // tk
    G = NSA_HEADS
    use_sel = sel is not None
    kv_w = G * HEAD_DIM if per_head_kv else HEAD_DIM

    def kv_blk(i, j):
        lo = jnp.maximum(i - window_tiles, 0) if window_tiles is not None else 0
        return jnp.clip(j, lo, i)

    in_specs = [
        pl.BlockSpec((1, tq, G * HEAD_DIM), lambda b, i, j: (b, i, q_col)),
        pl.BlockSpec((1, tk, kv_w), lambda b, i, j: (b, kv_blk(i, j), k_col)),
        pl.BlockSpec((1, tk, kv_w), lambda b, i, j: (b, kv_blk(i, j), v_col)),
        pl.BlockSpec((G, 1, tq, tk), lambda b, i, j: (bias_blk, jnp.clip(i - j, 0, 2), 0, 0)),
    ]
    args = [za, za, za, bias]
    if use_sel:
        in_specs += [pl.BlockSpec((1, sel.shape[1], tq, LANE), lambda b, i, j: (b, 0, i, 0)),
                     pl.BlockSpec((LANE, tk), lambda b, i, j: (0, kv_blk(i, j)))]
        args += [sel, emat]
    return pl.pallas_call(
        functools.partial(_flash_kernel, G=G, window_tiles=window_tiles, use_sel=use_sel, per_head_kv=per_head_kv),
        grid=(B, nq, nk),
        in_specs=in_specs,
        out_specs=pl.BlockSpec((1, tq, G * HEAD_DIM), lambda b, i, j: (b, i, 0)),
        out_shape=jax.ShapeDtypeStruct((B, T, G * HEAD_DIM), F32),
        scratch_shapes=[pltpu.VMEM((G, tq, 1), F32), pltpu.VMEM((G, tq, 1), F32),
                        pltpu.VMEM((G, tq, HEAD_DIM), F32)],
        compiler_params=_cparams(("parallel", "parallel", "arbitrary")),
        name="flash",
    )(*args)


def _cmp_kernel(ck_ref, cv_ref, q_ref, pek_ref, pev_ref, wk_ref, wv_ref, bias_ref, ovl_ref,
                o_ref, sel_ref, kc_scr, vc_scr):
    T = ck_ref.shape[1]
    n_ch = T // CMP_STRIDE
    i = pl.program_id(1)
    tq = ATT_TILE

    @pl.when(i == 0)
    def _():
        for src, pe, w, dst in ((ck_ref, pek_ref, wk_ref, kc_scr), (cv_ref, pev_ref, wv_ref, vc_scr)):
            first = jnp.zeros((n_ch, HEAD_DIM), F32)
            second = jnp.zeros((n_ch, HEAD_DIM), F32)
            for l in range(CMP_STRIDE):
                x = src[0, pl.ds(l, n_ch, stride=CMP_STRIDE), :]
                l2 = CMP_STRIDE + l
                first = first + jnp.dot((x + pe[l:l + 1, :]).astype(BF16), w[l * HEAD_DIM:(l + 1) * HEAD_DIM, :],
                                        preferred_element_type=F32)
                second = second + jnp.dot((x + pe[l2:l2 + 1, :]).astype(BF16),
                                          w[l2 * HEAD_DIM:(l2 + 1) * HEAD_DIM, :], preferred_element_type=F32)
            dst[...] = (first + pltpu.roll(second, n_ch - 1, 0)).astype(BF16)

    kc = kc_scr[...]
    vc = vc_scr[...]
    t = i * tq + lax.broadcasted_iota(jnp.int32, (tq, n_ch), 0)
    n = lax.broadcasted_iota(jnp.int32, (tq, n_ch), 1)
    vis = (n * CMP_STRIDE + (CMP_LEN - 1)) <= t
    imp = jnp.zeros((tq, LANE), F32)
    ovl = ovl_ref[...]
    for g in range(NSA_HEADS):
        q = q_ref[0, :, g * HEAD_DIM:(g + 1) * HEAD_DIM].astype(BF16)
        s = lax.dot_general(q, kc, (((1,), (1,)), ((), ())), preferred_element_type=F32)
        s = s * SCALE + bias_ref[g]
        s = jnp.where(vis, s, NEG)
        m = jnp.max(s, axis=1, keepdims=True)
        e = jnp.where(vis, jnp.exp(s - m), 0.0)
        l = jnp.sum(e, axis=1, keepdims=True)
        p = e / jnp.where(l > 0.0, l, 1.0)
        pb = p.astype(BF16)
        o_ref[0, :, g * HEAD_DIM:(g + 1) * HEAD_DIM] = jnp.dot(pb, vc, preferred_element_type=F32)
        imp = imp + jnp.dot(pb, ovl, preferred_element_type=F32)
    own = (i * tq + lax.broadcasted_iota(jnp.int32, (tq, 1), 0)) // SLC_BLOCK
    sel = _rank_select(imp, own, T // SLC_BLOCK, SLC_TOPK - 1)
    sel_ref[0, 0] = sel.astype(BF16)


def _cmp_branch(za, pe_k, pe_v, w_k, w_v, bias_cmp, overlap, B, T):
    assert T // CMP_STRIDE == LANE
    tq = ATT_TILE
    col = 3 * MOBA_HEADS + NSA_HEADS
    return pl.pallas_call(
        _cmp_kernel,
        grid=(B, T // tq),
        in_specs=[
            pl.BlockSpec((1, T, HEAD_DIM), lambda b, i: (b, 0, col)),
            pl.BlockSpec((1, T, HEAD_DIM), lambda b, i: (b, 0, col + 1)),
            pl.BlockSpec((1, tq, NSA_W), lambda b, i: (b, i, 3)),
            pl.BlockSpec((CMP_LEN, HEAD_DIM), lambda b, i: (0, 0)),
            pl.BlockSpec((CMP_LEN, HEAD_DIM), lambda b, i: (0, 0)),
            pl.BlockSpec((CMP_LEN * HEAD_DIM, HEAD_DIM), lambda b, i: (0, 0)),
            pl.BlockSpec((CMP_LEN * HEAD_DIM, HEAD_DIM), lambda b, i: (0, 0)),
            pl.BlockSpec((NSA_HEADS, tq, LANE), lambda b, i: (0, i, 0)),
            pl.BlockSpec((LANE, LANE), lambda b, i: (0, 0)),
        ],
        out_specs=[pl.BlockSpec((1, tq, NSA_W), lambda b, i: (b, i, 0)),
                   pl.BlockSpec((1, 1, tq, LANE), lambda b, i: (b, 0, i, 0))],
        out_shape=[jax.ShapeDtypeStruct((B, T, NSA_W), F32),
                   jax.ShapeDtypeStruct((B, 1, T, LANE), BF16)],
        scratch_shapes=[pltpu.VMEM((LANE, HEAD_DIM), BF16), pltpu.VMEM((LANE, HEAD_DIM), BF16)],
        compiler_params=_cparams(("parallel", "arbitrary")),
        name="nsa_cmp",
    )(za, za, za, pe_k, pe_v, w_k, w_v, bias_cmp, overlap)


def _gelu_tanh(x):
    return 0.5 * x * (1.0 + jnp.tanh(math.sqrt(2.0 / math.pi) * (x + 0.044715 * (x * x * x))))


def _softplus(x):
    return jnp.maximum(x, 0.0) + jnp.log1p(jnp.exp(-jnp.abs(x)))


def _lru_kernel(x_ref, g_ref, c0_ref, h0_ref, cw_ref, cb_ref, wr_ref, br_ref, wi_ref, bi_ref, lam_ref,
                y_ref, hl_ref, cl_ref, a_scr, x_scr):
    T = x_ref.shape[1]
    x = x_ref[0]
    row = lax.broadcasted_iota(jnp.int32, (T, LRU_BLOCK), 0)
    c0 = c0_ref[0]

    def shifted(d):
        r = pltpu.roll(x, d, 0)
        for t in range(d):
            r = jnp.where(row == t, c0[t + CONV_W - 1 - d:t + CONV_W - d, :], r)
        return r

    u = cb_ref[...] + shifted(3) * cw_ref[0:1, :]
    u = u + shifted(2) * cw_ref[1:2, :]
    u = u + shifted(1) * cw_ref[2:3, :]
    u = u + x * cw_ref[3:4, :]
    ub = u.astype(BF16)
    r = jax.nn.sigmoid(jnp.dot(ub, wr_ref[0], preferred_element_type=F32) + br_ref[...])
    ig = jax.nn.sigmoid(jnp.dot(ub, wi_ref[0], preferred_element_type=F32) + bi_ref[...])
    log_a = (-LRU_C * r) * _softplus(-lam_ref[...])
    a = jnp.exp(log_a)
    b = jnp.sqrt(1.0 - jnp.exp(2.0 * log_a)) * (ig * u)
    sub = row % 8
    for d in (1, 2, 4):
        ok = sub >= d
        b = jnp.where(ok, a * pltpu.roll(b, d, 0) + b, b)
        a = jnp.where(ok, a * pltpu.roll(a, d, 0), a)
    a_scr[...] = a
    x_scr[...] = b

    def group(k, carry):
        s = pl.multiple_of(k * 8, 8)
        h = x_scr[pl.ds(s, 8), :] + a_scr[pl.ds(s, 8), :] * carry
        x_scr[pl.ds(s, 8), :] = h
        return h[7:8, :]

    h_last = lax.fori_loop(0, T // 8, group, h0_ref[0], unroll=8)
    y_ref[0] = x_scr[...] * _gelu_tanh(g_ref[0])
    hl_ref[0] = h_last
    cl_ref[0] = x[T - (CONV_W - 1):, :]


def _rglru(zb, conv0, h0, conv_w, conv_b, w_r, b_r, w_i, b_i, lam, B, T):
    nblk = LRU_BLOCKS
    vec = lambda a: a.reshape(1, LRU_WIDTH)
    row_spec = pl.BlockSpec((1, LRU_BLOCK), lambda b, n: (0, n))
    return pl.pallas_call(
        _lru_kernel,
        grid=(B, nblk),
        in_specs=[
            pl.BlockSpec((1, T, LRU_BLOCK), lambda b, n: (b, 0, n)),
            pl.BlockSpec((1, T, LRU_BLOCK), lambda b, n: (b, 0, nblk + n)),
            pl.BlockSpec((1, CONV_W - 1, LRU_BLOCK), lambda b, n: (b, 0, n)),
            pl.BlockSpec((1, 1, LRU_BLOCK), lambda b, n: (b, 0, n)),
            pl.BlockSpec((CONV_W, LRU_BLOCK), lambda b, n: (0, n)),
            row_spec,
            pl.BlockSpec((1, LRU_BLOCK, LRU_BLOCK), lambda b, n: (n, 0, 0)),
            row_spec,
            pl.BlockSpec((1, LRU_BLOCK, LRU_BLOCK), lambda b, n: (n, 0, 0)),
            row_spec,
            row_spec,
        ],
        out_specs=[pl.BlockSpec((1, T, LRU_BLOCK), lambda b, n: (b, 0, n)),
                   pl.BlockSpec((1, 1, LRU_BLOCK), lambda b, n: (b, 0, n)),
                   pl.BlockSpec((1, CONV_W - 1, LRU_BLOCK), lambda b, n: (b, 0, n))],
        out_shape=[jax.ShapeDtypeStruct((B, T, LRU_WIDTH), F32),
                   jax.ShapeDtypeStruct((B, 1, LRU_WIDTH), F32),
                   jax.ShapeDtypeStruct((B, CONV_W - 1, LRU_WIDTH), F32)],
        scratch_shapes=[pltpu.VMEM((T, LRU_BLOCK), F32), pltpu.VMEM((T, LRU_BLOCK), F32)],
        compiler_params=_cparams(("parallel", "parallel")),
        name="rglru",
    )(zb, zb, conv0, h0.reshape(B, 1, LRU_WIDTH), conv_w, vec(conv_b), w_r.astype(BF16), vec(b_r),
      w_i.astype(BF16), vec(b_i), vec(lam))


def _moe_kernel(eid_ref, nt_ref, x_ref, wg_ref, wu_ref, wd_ref, cw_ref, o_ref):
    t = pl.program_id(0)

    @pl.when(t < nt_ref[0])
    def _():
        x = x_ref[...]
        g = jnp.dot(x, wg_ref[0], preferred_element_type=F32)
        u = jnp.dot(x, wu_ref[0], preferred_element_type=F32)
        act = (g * jax.nn.sigmoid(g)) * u
        y = jnp.dot(act.astype(BF16), wd_ref[0], preferred_element_type=F32)
        o_ref[...] = y * cw_ref[...]

    @pl.when(t >= nt_ref[0])
    def _():
        o_ref[...] = jnp.zeros(o_ref.shape, F32)


def _moe_ffn(x_pad, cw_pad, tile_eid, n_tiles, w_g, w_u, w_d):
    P, D = x_pad.shape
    tm = MOE_TILE
    grid_spec = pltpu.PrefetchScalarGridSpec(
        num_scalar_prefetch=2,
        grid=(P // tm,),
        in_specs=[
            pl.BlockSpec((tm, D), lambda t, eid, nt: (t, 0)),
            pl.BlockSpec((1, D, D_EXPERT), lambda t, eid, nt: (eid[t], 0, 0)),
            pl.BlockSpec((1, D, D_EXPERT), lambda t, eid, nt: (eid[t], 0, 0)),
            pl.BlockSpec((1, D_EXPERT, D), lambda t, eid, nt: (eid[t], 0, 0)),
            pl.BlockSpec((tm, 1), lambda t, eid, nt: (t, 0)),
        ],
        out_specs=pl.BlockSpec((tm, D), lambda t, eid, nt: (t, 0)),
    )
    return pl.pallas_call(
        _moe_kernel,
        grid_spec=grid_spec,
        out_shape=jax.ShapeDtypeStruct((P, D), F32),
        compiler_params=_cparams(("arbitrary",)),
        name="moe_ffn",
    )(tile_eid, n_tiles, x_pad, w_g, w_u, w_d, cw_pad)


def _route(logits):
    probs = jax.nn.softmax(logits.astype(F32), axis=-1)
    grp = probs.reshape(-1, N_GROUPS, EXPERTS_PER_GROUP)
    pairs = [grp[..., a] + grp[..., b] for a in range(EXPERTS_PER_GROUP) for b in range(a + 1, EXPERTS_PER_GROUP)]
    g_score = functools.reduce(jnp.maximum, pairs)
    g_best = jnp.argmax(g_score, axis=-1)
    pick = g_best[:, None, None] == jnp.arange(N_GROUPS)[None, :, None]
    in_grp = jnp.sum(jnp.where(pick, grp, 0.0), axis=1)
    i1 = jnp.argmax(in_grp, axis=-1)
    first = jnp.arange(EXPERTS_PER_GROUP)[None, :] == i1[:, None]
    w1 = jnp.max(in_grp, axis=-1)
    rest = jnp.where(first, -jnp.inf, in_grp)
    i2 = jnp.argmax(rest, axis=-1)
    w2 = jnp.max(rest, axis=-1)
    w_top = jnp.stack([w1, w2], axis=-1)
    w_top = w_top / jnp.sum(w_top, axis=-1, keepdims=True)
    e_idx = g_best[:, None] * EXPERTS_PER_GROUP + jnp.stack([i1, i2], axis=-1)
    return e_idx.astype(jnp.int32), w_top


def _moe_prompt(h2, logits, w_g, w_u, w_d):
    N, D = h2.shape
    tm = MOE_TILE
    e_idx, w_top = _route(logits)
    flat_e = e_idx.reshape(-1)
    onehot = (flat_e[:, None] == jnp.arange(N_EXPERTS)[None, :]).astype(jnp.int32)
    within = jnp.sum(onehot * jnp.cumsum(onehot, axis=0), axis=1) - 1
    sizes = jnp.sum(onehot, axis=0)
    padded = ((sizes + tm - 1) // tm) * tm
    pend = jnp.cumsum(padded)
    pstart = pend - padded
    dest = jnp.sum(onehot * pstart[None, :], axis=1) + within
    P = 2 * N + N_EXPERTS * tm
    src_tok = jnp.zeros((P,), jnp.int32).at[dest].set(jnp.arange(2 * N, dtype=jnp.int32) // TOP_K)
    cw_pad = jnp.zeros((P,), F32).at[dest].set(w_top.reshape(-1))
    x_pad = h2[src_tok]
    tile_start = jnp.arange(P // tm, dtype=jnp.int32) * tm
    tile_eid = jnp.minimum(jnp.sum(tile_start[:, None] >= pend[None, :], axis=1), N_EXPERTS - 1).astype(jnp.int32)
    n_tiles = (pend[-1] // tm).astype(jnp.int32).reshape(1)
    out = _moe_ffn(x_pad, cw_pad.reshape(P, 1), tile_eid, n_tiles, w_g, w_u, w_d)
    pos = dest.reshape(N, TOP_K)
    return out[pos[:, 0]] + out[pos[:, 1]]


def _t5_bucket(dist):
    n = jnp.maximum(dist, 0)
    exact = T5_BUCKETS // 2
    nf = jnp.maximum(n, 1).astype(F32)
    large = exact + (jnp.log(nf / exact) / math.log(T5_MAX_DIST / exact) * (T5_BUCKETS - exact)).astype(jnp.int32)
    return jnp.where(n < exact, n, jnp.minimum(large, T5_BUCKETS - 1))


def _bias_lookup(tab, dist):
    bucket = _t5_bucket(dist)
    out = jnp.zeros((tab.shape[0],) + bucket.shape, F32)
    for b in range(T5_BUCKETS):
        out = jnp.where(bucket[None] == b, tab[:, b].reshape((-1,) + (1,) * bucket.ndim), out)
    return out


def _rms(x, g):
    return x * lax.rsqrt(jnp.mean(x * x, axis=-1, keepdims=True) + EPS) * g


def _split(z, widths):
    outs, s = [], 0
    for w in widths:
        outs.append(z[..., s:s + w])
        s += w
    return outs


def _prompt_tables(t5_bias, T):
    t = ATT_TILE
    i = jnp.arange(t)
    dist = jnp.arange(3)[:, None, None] * t + i[None, :, None] - i[None, None, :]
    bias_tiles = _bias_lookup(t5_bias.T, dist)
    pos = jnp.arange(T)
    lane = jnp.arange(LANE)
    e_moba = (pos[None, :] // MOBA_BLOCK == lane[:, None]).astype(BF16)
    e_slc = (pos[None, :] // SLC_BLOCK == lane[:, None]).astype(BF16)
    d_cmp = pos[:, None] - (lane[None, :] * CMP_STRIDE + CMP_LEN - 1)
    bias_cmp = _bias_lookup(t5_bias.T[MOBA_HEADS:], d_cmp)
    c_start = lane * CMP_STRIDE
    s_start = lane * SLC_BLOCK
    n_cmp = T // CMP_STRIDE - 1
    overlap = ((c_start[:, None] < s_start[None, :] + SLC_BLOCK) & (c_start[:, None] + CMP_LEN > s_start[None, :])
               & (lane[:, None] < n_cmp) & (lane[None, :] < T // SLC_BLOCK)).astype(BF16)
    return bias_tiles, e_moba, e_slc, bias_cmp, overlap


def _layer_prompt(x, mod, mod3, lp, tabs, w_router_pad, b_router_pad):
    B, T, D = x.shape
    N = B * T
    bias_tiles, e_moba, e_slc, bias_cmp, overlap = tabs
    sh1, sc1, g1, sh2, sc2, g2 = [m[:, None, :] for m in jnp.split(mod, 6, axis=-1)]
    h = (_rms(x, lp['norm_mix']) * (1.0 + sc1) + sh1).astype(BF16).reshape(N, D)
    za = _matmul(h, lp['w_in'][:, :ZA_W].astype(BF16), 256, ZA_W).reshape(B, T, ZA_W)
    zb = _matmul(h, lp['w_in'][:, COL_LX:].astype(BF16), 512, 2048).reshape(B, T, -1)
    new_kv = jnp.concatenate([za[..., MOBA_W:3 * MOBA_W], za[..., 3 * MOBA_W + NSA_W:COL_NG - WIN_DIM]], axis=-1)
    win = za[:, T - WINDOW:, COL_NG - WIN_DIM:COL_NG]

    sel_moba = _moba_select(za, B, T)
    o_a = _flash(za, bias_tiles, B=B, T=T, q_col=0, k_col=1, v_col=2, bias_blk=0, per_head_kv=True,
                 sel=sel_moba, emat=e_moba)
    o_cmp, sel_slc = _cmp_branch(za, lp['cmp_pos_k'], lp['cmp_pos_v'],
                                 lp['w_cmp_k'].reshape(-1, HEAD_DIM).astype(BF16),
                                 lp['w_cmp_v'].reshape(-1, HEAD_DIM).astype(BF16), bias_cmp, overlap, B, T)
    o_slc = _flash(za, bias_tiles, B=B, T=T, q_col=3, k_col=18, v_col=19, bias_blk=1, per_head_kv=False,
                   sel=sel_slc, emat=e_slc)
    o_win = _flash(za, bias_tiles, B=B, T=T, q_col=3, k_col=20, v_col=21, bias_blk=1, per_head_kv=False,
                   window_tiles=WINDOW // ATT_TILE)
    o_c, h_last, conv_last = _rglru(zb, jnp.zeros((B, CONV_W - 1, LRU_WIDTH), F32), jnp.zeros((B, LRU_WIDTH), F32),
                                    lp['conv_w'], lp['conv_b'], lp['w_rg'], lp['b_rg'], lp['w_ig'], lp['b_ig'],
                                    lp['lru_lambda'], B, T)
    flat = lambda a: a.reshape(N, a.shape[-1])
    merged = _merge(flat(o_a), flat(o_cmp), flat(o_slc), flat(o_win), flat(za), flat(o_c), flat(zb),
                    lp['w_br_moba'].astype(BF16), lp['w_br_nsa'].astype(BF16), lp['w_br_lru'].astype(BF16), 256)
    x, h2, logits = _outproj(merged, lp['w_out'].astype(BF16), x.reshape(N, D), mod3, lp['norm_ffn'].reshape(1, D),
                             w_router_pad, b_router_pad, 256, T, 0)
    moe = _moe_prompt(h2, logits[:, :N_EXPERTS], lp['w_e_gate'].astype(BF16), lp['w_e_up'].astype(BF16),
                      lp['w_e_down'].astype(BF16))
    x = x.reshape(B, T, D) + g2 * moe.reshape(B, T, D)
    return x, new_kv, win, h_last.reshape(B, LRU_WIDTH), conv_last


def _dot_hi(a, b):
    return jnp.matmul(a, b)


def _softmax_rows(logits, mask):
    logits = jnp.where(mask, logits, NEG)
    p = jax.nn.softmax(logits, axis=-1)
    return jnp.where(mask, p, 0.0)


def _layer_sample(x, mod, cache_kv, page_table, win_buf, h0, conv0, lp, t5_bias, w_router, b_router):
    B, D = x.shape
    n_pages = page_table.shape[1]
    past = n_pages * PAGE_SIZE
    sh1, sc1, g1, sh2, sc2, g2 = jnp.split(mod, 6, axis=-1)
    h = _rms(x, lp['norm_mix']) * (1.0 + sc1) + sh1
    z = _dot_hi(h, lp['w_in'])
    mq, mk, mv, nq, ck, cv, sk, sv, wk, wv, ng, lx, lg, ga, gb, gc = _split(z, IN_WIDTHS)
    new_kv = jnp.concatenate([mk, mv, ck, cv, sk, sv], axis=-1)
    tab = t5_bias.T
    bidx = jnp.arange(B)

    nb = past // MOBA_BLOCK
    pages_per_blk = MOBA_BLOCK // PAGE_SIZE
    k_all = cache_kv[:, :, :MOBA_W].reshape(-1, PAGE_SIZE, MOBA_HEADS, HEAD_DIM)
    v_all = cache_kv[:, :, MOBA_W:2 * MOBA_W].reshape(-1, PAGE_SIZE, MOBA_HEADS, HEAD_DIM)
    k_mean = jnp.mean(k_all[page_table].reshape(B, nb, MOBA_BLOCK, MOBA_HEADS, HEAD_DIM), axis=2)
    q = mq.reshape(B, MOBA_HEADS, HEAD_DIM)
    score = jnp.einsum('bhd,bnhd->bhn', q, k_mean)
    _, top = lax.top_k(score, MOBA_TOPK)
    pg = page_table[bidx[:, None, None, None], top[..., None] * pages_per_blk + jnp.arange(pages_per_blk)]
    hidx = jnp.arange(MOBA_HEADS)[None, :, None, None]
    kg = k_all[pg, :, hidx].reshape(B, MOBA_HEADS, MOBA_TOPK * MOBA_BLOCK, HEAD_DIM)
    vg = v_all[pg, :, hidx].reshape(B, MOBA_HEADS, MOBA_TOPK * MOBA_BLOCK, HEAD_DIM)
    kpos = (top[..., None] * MOBA_BLOCK + jnp.arange(MOBA_BLOCK)).reshape(B, MOBA_HEADS, -1)
    kg = jnp.concatenate([kg, mk.reshape(B, MOBA_HEADS, 1, HEAD_DIM)], axis=2)
    vg = jnp.concatenate([vg, mv.reshape(B, MOBA_HEADS, 1, HEAD_DIM)], axis=2)
    dist = past - jnp.concatenate([kpos, jnp.full((B, MOBA_HEADS, 1), past)], axis=2)
    logits = jnp.einsum('bhd,bhkd->bhk', q, kg) * SCALE
    bucket = _t5_bucket(dist)
    for bkt in range(T5_BUCKETS):
        logits = logits + jnp.where(bucket == bkt, tab[None, :MOBA_HEADS, bkt, None], 0.0)
    p = jax.nn.softmax(logits, axis=-1)
    o_a = jnp.einsum('bhk,bhkd->bhd', p, vg).reshape(B, MOBA_W)

    tab_n = tab[MOBA_HEADS:]
    qn = nq.reshape(B, NSA_HEADS, HEAD_DIM)
    n_ch = past // CMP_STRIDE
    n_cmp = n_ch - 1
    c0 = 2 * MOBA_W

    def compress(col, pe, w):
        rows = cache_kv[:, :, col:col + HEAD_DIM][page_table].reshape(B, n_ch, CMP_STRIDE, HEAD_DIM)
        first = jnp.einsum('bcld,lde->bce', rows + pe[:CMP_STRIDE], w[:CMP_STRIDE])
        second = jnp.einsum('bcld,lde->bce', rows + pe[CMP_STRIDE:], w[CMP_STRIDE:])
        return first[:, :-1] + second[:, 1:]

    k_cmp = compress(c0, lp['cmp_pos_k'], lp['w_cmp_k'])
    v_cmp = compress(c0 + HEAD_DIM, lp['cmp_pos_v'], lp['w_cmp_v'])
    d_cmp = past - (jnp.arange(n_cmp) * CMP_STRIDE + CMP_LEN - 1)
    logits = jnp.einsum('bhd,bnd->bhn', qn, k_cmp) * SCALE + _bias_lookup(tab_n, d_cmp)[None]
    p_cmp = _softmax_rows(logits, jnp.broadcast_to(d_cmp >= 0, logits.shape))
    o_cmp = jnp.einsum('bhn,bnd->bhd', p_cmp, v_cmp)

    n_slc = past // SLC_BLOCK
    c_start = jnp.arange(n_cmp) * CMP_STRIDE
    s_start = jnp.arange(n_slc) * SLC_BLOCK
    overlap = ((c_start[:, None] < s_start[None, :] + SLC_BLOCK)
               & (c_start[:, None] + CMP_LEN > s_start[None, :])).astype(F32)
    imp = jnp.einsum('bhn,ns->bs', p_cmp, overlap)
    _, top = lax.top_k(imp, SLC_TOPK - 1)
    per_page = PAGE_SIZE // SLC_BLOCK
    pg = page_table[bidx[:, None], top // per_page]
    halves = cache_kv[:, :, c0 + 2 * HEAD_DIM:].reshape(-1, SLC_BLOCK, 2 * HEAD_DIM)[pg * per_page + top % per_page]
    ks = halves[..., :HEAD_DIM].reshape(B, -1, HEAD_DIM)
    vs = halves[..., HEAD_DIM:].reshape(B, -1, HEAD_DIM)
    ks = jnp.concatenate([ks, sk[:, None, :]], axis=1)
    vs = jnp.concatenate([vs, sv[:, None, :]], axis=1)
    kpos = (top[..., None] * SLC_BLOCK + jnp.arange(SLC_BLOCK)).reshape(B, -1)
    dist = past - jnp.concatenate([kpos, jnp.full((B, 1), past)], axis=1)
    logits = jnp.einsum('bhd,bkd->bhk', qn, ks) * SCALE + _bias_lookup(tab_n, dist).transpose(1, 0, 2)
    p = jax.nn.softmax(logits, axis=-1)
    o_slc = jnp.einsum('bhk,bkd->bhd', p, vs)

    n_win = win_buf.shape[1]
    full_win = jnp.concatenate([win_buf, jnp.concatenate([wk, wv], axis=-1)[:, None, :]], axis=1)
    dist = n_win - jnp.arange(n_win + 1)
    mask = (dist >= 0) & (dist < WINDOW)
    logits = jnp.einsum('bhd,bkd->bhk', qn, full_win[..., :HEAD_DIM]) * SCALE
    logits = logits + _bias_lookup(tab_n, dist)[None]
    p = _softmax_rows(logits, jnp.broadcast_to(mask, logits.shape))
    o_win = jnp.einsum('bhk,bkd->bhd', p, full_win[..., HEAD_DIM:])
    gt = jax.nn.sigmoid(ng).reshape(B, NSA_HEADS, 3)
    o_b = (gt[..., 0:1] * o_cmp + gt[..., 1:2] * o_slc + gt[..., 2:3] * o_win).reshape(B, NSA_W)

    xc = jnp.concatenate([conv0, lx[:, None, :]], axis=1)
    u = lp['conv_b']
    for j in range(CONV_W):
        u = u + xc[:, j] * lp['conv_w'][j]
    ub = u.reshape(B, LRU_BLOCKS, LRU_BLOCK)
    r = jax.nn.sigmoid(jnp.einsum('bnk,nkj->bnj', ub, lp['w_rg']).reshape(B, LRU_WIDTH) + lp['b_rg'])
    ig = jax.nn.sigmoid(jnp.einsum('bnk,nkj->bnj', ub, lp['w_ig']).reshape(B, LRU_WIDTH) + lp['b_ig'])
    log_a = -LRU_C * r * jax.nn.softplus(-lp['lru_lambda'])
    h_new = jnp.exp(log_a) * h0 + jnp.sqrt(1.0 - jnp.exp(2.0 * log_a)) * (ig * u)
    o_c = h_new * jax.nn.gelu(lg)

    merged = (jax.nn.sigmoid(ga) * _dot_hi(o_a, lp['w_br_moba']) + jax.nn.sigmoid(gb) * _dot_hi(o_b, lp['w_br_nsa'])
              + jax.nn.sigmoid(gc) * _dot_hi(o_c, lp['w_br_lru']))
    x = x + g1 * _dot_hi(merged, lp['w_out'])
    h2 = _rms(x, lp['norm_ffn']) * (1.0 + sc2) + sh2
    e_idx, w_top = _route(_dot_hi(h2, w_router) + b_router)
    comb = jnp.sum(jax.nn.one_hot(e_idx, N_EXPERTS, dtype=F32) * w_top[..., None], axis=1)
    y = jnp.zeros_like(h2)
    for e in range(N_EXPERTS):
        act = jax.nn.silu(_dot_hi(h2, lp['w_e_gate'][e])) * _dot_hi(h2, lp['w_e_up'][e])
        y = y + comb[:, e:e + 1] * _dot_hi(act, lp['w_e_down'][e])
    x = x + g2 * y
    return x, new_kv, full_win[:, full_win.shape[1] - min(WINDOW, full_win.shape[1]):], h_new, xc[:, 1:]


def kernel(x_prompt, x_sample, cache_kv, cache_win, state_lru_h, state_lru_conv, page_table, c_prompt, c_sample,
           w_ada, b_ada, norm_mix, norm_ffn, w_in, cmp_pos_k, cmp_pos_v, w_cmp_k, w_cmp_v, conv_w, conv_b,
           w_rg, b_rg, w_ig, b_ig, lru_lambda, w_br_moba, w_br_nsa, w_br_lru, w_out, w_e_gate, w_e_up, w_e_down,
           t5_bias, w_router, b_router, norm_final):
    n_p, T, D = x_prompt.shape
    n_s = x_sample.shape[0]
    xp = x_prompt
    xs = x_sample.reshape(n_s, D)
    tabs = _prompt_tables(t5_bias, T)
    w_router_pad = jnp.pad(w_router, ((0, 0), (0, LANE - N_EXPERTS))).astype(BF16)
    b_router_pad = jnp.pad(b_router, (0, LANE - N_EXPERTS)).reshape(1, LANE)
    c_all = jax.nn.silu(jnp.concatenate([c_prompt, c_sample], axis=0))
    c_all = jnp.pad(c_all, ((0, 16 - n_p - n_s), (0, 0)))
    outs = [[] for _ in range(8)]
    for l in range(DEPTH):
        lp = {'norm_mix': norm_mix[l], 'norm_ffn': norm_ffn[l], 'w_in': w_in[l],
              'cmp_pos_k': cmp_pos_k[l], 'cmp_pos_v': cmp_pos_v[l], 'w_cmp_k': w_cmp_k[l], 'w_cmp_v': w_cmp_v[l],
              'conv_w': conv_w[l], 'conv_b': conv_b[l], 'w_rg': w_rg[l], 'b_rg': b_rg[l], 'w_ig': w_ig[l],
              'b_ig': b_ig[l], 'lru_lambda': lru_lambda[l], 'w_br_moba': w_br_moba[l], 'w_br_nsa': w_br_nsa[l],
              'w_br_lru': w_br_lru[l], 'w_out': w_out[l], 'w_e_gate': w_e_gate[l], 'w_e_up': w_e_up[l],
              'w_e_down': w_e_down[l]}
        mod = _matmul(c_all, w_ada[l], 16, 2048) + b_ada[l]
        mod3 = mod.reshape(16 * 6, 1, D)
        xp, kv_p, win_p, h_p, conv_p = _layer_prompt(xp, mod[:n_p], mod3, lp, tabs, w_router_pad, b_router_pad)
        xs, kv_s, win_s, h_s, conv_s = _layer_sample(xs, mod[n_p:n_p + n_s], cache_kv[l], page_table, cache_win[l],
                                                     state_lru_h[l], state_lru_conv[l], lp, t5_bias, w_router,
                                                     b_router)
        for lst, val in zip(outs, (kv_p, kv_s[:, None, :], win_p, win_s, h_p, h_s, conv_p, conv_s)):
            lst.append(val)
    y_prompt = _rms(xp, norm_final)
    y_sample = _rms(xs, norm_final).reshape(n_s, 1, D)
    return (y_prompt, y_sample) + tuple(jnp.stack(o) for o in outs)
```

```python
import functools
import math

import jax
import jax.numpy as jnp
from jax import lax
from jax.experimental import pallas as pl
from jax.experimental.pallas import tpu as pltpu

D_MODEL = 2048
DEPTH = 2
PAGE_SIZE = 128
HEAD_DIM = 128
MOBA_HEADS = 4
MOBA_BLOCK = 256
MOBA_TOPK = 3
NSA_HEADS = 4
CMP_STRIDE = 16
CMP_LEN = 32
SLC_BLOCK = 64
SLC_TOPK = 16
WINDOW = 512
LRU_WIDTH = D_MODEL // 2
LRU_BLOCKS = 8
LRU_BLOCK = LRU_WIDTH // LRU_BLOCKS
LRU_C = 8.0
CONV_W = 4
N_EXPERTS = 16
N_GROUPS = 4
EXPERTS_PER_GROUP = N_EXPERTS // N_GROUPS
TOP_K = 2
D_EXPERT = D_MODEL // 2
T5_BUCKETS = 32
T5_MAX_DIST = 128
EPS = 1e-6
NEG = -1e30

MOBA_W = MOBA_HEADS * HEAD_DIM
NSA_W = NSA_HEADS * HEAD_DIM
KV_DIM = 2 * MOBA_W + 4 * HEAD_DIM
WIN_DIM = 2 * HEAD_DIM
IN_WIDTHS = (MOBA_W, MOBA_W, MOBA_W, NSA_W, HEAD_DIM, HEAD_DIM, HEAD_DIM, HEAD_DIM, HEAD_DIM, HEAD_DIM,
             3 * NSA_HEADS, LRU_WIDTH, LRU_WIDTH, D_MODEL, D_MODEL, D_MODEL)
COL_NG = 3 * MOBA_W + NSA_W + 6 * HEAD_DIM
COL_LX = COL_NG + 3 * NSA_HEADS
ZA_W = COL_NG + HEAD_DIM
SCALE = HEAD_DIM ** -0.5

LANE = 128
ATT_TILE = 512
MOE_TILE = 256
VMEM_LIMIT = 56 * 1024 * 1024

F32 = jnp.float32
BF16 = jnp.bfloat16


def _cparams(sem):
    return pltpu.CompilerParams(dimension_semantics=sem, vmem_limit_bytes=VMEM_LIMIT)


def _mm_kernel(x_ref, w_ref, o_ref):
    o_ref[...] = jnp.dot(x_ref[...].astype(BF16), w_ref[...].astype(BF16), preferred_element_type=F32)


def _matmul(x, w, tm, tn):
    M, K = x.shape
    N = w.shape[1]
    assert M % tm == 0 and N % tn == 0
    return pl.pallas_call(
        _mm_kernel,
        grid=(N // tn, M // tm),
        in_specs=[pl.BlockSpec((tm, K), lambda j, i: (i, 0)),
                  pl.BlockSpec((K, tn), lambda j, i: (0, j))],
        out_specs=pl.BlockSpec((tm, tn), lambda j, i: (i, j)),
        out_shape=jax.ShapeDtypeStruct((M, N), F32),
        compiler_params=_cparams(("parallel", "parallel")),
        name="matmul",
    )(x, w)


def _merge_kernel(oa_ref, ocmp_ref, oslc_ref, owin_ref, ng_ref, oc_ref, ga_ref, gb_ref, gc_ref,
                  wa_ref, wb_ref, wc_ref, o_ref):
    gt = jax.nn.sigmoid(ng_ref[...])
    heads = []
    for g in range(NSA_HEADS):
        sl = slice(g * HEAD_DIM, (g + 1) * HEAD_DIM)
        heads.append(gt[:, 3 * g:3 * g + 1] * ocmp_ref[:, sl] + gt[:, 3 * g + 1:3 * g + 2] * oslc_ref[:, sl]
                     + gt[:, 3 * g + 2:3 * g + 3] * owin_ref[:, sl])
    ob = jnp.concatenate(heads, axis=1).astype(BF16)
    pa = jnp.dot(oa_ref[...].astype(BF16), wa_ref[...], preferred_element_type=F32)
    pb = jnp.dot(ob, wb_ref[...], preferred_element_type=F32)
    pc = jnp.dot(oc_ref[...].astype(BF16), wc_ref[...], preferred_element_type=F32)
    merged = (jax.nn.sigmoid(ga_ref[...]) * pa + jax.nn.sigmoid(gb_ref[...]) * pb
              + jax.nn.sigmoid(gc_ref[...]) * pc)
    o_ref[...] = merged.astype(BF16)


def _merge(o_a, o_cmp, o_slc, o_win, za, o_c, zb, w_a, w_b, w_c, tm):
    N = o_a.shape[0]
    D = D_MODEL
    row = lambda w, c: pl.BlockSpec((tm, w), lambda i: (i, c))
    full = lambda a: pl.BlockSpec(a.shape, lambda i: (0, 0))
    return pl.pallas_call(
        _merge_kernel,
        grid=(N // tm,),
        in_specs=[row(MOBA_W, 0), row(NSA_W, 0), row(NSA_W, 0), row(NSA_W, 0), row(LANE, COL_NG // LANE),
                  row(LRU_WIDTH, 0), row(D, 1), row(D, 2), row(D, 3), full(w_a), full(w_b), full(w_c)],
        out_specs=pl.BlockSpec((tm, D), lambda i: (i, 0)),
        out_shape=jax.ShapeDtypeStruct((N, D), BF16),
        compiler_params=_cparams(("parallel",)),
        name="merge",
    )(o_a, o_cmp, o_slc, o_win, za, o_c, zb, zb, zb, w_a, w_b, w_c)


def _outproj_kernel(m_ref, w_ref, x_ref, g1_ref, sc2_ref, sh2_ref, nf_ref, wr_ref, br_ref,
                    xo_ref, h2_ref, lg_ref):
    y = jnp.dot(m_ref[...], w_ref[...], preferred_element_type=F32)
    x = x_ref[...] + g1_ref[0] * y
    xo_ref[...] = x
    h2 = x * lax.rsqrt(jnp.mean(x * x, axis=-1, keepdims=True) + EPS) * nf_ref[...]
    h2 = (h2 * (1.0 + sc2_ref[0]) + sh2_ref[0]).astype(BF16)
    h2_ref[...] = h2
    lg_ref[...] = jnp.dot(h2, wr_ref[...], preferred_element_type=F32) + br_ref[...]


def _outproj(merged, w_out, x, mod3, norm_ffn, w_router_pad, b_router_pad, tm, rows_per_mod, mod_base):
    N, D = x.shape
    R = mod3.shape[1]
    mspec = lambda k: pl.BlockSpec((1, R, D), lambda i: ((mod_base + (i * tm) // rows_per_mod) * 6 + k, 0, 0))
    full = lambda a: pl.BlockSpec(a.shape, lambda i: (0, 0))
    return pl.pallas_call(
        _outproj_kernel,
        grid=(N // tm,),
        in_specs=[pl.BlockSpec((tm, D), lambda i: (i, 0)), full(w_out), pl.BlockSpec((tm, D), lambda i: (i, 0)),
                  mspec(2), mspec(4), mspec(3), full(norm_ffn), full(w_router_pad), full(b_router_pad)],
        out_specs=[pl.BlockSpec((tm, D), lambda i: (i, 0)), pl.BlockSpec((tm, D), lambda i: (i, 0)),
                   pl.BlockSpec((tm, LANE), lambda i: (i, 0))],
        out_shape=[jax.ShapeDtypeStruct((N, D), F32), jax.ShapeDtypeStruct((N, D), BF16),
                   jax.ShapeDtypeStruct((N, LANE), F32)],
        compiler_params=_cparams(("parallel",)),
        name="outproj",
    )(merged, w_out, x, mod3, mod3, mod3, norm_ffn, w_router_pad, b_router_pad)


def _rank_select(score, own, nblk, topk):
    lane = lax.broadcasted_iota(jnp.int32, score.shape, 1)
    past = lane < own
    s = jnp.where(past, score, NEG)
    rank = jnp.zeros(score.shape, jnp.int32)
    for j in range(nblk):
        col = s[:, j:j + 1]
        beats = (col > s) | ((col == s) & (lane > j))
        rank = rank + beats.astype(jnp.int32)
    return (past & (rank < topk)) | (lane == own)


def _moba_select_kernel(q_ref, k_ref, sel_ref):
    T = q_ref.shape[1]
    nb = T // MOBA_BLOCK
    k = k_ref[0]
    kmean = jnp.sum(k.reshape(nb, MOBA_BLOCK, HEAD_DIM), axis=1) * (1.0 / MOBA_BLOCK)
    kmean = jnp.concatenate([kmean, jnp.zeros((LANE - nb, HEAD_DIM), F32)], axis=0)
    score = lax.dot_general(q_ref[0].astype(BF16), kmean.astype(BF16), (((1,), (1,)), ((), ())),
                            preferred_element_type=F32)
    own = lax.broadcasted_iota(jnp.int32, (T, 1), 0) // MOBA_BLOCK
    sel = _rank_select(score, own, nb, MOBA_TOPK)
    sel_ref[0, 0] = sel.astype(BF16)


def _moba_select(za, B, T):
    return pl.pallas_call(
        _moba_select_kernel,
        grid=(B, MOBA_HEADS),
        in_specs=[pl.BlockSpec((1, T, HEAD_DIM), lambda b, h: (b, 0, h)),
                  pl.BlockSpec((1, T, HEAD_DIM), lambda b, h: (b, 0, MOBA_HEADS + h))],
        out_specs=pl.BlockSpec((1, 1, T, LANE), lambda b, h: (b, h, 0, 0)),
        out_shape=jax.ShapeDtypeStruct((B, MOBA_HEADS, T, LANE), BF16),
        compiler_params=_cparams(("parallel", "parallel")),
        name="moba_select",
    )(za, za)


def _flash_kernel(*refs, G, window_tiles, use_sel, per_head_kv):
    if use_sel:
        q_ref, k_ref, v_ref, bias_ref, sel_ref, e_ref, o_ref, m_scr, l_scr, acc_scr = refs
    else:
        q_ref, k_ref, v_ref, bias_ref, o_ref, m_scr, l_scr, acc_scr = refs
    qi = pl.program_id(1)
    kj = pl.program_id(2)
    nk = pl.num_programs(2)
    tq, tk = ATT_TILE, ATT_TILE

    @pl.when(kj == 0)
    def _():
        m_scr[...] = jnp.full(m_scr.shape, NEG, F32)
        l_scr[...] = jnp.zeros(l_scr.shape, F32)
        acc_scr[...] = jnp.zeros(acc_scr.shape, F32)

    lo = jnp.maximum(qi - window_tiles, 0) if window_tiles is not None else 0

    @pl.when((kj >= lo) & (kj <= qi))
    def _():
        row = lax.broadcasted_iota(jnp.int32, (tq, tk), 0)
        col = lax.broadcasted_iota(jnp.int32, (tq, tk), 1)
        dist = (qi - kj) * tq + row - col
        band = dist >= 0
        if window_tiles is not None:
            band = band & (dist < WINDOW)
        mask = band
        for g in range(G):
            kv = slice(g * HEAD_DIM, (g + 1) * HEAD_DIM) if per_head_kv else slice(0, HEAD_DIM)
            if g == 0 or per_head_kv:
                k = k_ref[0, :, kv].astype(BF16)
                v = v_ref[0, :, kv].astype(BF16)
                if use_sel:
                    hit = jnp.dot(sel_ref[0, g], e_ref[...], preferred_element_type=F32)
                    mask = band & (hit > 0.5)
            q = q_ref[0, :, g * HEAD_DIM:(g + 1) * HEAD_DIM].astype(BF16)
            s = lax.dot_general(q, k, (((1,), (1,)), ((), ())), preferred_element_type=F32)
            s = s * SCALE + bias_ref[g, 0]
            s = jnp.where(mask, s, NEG)
            m_prev = m_scr[g]
            m_new = jnp.maximum(m_prev, jnp.max(s, axis=1, keepdims=True))
            p = jnp.where(mask, jnp.exp(s - m_new), 0.0)
            alpha = jnp.exp(m_prev - m_new)
            l_scr[g] = alpha * l_scr[g] + jnp.sum(p, axis=1, keepdims=True)
            acc_scr[g] = alpha * acc_scr[g] + jnp.dot(p.astype(BF16), v, preferred_element_type=F32)
            m_scr[g] = m_new

    @pl.when(kj == nk - 1)
    def _():
        for g in range(G):
            l = l_scr[g]
            o = acc_scr[g] / jnp.where(l > 0.0, l, 1.0)
            o_ref[0, :, g * HEAD_DIM:(g + 1) * HEAD_DIM] = o


def _flash(za, bias, *, B, T, q_col, k_col, v_col, bias_blk, per_head_kv, window_tiles=None, sel=None, emat=None):
    tq = tk = ATT_TILE
    nq, nk = T // tq, T // tk
    G = NSA_HEADS
    use_sel = sel is not None
    kv_w = G * HEAD_DIM if per_head_kv else HEAD_DIM

    def kv_blk(i, j):
        lo = jnp.maximum(i - window_tiles, 0) if window_tiles is not None else 0
        return jnp.clip(j, lo, i)

    in_specs = [
        pl.BlockSpec((1, tq, G * HEAD_DIM), lambda b, i, j: (b, i, q_col)),
        pl.BlockSpec((1, tk, kv_w), lambda b, i, j: (b, kv_blk(i, j), k_col)),
        pl.BlockSpec((1, tk, kv_w), lambda b, i, j: (b, kv_blk(i, j), v_col)),
        pl.BlockSpec((G, 1, tq, tk), lambda b, i, j: (bias_blk, jnp.clip(i - j, 0, 2), 0, 0)),
    ]
    args = [za, za, za, bias]
    if use_sel:
        in_specs += [pl.BlockSpec((1, sel.shape[1], tq, LANE), lambda b, i, j: (b, 0, i, 0)),
                     pl.BlockSpec((LANE, tk), lambda b, i, j: (0, kv_blk(i, j)))]
        args += [sel, emat]
    return pl.pallas_call(
        functools.partial(_flash_kernel, G=G, window_tiles=window_tiles, use_sel=use_sel, per_head_kv=per_head_kv),
        grid=(B, nq, nk),
        in_specs=in_specs,
        out_specs=pl.BlockSpec((1, tq, G * HEAD_DIM), lambda b, i, j: (b, i, 0)),
        out_shape=jax.ShapeDtypeStruct((B, T, G * HEAD_DIM), F32),
        scratch_shapes=[pltpu.VMEM((G, tq, 1), F32), pltpu.VMEM((G, tq, 1), F32),
                        pltpu.VMEM((G, tq, HEAD_DIM), F32)],
        compiler_params=_cparams(("parallel", "parallel", "arbitrary")),
        name="flash",
    )(*args)


def _cmp_kernel(ck_ref, cv_ref, q_ref, pek_ref, pev_ref, wk_ref, wv_ref, bias_ref, ovl_ref,
                o_ref, sel_ref, kc_scr, vc_scr):
    T = ck_ref.shape[1]
    n_ch = T // CMP_STRIDE
    i = pl.program_id(1)
    tq = ATT_TILE

    @pl.when(i == 0)
    def _():
        for src, pe, w, dst in ((ck_ref, pek_ref, wk_ref, kc_scr), (cv_ref, pev_ref, wv_ref, vc_scr)):
            first = jnp.zeros((n_ch, HEAD_DIM), F32)
            second = jnp.zeros((n_ch, HEAD_DIM), F32)
            for l in range(CMP_STRIDE):
                x = src[0, pl.ds(l, n_ch, stride=CMP_STRIDE), :]
                l2 = CMP_STRIDE + l
                first = first + jnp.dot((x + pe[l:l + 1, :]).astype(BF16), w[l * HEAD_DIM:(l + 1) * HEAD_DIM, :],
                                        preferred_element_type=F32)
                second = second + jnp.dot((x + pe[l2:l2 + 1, :]).astype(BF16),
                                          w[l2 * HEAD_DIM:(l2 + 1) * HEAD_DIM, :], preferred_element_type=F32)
            dst[...] = (first + pltpu.roll(second, n_ch - 1, 0)).astype(BF16)

    kc = kc_scr[...]
    vc = vc_scr[...]
    t = i * tq + lax.broadcasted_iota(jnp.int32, (tq, n_ch), 0)
    n = lax.broadcasted_iota(jnp.int32, (tq, n_ch), 1)
    vis = (n * CMP_STRIDE + (CMP_LEN - 1)) <= t
    imp = jnp.zeros((tq, LANE), F32)
    ovl = ovl_ref[...]
    for g in range(NSA_HEADS):
        q = q_ref[0, :, g * HEAD_DIM:(g + 1) * HEAD_DIM].astype(BF16)
        s = lax.dot_general(q, kc, (((1,), (1,)), ((), ())), preferred_element_type=F32)
        s = s * SCALE + bias_ref[g]
        s = jnp.where(vis, s, NEG)
        m = jnp.max(s, axis=1, keepdims=True)
        e = jnp.where(vis, jnp.exp(s - m), 0.0)
        l = jnp.sum(e, axis=1, keepdims=True)
        p = e / jnp.where(l > 0.0, l, 1.0)
        pb = p.astype(BF16)
        o_ref[0, :, g * HEAD_DIM:(g + 1) * HEAD_DIM] = jnp.dot(pb, vc, preferred_element_type=F32)
        imp = imp + jnp.dot(pb, ovl, preferred_element_type=F32)
    own = (i * tq + lax.broadcasted_iota(jnp.int32, (tq, 1), 0)) // SLC_BLOCK
    sel = _rank_select(imp, own, T // SLC_BLOCK, SLC_TOPK - 1)
    sel_ref[0, 0] = sel.astype(BF16)


def _cmp_branch(za, pe_k, pe_v, w_k, w_v, bias_cmp, overlap, B, T):
    assert T // CMP_STRIDE == LANE
    tq = ATT_TILE
    col = 3 * MOBA_HEADS + NSA_HEADS
    return pl.pallas_call(
        _cmp_kernel,
        grid=(B, T // tq),
        in_specs=[
            pl.BlockSpec((1, T, HEAD_DIM), lambda b, i: (b, 0, col)),
            pl.BlockSpec((1, T, HEAD_DIM), lambda b, i: (b, 0, col + 1)),
            pl.BlockSpec((1, tq, NSA_W), lambda b, i: (b, i, 3)),
            pl.BlockSpec((CMP_LEN, HEAD_DIM), lambda b, i: (0, 0)),
            pl.BlockSpec((CMP_LEN, HEAD_DIM), lambda b, i: (0, 0)),
            pl.BlockSpec((CMP_LEN * HEAD_DIM, HEAD_DIM), lambda b, i: (0, 0)),
            pl.BlockSpec((CMP_LEN * HEAD_DIM, HEAD_DIM), lambda b, i: (0, 0)),
            pl.BlockSpec((NSA_HEADS, tq, LANE), lambda b, i: (0, i, 0)),
            pl.BlockSpec((LANE, LANE), lambda b, i: (0, 0)),
        ],
        out_specs=[pl.BlockSpec((1, tq, NSA_W), lambda b, i: (b, i, 0)),
                   pl.BlockSpec((1, 1, tq, LANE), lambda b, i: (b, 0, i, 0))],
        out_shape=[jax.ShapeDtypeStruct((B, T, NSA_W), F32),
                   jax.ShapeDtypeStruct((B, 1, T, LANE), BF16)],
        scratch_shapes=[pltpu.VMEM((LANE, HEAD_DIM), BF16), pltpu.VMEM((LANE, HEAD_DIM), BF16)],
        compiler_params=_cparams(("parallel", "arbitrary")),
        name="nsa_cmp",
    )(za, za, za, pe_k, pe_v, w_k, w_v, bias_cmp, overlap)


def _gelu_tanh(x):
    return 0.5 * x * (1.0 + jnp.tanh(math.sqrt(2.0 / math.pi) * (x + 0.044715 * (x * x * x))))


def _softplus(x):
    return jnp.maximum(x, 0.0) + jnp.log1p(jnp.exp(-jnp.abs(x)))


def _lru_kernel(x_ref, g_ref, c0_ref, h0_ref, cw_ref, cb_ref, wr_ref, br_ref, wi_ref, bi_ref, lam_ref,
                y_ref, hl_ref, cl_ref, a_scr, x_scr):
    T = x_ref.shape[1]
    x = x_ref[0]
    row = lax.broadcasted_iota(jnp.int32, (T, LRU_BLOCK), 0)
    c0 = c0_ref[0]

    def shifted(d):
        r = pltpu.roll(x, d, 0)
        for t in range(d):
            r = jnp.where(row == t, c0[t + CONV_W - 1 - d:t + CONV_W - d, :], r)
        return r

    u = cb_ref[...] + shifted(3) * cw_ref[0:1, :]
    u = u + shifted(2) * cw_ref[1:2, :]
    u = u + shifted(1) * cw_ref[2:3, :]
    u = u + x * cw_ref[3:4, :]
    ub = u.astype(BF16)
    r = jax.nn.sigmoid(jnp.dot(ub, wr_ref[0], preferred_element_type=F32) + br_ref[...])
    ig = jax.nn.sigmoid(jnp.dot(ub, wi_ref[0], preferred_element_type=F32) + bi_ref[...])
    log_a = (-LRU_C * r) * _softplus(-lam_ref[...])
    a = jnp.exp(log_a)
    b = jnp.sqrt(1.0 - jnp.exp(2.0 * log_a)) * (ig * u)
    sub = row % 8
    for d in (1, 2, 4):
        ok = sub >= d
        b = jnp.where(ok, a * pltpu.roll(b, d, 0) + b, b)
        a = jnp.where(ok, a * pltpu.roll(a, d, 0), a)
    a_scr[...] = a
    x_scr[...] = b

    def group(k, carry):
        s = pl.multiple_of(k * 8, 8)
        h = x_scr[pl.ds(s, 8), :] + a_scr[pl.ds(s, 8), :] * carry
        x_scr[pl.ds(s, 8), :] = h
        return h[7:8, :]

    h_last = lax.fori_loop(0, T // 8, group, h0_ref[0], unroll=8)
    y_ref[0] = x_scr[...] * _gelu_tanh(g_ref[0])
    hl_ref[0] = h_last
    cl_ref[0] = x[T - (CONV_W - 1):, :]


def _rglru(zb, conv0, h0, conv_w, conv_b, w_r, b_r, w_i, b_i, lam, B, T):
    nblk = LRU_BLOCKS
    vec = lambda a: a.reshape(1, LRU_WIDTH)
    row_spec = pl.BlockSpec((1, LRU_BLOCK), lambda b, n: (0, n))
    return pl.pallas_call(
        _lru_kernel,
        grid=(B, nblk),
        in_specs=[
            pl.BlockSpec((1, T, LRU_BLOCK), lambda b, n: (b, 0, n)),
            pl.BlockSpec((1, T, LRU_BLOCK), lambda b, n: (b, 0, nblk + n)),
            pl.BlockSpec((1, CONV_W - 1, LRU_BLOCK), lambda b, n: (b, 0, n)),
            pl.BlockSpec((1, 1, LRU_BLOCK), lambda b, n: (b, 0, n)),
            pl.BlockSpec((CONV_W, LRU_BLOCK), lambda b, n: (0, n)),
            row_spec,
            pl.BlockSpec((1, LRU_BLOCK, LRU_BLOCK), lambda b, n: (n, 0, 0)),
            row_spec,
            pl.BlockSpec((1, LRU_BLOCK, LRU_BLOCK), lambda b, n: (n, 0, 0)),
            row_spec,
            row_spec,
        ],
        out_specs=[pl.BlockSpec((1, T, LRU_BLOCK), lambda b, n: (b, 0, n)),
                   pl.BlockSpec((1, 1, LRU_BLOCK), lambda b, n: (b, 0, n)),
                   pl.BlockSpec((1, CONV_W - 1, LRU_BLOCK), lambda b, n: (b, 0, n))],
        out_shape=[jax.ShapeDtypeStruct((B, T, LRU_WIDTH), F32),
                   jax.ShapeDtypeStruct((B, 1, LRU_WIDTH), F32),
                   jax.ShapeDtypeStruct((B, CONV_W - 1, LRU_WIDTH), F32)],
        scratch_shapes=[pltpu.VMEM((T, LRU_BLOCK), F32), pltpu.VMEM((T, LRU_BLOCK), F32)],
        compiler_params=_cparams(("parallel", "parallel")),
        name="rglru",
    )(zb, zb, conv0, h0.reshape(B, 1, LRU_WIDTH), conv_w, vec(conv_b), w_r.astype(BF16), vec(b_r),
      w_i.astype(BF16), vec(b_i), vec(lam))


def _moe_kernel(eid_ref, nt_ref, x_ref, wg_ref, wu_ref, wd_ref, cw_ref, o_ref):
    t = pl.program_id(0)

    @pl.when(t < nt_ref[0])
    def _():
        x = x_ref[...]
        g = jnp.dot(x, wg_ref[0], preferred_element_type=F32)
        u = jnp.dot(x, wu_ref[0], preferred_element_type=F32)
        act = (g * jax.nn.sigmoid(g)) * u
        y = jnp.dot(act.astype(BF16), wd_ref[0], preferred_element_type=F32)
        o_ref[...] = y * cw_ref[...]

    @pl.when(t >= nt_ref[0])
    def _():
        o_ref[...] = jnp.zeros(o_ref.shape, F32)


def _moe_ffn(x_pad, cw_pad, tile_eid, n_tiles, w_g, w_u, w_d):
    P, D = x_pad.shape
    tm = MOE_TILE
    grid_spec = pltpu.PrefetchScalarGridSpec(
        num_scalar_prefetch=2,
        grid=(P // tm,),
        in_specs=[
            pl.BlockSpec((tm, D), lambda t, eid, nt: (t, 0)),
            pl.BlockSpec((1, D, D_EXPERT), lambda t, eid, nt: (eid[t], 0, 0)),
            pl.BlockSpec((1, D, D_EXPERT), lambda t, eid, nt: (eid[t], 0, 0)),
            pl.BlockSpec((1, D_EXPERT, D), lambda t, eid, nt: (eid[t], 0, 0)),
            pl.BlockSpec((tm, 1), lambda t, eid, nt: (t, 0)),
        ],
        out_specs=pl.BlockSpec((tm, D), lambda t, eid, nt: (t, 0)),
    )
    return pl.pallas_call(
        _moe_kernel,
        grid_spec=grid_spec,
        out_shape=jax.ShapeDtypeStruct((P, D), F32),
        compiler_params=_cparams(("arbitrary",)),
        name="moe_ffn",
    )(tile_eid, n_tiles, x_pad, w_g, w_u, w_d, cw_pad)


def _route(logits):
    probs = jax.nn.softmax(logits.astype(F32), axis=-1)
    grp = probs.reshape(-1, N_GROUPS, EXPERTS_PER_GROUP)
    pairs = [grp[..., a] + grp[..., b] for a in range(EXPERTS_PER_GROUP) for b in range(a + 1, EXPERTS_PER_GROUP)]
    g_score = functools.reduce(jnp.maximum, pairs)
    g_best = jnp.argmax(g_score, axis=-1)
    pick = g_best[:, None, None] == jnp.arange(N_GROUPS)[None, :, None]
    in_grp = jnp.sum(jnp.where(pick, grp, 0.0), axis=1)
    i1 = jnp.argmax(in_grp, axis=-1)
    first = jnp.arange(EXPERTS_PER_GROUP)[None, :] == i1[:, None]
    w1 = jnp.max(in_grp, axis=-1)
    rest = jnp.where(first, -jnp.inf, in_grp)
    i2 = jnp.argmax(rest, axis=-1)
    w2 = jnp.max(rest, axis=-1)
    w_top = jnp.stack([w1, w2], axis=-1)
    w_top = w_top / jnp.sum(w_top, axis=-1, keepdims=True)
    e_idx = g_best[:, None] * EXPERTS_PER_GROUP + jnp.stack([i1, i2], axis=-1)
    return e_idx.astype(jnp.int32), w_top


def _moe_prompt(h2, logits, w_g, w_u, w_d):
    N, D = h2.shape
    tm = MOE_TILE
    e_idx, w_top = _route(logits)
    flat_e = e_idx.reshape(-1)
    onehot = (flat_e[:, None] == jnp.arange(N_EXPERTS)[None, :]).astype(jnp.int32)
    within = jnp.sum(onehot * jnp.cumsum(onehot, axis=0), axis=1) - 1
    sizes = jnp.sum(onehot, axis=0)
    padded = ((sizes + tm - 1) // tm) * tm
    pend = jnp.cumsum(padded)
    pstart = pend - padded
    dest = jnp.sum(onehot * pstart[None, :], axis=1) + within
    P = 2 * N + N_EXPERTS * tm
    src_tok = jnp.zeros((P,), jnp.int32).at[dest].set(jnp.arange(2 * N, dtype=jnp.int32) // TOP_K)
    cw_pad = jnp.zeros((P,), F32).at[dest].set(w_top.reshape(-1))
    x_pad = h2[src_tok]
    tile_start = jnp.arange(P // tm, dtype=jnp.int32) * tm
    tile_eid = jnp.minimum(jnp.sum(tile_start[:, None] >= pend[None, :], axis=1), N_EXPERTS - 1).astype(jnp.int32)
    n_tiles = (pend[-1] // tm).astype(jnp.int32).reshape(1)
    out = _moe_ffn(x_pad, cw_pad.reshape(P, 1), tile_eid, n_tiles, w_g, w_u, w_d)
    pos = dest.reshape(N, TOP_K)
    return out[pos[:, 0]] + out[pos[:, 1]]


def _t5_bucket(dist):
    n = jnp.maximum(dist, 0)
    exact = T5_BUCKETS // 2
    nf = jnp.maximum(n, 1).astype(F32)
    large = exact + (jnp.log(nf / exact) / math.log(T5_MAX_DIST / exact) * (T5_BUCKETS - exact)).astype(jnp.int32)
    return jnp.where(n < exact, n, jnp.minimum(large, T5_BUCKETS - 1))


def _bias_lookup(tab, dist):
    bucket = _t5_bucket(dist)
    out = jnp.zeros((tab.shape[0],) + bucket.shape, F32)
    for b in range(T5_BUCKETS):
        out = jnp.where(bucket[None] == b, tab[:, b].reshape((-1,) + (1,) * bucket.ndim), out)
    return out


def _rms(x, g):
    return x * lax.rsqrt(jnp.mean(x * x, axis=-1, keepdims=True) + EPS) * g


def _split(z, widths):
    outs, s = [], 0
    for w in widths:
        outs.append(z[..., s:s + w])
        s += w
    return outs


def _prompt_tables(t5_bias, T):
    t = ATT_TILE
    i = jnp.arange(t)
    dist = jnp.arange(3)[:, None, None] * t + i[None, :, None] - i[None, None, :]
    bias_tiles = _bias_lookup(t5_bias.T, dist)
    pos = jnp.arange(T)
    lane = jnp.arange(LANE)
    e_moba = (pos[None, :] // MOBA_BLOCK == lane[:, None]).astype(BF16)
    e_slc = (pos[None, :] // SLC_BLOCK == lane[:, None]).astype(BF16)
    d_cmp = pos[:, None] - (lane[None, :] * CMP_STRIDE + CMP_LEN - 1)
    bias_cmp = _bias_lookup(t5_bias.T[MOBA_HEADS:], d_cmp)
    c_start = lane * CMP_STRIDE
    s_start = lane * SLC_BLOCK
    n_cmp = T // CMP_STRIDE - 1
    overlap = ((c_start[:, None] < s_start[None, :] + SLC_BLOCK) & (c_start[:, None] + CMP_LEN > s_start[None, :])
               & (lane[:, None] < n_cmp) & (lane[None, :] < T // SLC_BLOCK)).astype(BF16)
    return bias_tiles, e_moba, e_slc, bias_cmp, overlap


def _mixer_prompt(x, mod, mod3, lp, tabs, w_router_pad, b_router_pad):
    B, T, D = x.shape
    N = B * T
    bias_tiles, e_moba, e_slc, bias_cmp, overlap = tabs
    sh1, sc1 = mod[:, None, :D], mod[:, None, D:2 * D]
    h = (_rms(x, lp['norm_mix']) * (1.0 + sc1) + sh1).astype(BF16).reshape(N, D)
    za = _matmul(h, lp['w_a'], 256, ZA_W).reshape(B, T, ZA_W)
    zb = _matmul(h, lp['w_b'], 512, 2048).reshape(B, T, -1)
    new_kv = jnp.concatenate([za[..., MOBA_W:3 * MOBA_W], za[..., 3 * MOBA_W + NSA_W:COL_NG - WIN_DIM]], axis=-1)
    win = za[:, T - WINDOW:, COL_NG - WIN_DIM:COL_NG]

    sel_moba = _moba_select(za, B, T)
    o_a = _flash(za, bias_tiles, B=B, T=T, q_col=0, k_col=1, v_col=2, bias_blk=0, per_head_kv=True,
                 sel=sel_moba, emat=e_moba)
    o_cmp, sel_slc = _cmp_branch(za, lp['cmp_pos_k'], lp['cmp_pos_v'], lp['w_cmp_k'].reshape(-1, HEAD_DIM),
                                 lp['w_cmp_v'].reshape(-1, HEAD_DIM), bias_cmp, overlap, B, T)
    o_slc = _flash(za, bias_tiles, B=B, T=T, q_col=3, k_col=18, v_col=19, bias_blk=1, per_head_kv=False,
                   sel=sel_slc, emat=e_slc)
    o_win = _flash(za, bias_tiles, B=B, T=T, q_col=3, k_col=20, v_col=21, bias_blk=1, per_head_kv=False,
                   window_tiles=WINDOW // ATT_TILE)
    o_c, h_last, conv_last = _rglru(zb, jnp.zeros((B, CONV_W - 1, LRU_WIDTH), F32), jnp.zeros((B, LRU_WIDTH), F32),
                                    lp['conv_w'], lp['conv_b'], lp['w_rg'], lp['b_rg'], lp['w_ig'], lp['b_ig'],
                                    lp['lru_lambda'], B, T)
    flat = lambda a: a.reshape(N, a.shape[-1])
    merged = _merge(flat(o_a), flat(o_cmp), flat(o_slc), flat(o_win), flat(za), flat(o_c), flat(zb),
                    lp['w_br_moba'], lp['w_br_nsa'], lp['w_br_lru'], 256)
    x, h2, logits = _outproj(merged, lp['w_out'], x.reshape(N, D), mod3, lp['norm_ffn'].reshape(1, D),
                             w_router_pad, b_router_pad, 256, T, 0)
    return x, h2, logits, new_kv, win, h_last.reshape(B, LRU_WIDTH), conv_last


def _mixer_decode(l, x, mod, lp, cache_kv, cache_win, page_table, h0, conv0, dtabs, t5_bias, w_router_pad,
                  b_router_pad):
    B, D = x.shape
    pad = lambda a: jnp.pad(a, ((0, DEC_ROWS - B),) + ((0, 0),) * (a.ndim - 1))
    sh1, sc1 = mod[:, :D], mod[:, D:2 * D]
    h = pad((_rms(x, lp['norm_mix']) * (1.0 + sc1) + sh1).astype(BF16))
    za = _matmul(h, lp['w_a'], DEC_ROWS, ZA_W)
    zb = _matmul(h, lp['w_b'], DEC_ROWS, 2048)
    new_kv = jnp.concatenate([za[:B, MOBA_W:3 * MOBA_W], za[:B, 3 * MOBA_W + NSA_W:COL_NG - WIN_DIM]], axis=-1)
    n_win = cache_win.shape[2]
    win = jnp.concatenate([cache_win[l], za[:B, None, COL_NG - WIN_DIM:COL_NG]], axis=1)
    win = win[:, win.shape[1] - min(WINDOW, n_win + 1):]
    cmp_w = (lp['cmp_pos_k'], lp['cmp_pos_v'], lp['w_cmp_k'], lp['w_cmp_v'])
    o_a, o_cmp, o_slc, o_win = _decode_attention(l, za, cache_kv, cache_win, page_table, cmp_w, dtabs, t5_bias)
    o_c, h_new, conv_t = _lru_step(zb, conv0.transpose(1, 0, 2), h0, lp['conv_w'], lp['conv_b'], lp['w_rg'],
                                   lp['b_rg'], lp['w_ig'], lp['b_ig'], lp['lru_lambda'])
    merged = _merge(o_a, o_cmp, o_slc, o_win, za, pad(o_c), zb, lp['w_br_moba'], lp['w_br_nsa'], lp['w_br_lru'],
                    DEC_ROWS)
    mod3 = pad(mod).reshape(DEC_ROWS, 6, D).transpose(1, 0, 2)
    x, h2, logits = _outproj(merged, lp['w_out'], pad(x), mod3, lp['norm_ffn'].reshape(1, D), w_router_pad,
                             b_router_pad, DEC_ROWS, DEC_ROWS, 0)
    return x[:B], h2, logits, new_kv, win, h_new, conv_t.transpose(1, 0, 2)


SCAN_PAGES = 32
DEC_ROWS = 16


def _page_scan_kernel(pt_ref, *refs):
    del pt_ref
    ck_refs, cv_refs, mk_refs = refs[:SCAN_PAGES], refs[SCAN_PAGES:2 * SCAN_PAGES], refs[2 * SCAN_PAGES:3 * SCAN_PAGES]
    pek_ref, pev_ref, wk_ref, wv_ref, abk_ref, abv_ref, km_ref = refs[3 * SCAN_PAGES:]
    chunks = PAGE_SIZE // CMP_STRIDE
    for src, pe, w, dst in ((ck_refs, pek_ref, wk_ref, abk_ref), (cv_refs, pev_ref, wv_ref, abv_ref)):
        first = jnp.zeros((SCAN_PAGES * chunks, HEAD_DIM), F32)
        second = jnp.zeros((SCAN_PAGES * chunks, HEAD_DIM), F32)
        for l in range(CMP_STRIDE):
            x = jnp.concatenate([r[0, 0, pl.ds(l, chunks, stride=CMP_STRIDE), :] for r in src], axis=0)
            l2 = CMP_STRIDE + l
            first = first + jnp.dot((x + pe[l:l + 1, :]).astype(BF16), w[l], preferred_element_type=F32)
            second = second + jnp.dot((x + pe[l2:l2 + 1, :]).astype(BF16), w[l2], preferred_element_type=F32)
        dst[0, :, 0:HEAD_DIM] = first
        dst[0, :, HEAD_DIM:2 * HEAD_DIM] = second
    sums = [jnp.sum(r[0, 0], axis=0, keepdims=True) for r in mk_refs]
    per_blk = MOBA_BLOCK // PAGE_SIZE
    means = [functools.reduce(lambda a, b: a + b, sums[j * per_blk:(j + 1) * per_blk])
             for j in range(SCAN_PAGES // per_blk)]
    km_ref[0] = jnp.concatenate(means, axis=0) * (1.0 / MOBA_BLOCK)


def _page_scan(cache_kv, l, page_table, pe_k, pe_v, w_k, w_v):
    B, n_pages = page_table.shape
    steps = n_pages // SCAN_PAGES
    chunks = PAGE_SIZE // CMP_STRIDE
    blocks = SCAN_PAGES * PAGE_SIZE // MOBA_BLOCK
    cmp_col = 2 * MOBA_W // HEAD_DIM

    def cmp_spec(j, col):
        return pl.BlockSpec((1, 1, PAGE_SIZE, HEAD_DIM), lambda b, s, pt: (l, pt[b, s * SCAN_PAGES + j], 0, col))

    def mk_spec(j):
        return pl.BlockSpec((1, 1, PAGE_SIZE, MOBA_W), lambda b, s, pt: (l, pt[b, s * SCAN_PAGES + j], 0, 0))

    const = lambda a: pl.BlockSpec(a.shape, lambda b, s, pt: (0,) * a.ndim)
    grid_spec = pltpu.PrefetchScalarGridSpec(
        num_scalar_prefetch=1,
        grid=(B, steps),
        in_specs=([cmp_spec(j, cmp_col) for j in range(SCAN_PAGES)]
                  + [cmp_spec(j, cmp_col + 1) for j in range(SCAN_PAGES)] + [mk_spec(j) for j in range(SCAN_PAGES)]
                  + [const(pe_k), const(pe_v), const(w_k), const(w_v)]),
        out_specs=[pl.BlockSpec((1, SCAN_PAGES * chunks, 2 * HEAD_DIM), lambda b, s, pt: (b, s, 0)),
                   pl.BlockSpec((1, SCAN_PAGES * chunks, 2 * HEAD_DIM), lambda b, s, pt: (b, s, 0)),
                   pl.BlockSpec((1, blocks, MOBA_W), lambda b, s, pt: (b, s, 0))],
    )
    n_ch = n_pages * chunks
    return pl.pallas_call(
        _page_scan_kernel,
        grid_spec=grid_spec,
        out_shape=[jax.ShapeDtypeStruct((B, n_ch, 2 * HEAD_DIM), F32),
                   jax.ShapeDtypeStruct((B, n_ch, 2 * HEAD_DIM), F32),
                   jax.ShapeDtypeStruct((B, steps * blocks, MOBA_W), F32)],
        compiler_params=_cparams(("parallel", "arbitrary")),
        name="page_scan",
    )(page_table, *([cache_kv] * (3 * SCAN_PAGES)), pe_k, pe_v, w_k, w_v)


def _topk_lanes(score, k):
    lane = lax.broadcasted_iota(jnp.int32, score.shape, 1)
    out = jnp.zeros(score.shape, jnp.int32)
    for r in range(k):
        m = jnp.max(score, axis=1, keepdims=True)
        idx = jnp.min(jnp.where(score == m, lane, score.shape[1]), axis=1, keepdims=True)
        out = jnp.where(lane == r, idx, out)
        score = jnp.where(lane == idx, -jnp.inf, score)
    return out


def _head_rows(q):
    rows = [q[:, g * HEAD_DIM:(g + 1) * HEAD_DIM] for g in range(NSA_HEADS)]
    return jnp.concatenate(rows + [jnp.zeros((8 - NSA_HEADS, HEAD_DIM), q.dtype)], axis=0)


def _dec_select_kernel(abk_ref, abv_ref, km_ref, mq_ref, nq_ref, bias_ref, ovl_ref, ocmp_ref, tm_ref, ts_ref,
                       *, past):
    b = pl.program_id(0)
    n_ch = abk_ref.shape[1]
    nb = km_ref.shape[1]
    row = lax.broadcasted_iota(jnp.int32, (8, LANE), 0)
    mq = mq_ref[pl.ds(b, 1), :]
    sc = jnp.full((8, LANE), NEG, F32)
    for h in range(MOBA_HEADS):
        hs = slice(h * HEAD_DIM, (h + 1) * HEAD_DIM)
        qh = jnp.broadcast_to(mq[:, hs], (8, HEAD_DIM)).astype(BF16)
        s = lax.dot_general(qh, km_ref[0, :, hs].astype(BF16), (((1,), (1,)), ((), ())), preferred_element_type=F32)
        s = jnp.concatenate([s, jnp.full((8, LANE - nb), NEG, F32)], axis=1)
        sc = jnp.where(row == h, s, sc)
    tm_ref[0] = _topk_lanes(sc, MOBA_TOPK)
    abk = abk_ref[0]
    abv = abv_ref[0]
    kc = (abk[:, :HEAD_DIM] + pltpu.roll(abk[:, HEAD_DIM:], n_ch - 1, 0)).astype(BF16)
    vc = (abv[:, :HEAD_DIM] + pltpu.roll(abv[:, HEAD_DIM:], n_ch - 1, 0)).astype(BF16)
    q4 = _head_rows(nq_ref[pl.ds(b, 1), :]).astype(BF16)
    s = lax.dot_general(q4, kc, (((1,), (1,)), ((), ())), preferred_element_type=F32) * SCALE + bias_ref[...]
    n = lax.broadcasted_iota(jnp.int32, (8, n_ch), 1)
    vis = (n < n_ch - 1) & (n * CMP_STRIDE + (CMP_LEN - 1) <= past)
    s = jnp.where(vis, s, NEG)
    m = jnp.max(s, axis=1, keepdims=True)
    e = jnp.where(vis, jnp.exp(s - m), 0.0)
    den = jnp.sum(e, axis=1, keepdims=True)
    pb = (e / jnp.where(den > 0.0, den, 1.0)).astype(BF16)
    ocmp_ref[0] = jnp.dot(pb, vc, preferred_element_type=F32)
    imp = jnp.dot(pb, ovl_ref[...], preferred_element_type=F32)
    rows = lax.broadcasted_iota(jnp.int32, imp.shape, 0)
    imp = jnp.sum(jnp.where(rows < NSA_HEADS, imp, 0.0), axis=0, keepdims=True)
    ts_ref[0] = _topk_lanes(jnp.broadcast_to(imp, (8, imp.shape[1])), SLC_TOPK - 1)[:, :LANE]


def _dec_select(abk, abv, kmean, za_s, bias_cmp, overlap, past):
    B, n_ch, _ = abk.shape
    full = lambda a: pl.BlockSpec(a.shape, lambda b: (0,) * a.ndim)
    per_b = lambda a: pl.BlockSpec((1,) + a.shape[1:], lambda b: (b, 0, 0))
    out = jax.ShapeDtypeStruct((B, 8, LANE), F32)
    outi = jax.ShapeDtypeStruct((B, 8, LANE), jnp.int32)
    return pl.pallas_call(
        functools.partial(_dec_select_kernel, past=past),
        grid=(B,),
        in_specs=[per_b(abk), per_b(abv), per_b(kmean),
                  pl.BlockSpec((DEC_ROWS, MOBA_W), lambda b: (0, 0)),
                  pl.BlockSpec((DEC_ROWS, NSA_W), lambda b: (0, 3)),
                  full(bias_cmp), full(overlap)],
        out_specs=[pl.BlockSpec((1, 8, LANE), lambda b: (b, 0, 0))] * 3,
        out_shape=[out, outi, outi],
        compiler_params=_cparams(("parallel",)),
        name="dec_select",
    )(abk, abv, kmean, za_s, za_s, bias_cmp, overlap)


def _softmax_parts(scores):
    m = functools.reduce(jnp.maximum, [jnp.max(s, axis=1, keepdims=True) for s in scores])
    es = [jnp.exp(s - m) for s in scores]
    den = functools.reduce(lambda a, b: a + b, [jnp.sum(e, axis=1, keepdims=True) for e in es])
    return [e / den for e in es]


def _dec_attend_kernel(pm_ref, ps_ref, hs_ref, *refs, n_moba, n_slc, n_win):
    del pm_ref, ps_ref, hs_ref
    mk, mv = refs[:n_moba], refs[n_moba:2 * n_moba]
    sk, sv = refs[2 * n_moba:2 * n_moba + n_slc], refs[2 * n_moba + n_slc:2 * (n_moba + n_slc)]
    (kw_ref, vw_ref, mq_ref, mkn_ref, mvn_ref, nq_ref, skn_ref, svn_ref, wkn_ref, wvn_ref, bm_ref, bs_ref, bw_ref,
     oa_ref, oslc_ref, owin_ref) = refs[2 * (n_moba + n_slc):]
    b = pl.program_id(0)
    h = pl.program_id(1)
    nt = (((1,), (1,)), ((), ()))
    rnd = lambda a: a.astype(BF16).astype(F32)

    q = mq_ref[pl.ds(b, 1), :]
    q8 = jnp.broadcast_to(q, (8, HEAD_DIM)).astype(BF16)
    bm = bm_ref[0]
    scores = [lax.dot_general(q8, mk[j][0, 0].astype(BF16), nt, preferred_element_type=F32)[0:1] * SCALE
              + bm[j:j + 1] for j in range(n_moba)]
    s_new = jnp.sum(rnd(q) * rnd(mkn_ref[pl.ds(b, 1), :]), axis=1, keepdims=True) * SCALE + bm[n_moba:n_moba + 1, 0:1]
    probs = _softmax_parts(scores + [s_new])
    o = rnd(probs[-1]) * rnd(mvn_ref[pl.ds(b, 1), :])
    for j in range(n_moba):
        pj = jnp.broadcast_to(probs[j], (8, PAGE_SIZE)).astype(BF16)
        o = o + jnp.dot(pj, mv[j][0, 0].astype(BF16), preferred_element_type=F32)[0:1]
    oa_ref[0] = jnp.broadcast_to(o, (8, HEAD_DIM))

    @pl.when(h == 0)
    def _():
        q4f = _head_rows(nq_ref[pl.ds(b, 1), :])
        q4 = q4f.astype(BF16)
        scores = [lax.dot_general(q4, sk[j][0, 0].astype(BF16), nt, preferred_element_type=F32) * SCALE + bs_ref[0, j]
                  for j in range(n_slc)]
        s_new = (jnp.sum(rnd(q4f) * rnd(skn_ref[pl.ds(b, 1), :]), axis=1, keepdims=True) * SCALE
                 + bs_ref[0, n_slc][:, 0:1])
        probs = _softmax_parts(scores + [s_new])
        o = rnd(probs[-1]) * rnd(svn_ref[pl.ds(b, 1), :])
        for j in range(n_slc):
            o = o + jnp.dot(probs[j].astype(BF16), sv[j][0, 0].astype(BF16), preferred_element_type=F32)
        oslc_ref[0] = o
        s = lax.dot_general(q4, kw_ref[0, 0].astype(BF16), nt, preferred_element_type=F32) * SCALE + bw_ref[:, :n_win]
        i = lax.broadcasted_iota(jnp.int32, (8, n_win), 1)
        s = jnp.where(n_win - i < WINDOW, s, NEG)
        s_new = (jnp.sum(rnd(q4f) * rnd(wkn_ref[pl.ds(b, 1), :]), axis=1, keepdims=True) * SCALE
                 + bw_ref[:, n_win:n_win + 1])
        p_win, p_new = _softmax_parts([s, s_new])
        p_win = jnp.where(n_win - i < WINDOW, p_win, 0.0)
        owin_ref[0] = (rnd(p_new) * rnd(wvn_ref[pl.ds(b, 1), :])
                       + jnp.dot(p_win.astype(BF16), vw_ref[0, 0].astype(BF16), preferred_element_type=F32))


def _dec_attend(cache_kv, cache_win, l, za_s, pages_moba, pages_slc, halves_slc, bias_moba, bias_slc, bias_win, B):
    n_moba = MOBA_TOPK * MOBA_BLOCK // PAGE_SIZE
    n_slc = SLC_TOPK - 1
    n_win = cache_win.shape[2]

    def moba_spec(j, col0):
        return pl.BlockSpec((1, 1, PAGE_SIZE, HEAD_DIM),
                            lambda b, h, pm, ps, hs: (l, pm[(b * MOBA_HEADS + h) * n_moba + j], 0, col0 + h))

    def slc_spec(j, col):
        return pl.BlockSpec((1, 1, SLC_BLOCK, HEAD_DIM),
                            lambda b, h, pm, ps, hs: (l, ps[b * n_slc + j], hs[b * n_slc + j], col))

    zcol = lambda w, c: pl.BlockSpec((DEC_ROWS, w), lambda b, h, pm, ps, hs: (0, c))
    zhead = lambda c0: pl.BlockSpec((DEC_ROWS, HEAD_DIM), lambda b, h, pm, ps, hs: (0, c0 + h))
    slc_col = (2 * MOBA_W + 2 * HEAD_DIM) // HEAD_DIM
    in_specs = ([moba_spec(j, 0) for j in range(n_moba)] + [moba_spec(j, MOBA_HEADS) for j in range(n_moba)]
                + [slc_spec(j, slc_col) for j in range(n_slc)] + [slc_spec(j, slc_col + 1) for j in range(n_slc)]
                + [pl.BlockSpec((1, 1, n_win, HEAD_DIM), lambda b, h, pm, ps, hs: (l, b, 0, 0)),
                   pl.BlockSpec((1, 1, n_win, HEAD_DIM), lambda b, h, pm, ps, hs: (l, b, 0, 1)),
                   zhead(0), zhead(MOBA_HEADS), zhead(2 * MOBA_HEADS), zcol(NSA_W, 3),
                   zcol(HEAD_DIM, 18), zcol(HEAD_DIM, 19), zcol(HEAD_DIM, 20), zcol(HEAD_DIM, 21),
                   pl.BlockSpec((1, 8, LANE), lambda b, h, pm, ps, hs: (b * MOBA_HEADS + h, 0, 0)),
                   pl.BlockSpec((1, n_slc + 1, 8, SLC_BLOCK), lambda b, h, pm, ps, hs: (b, 0, 0, 0)),
                   pl.BlockSpec(bias_win.shape, lambda b, h, pm, ps, hs: (0, 0))])
    grid_spec = pltpu.PrefetchScalarGridSpec(
        num_scalar_prefetch=3,
        grid=(B, MOBA_HEADS),
        in_specs=in_specs,
        out_specs=[pl.BlockSpec((1, 8, HEAD_DIM), lambda b, h, pm, ps, hs: (b * MOBA_HEADS + h, 0, 0)),
                   pl.BlockSpec((1, 8, HEAD_DIM), lambda b, h, pm, ps, hs: (b, 0, 0)),
                   pl.BlockSpec((1, 8, HEAD_DIM), lambda b, h, pm, ps, hs: (b, 0, 0))],
    )
    return pl.pallas_call(
        functools.partial(_dec_attend_kernel, n_moba=n_moba, n_slc=n_slc, n_win=n_win),
        grid_spec=grid_spec,
        out_shape=[jax.ShapeDtypeStruct((B * MOBA_HEADS, 8, HEAD_DIM), F32),
                   jax.ShapeDtypeStruct((B, 8, HEAD_DIM), F32), jax.ShapeDtypeStruct((B, 8, HEAD_DIM), F32)],
        compiler_params=_cparams(("parallel", "arbitrary")),
        name="dec_attend",
    )(pages_moba, pages_slc, halves_slc, *([cache_kv] * (2 * n_moba + 2 * n_slc)), cache_win, cache_win,
      za_s, za_s, za_s, za_s, za_s, za_s, za_s, za_s, bias_moba, bias_slc, bias_win)


def _lru_step_kernel(x_ref, g_ref, c0_ref, h0_ref, cw_ref, cb_ref, wr_ref, br_ref, wi_ref, bi_ref, lam_ref,
                     y_ref, h_ref, cl_ref):
    B = h0_ref.shape[0]
    x = x_ref[0:B, :]
    u = cb_ref[...] + c0_ref[0] * cw_ref[0:1, :]
    u = u + c0_ref[1] * cw_ref[1:2, :]
    u = u + c0_ref[2] * cw_ref[2:3, :]
    u = u + x * cw_ref[3:4, :]
    ub = u.astype(BF16)
    r = jax.nn.sigmoid(jnp.dot(ub, wr_ref[0], preferred_element_type=F32) + br_ref[...])
    ig = jax.nn.sigmoid(jnp.dot(ub, wi_ref[0], preferred_element_type=F32) + bi_ref[...])
    log_a = (-LRU_C * r) * _softplus(-lam_ref[...])
    h = jnp.exp(log_a) * h0_ref[...] + jnp.sqrt(1.0 - jnp.exp(2.0 * log_a)) * (ig * u)
    h_ref[...] = h
    y_ref[...] = h * _gelu_tanh(g_ref[0:B, :])
    cl_ref[0] = c0_ref[1]
    cl_ref[1] = c0_ref[2]
    cl_ref[2] = x


def _lru_step(zb_s, conv0_t, h0, conv_w, conv_b, w_r, b_r, w_i, b_i, lam):
    B = h0.shape[0]
    nblk = LRU_BLOCKS
    vec = lambda a: a.reshape(1, LRU_WIDTH)
    row_spec = pl.BlockSpec((1, LRU_BLOCK), lambda n: (0, n))
    bw = lambda: pl.BlockSpec((B, LRU_BLOCK), lambda n: (0, n))
    return pl.pallas_call(
        _lru_step_kernel,
        grid=(nblk,),
        in_specs=[pl.BlockSpec((DEC_ROWS, LRU_BLOCK), lambda n: (0, n)),
                  pl.BlockSpec((DEC_ROWS, LRU_BLOCK), lambda n: (0, nblk + n)),
                  pl.BlockSpec((CONV_W - 1, B, LRU_BLOCK), lambda n: (0, 0, n)), bw(),
                  pl.BlockSpec((CONV_W, LRU_BLOCK), lambda n: (0, n)), row_spec,
                  pl.BlockSpec((1, LRU_BLOCK, LRU_BLOCK), lambda n: (n, 0, 0)), row_spec,
                  pl.BlockSpec((1, LRU_BLOCK, LRU_BLOCK), lambda n: (n, 0, 0)), row_spec, row_spec],
        out_specs=[bw(), bw(), pl.BlockSpec((CONV_W - 1, B, LRU_BLOCK), lambda n: (0, 0, n))],
        out_shape=[jax.ShapeDtypeStruct((B, LRU_WIDTH), F32), jax.ShapeDtypeStruct((B, LRU_WIDTH), F32),
                   jax.ShapeDtypeStruct((CONV_W - 1, B, LRU_WIDTH), F32)],
        compiler_params=_cparams(("parallel",)),
        name="lru_step",
    )(zb_s, zb_s, conv0_t, h0, conv_w, vec(conv_b), w_r, vec(b_r), w_i, vec(b_i), vec(lam))


def _bias_heads(tab, dist):
    bucket = _t5_bucket(dist)
    out = jnp.zeros(bucket.shape, F32)
    shape = (1, -1) + (1,) * (bucket.ndim - 2)
    for bkt in range(T5_BUCKETS):
        out = jnp.where(bucket == bkt, tab[:, bkt].reshape(shape), out)
    return out


def _decode_tables(t5_bias, past, n_win):
    tab_n = t5_bias.T[MOBA_HEADS:]
    n_ch = past // CMP_STRIDE
    n = jnp.arange(n_ch)
    bias_cmp = jnp.pad(_bias_lookup(tab_n, past - (n * CMP_STRIDE + CMP_LEN - 1)), ((0, 8 - NSA_HEADS), (0, 0)))
    s_start = jnp.arange(past // SLC_BLOCK) * SLC_BLOCK
    c_start = n * CMP_STRIDE
    overlap = ((c_start[:, None] < s_start[None, :] + SLC_BLOCK) & (c_start[:, None] + CMP_LEN > s_start[None, :])
               & (n[:, None] < n_ch - 1)).astype(BF16)
    d_win = jnp.concatenate([n_win - jnp.arange(n_win), jnp.zeros((LANE,), jnp.int32)])
    bias_win = jnp.pad(_bias_lookup(tab_n, d_win), ((0, 8 - NSA_HEADS), (0, 0)))
    return bias_cmp, overlap, bias_win


def _decode_attention(l, za_s, cache_kv, cache_win, page_table, lp_cmp, dtabs, t5_bias):
    B, n_pages = page_table.shape
    past = n_pages * PAGE_SIZE
    bias_cmp, overlap, bias_win = dtabs
    pe_k, pe_v, w_k, w_v = lp_cmp
    abk, abv, kmean = _page_scan(cache_kv, l, page_table, pe_k, pe_v, w_k, w_v)
    o_cmp, top_m, top_s = _dec_select(abk, abv, kmean, za_s, bias_cmp, overlap, past)
    top_m = top_m[:, :MOBA_HEADS, :MOBA_TOPK]
    top_s = top_s[:, 0, :SLC_TOPK - 1]
    tab = t5_bias.T
    bidx = jnp.arange(B)
    per_blk = MOBA_BLOCK // PAGE_SIZE
    pg_off = top_m[..., None] * per_blk + jnp.arange(per_blk)
    pages_moba = page_table[bidx[:, None, None, None], pg_off].reshape(-1)
    kpos = (pg_off.reshape(B, MOBA_HEADS, -1, 1) * PAGE_SIZE + jnp.arange(PAGE_SIZE))
    bm = _bias_heads(tab[:MOBA_HEADS], past - kpos)
    bm_new = jnp.zeros((B, MOBA_HEADS, 1, LANE), F32).at[..., 0].set(tab[None, :MOBA_HEADS, 0, None])
    bias_moba = jnp.concatenate([bm, bm_new, jnp.zeros((B, MOBA_HEADS, 1, LANE), F32)], axis=2)
    bias_moba = bias_moba.reshape(B * MOBA_HEADS, 8, LANE)
    per_page = PAGE_SIZE // SLC_BLOCK
    pages_slc = page_table[bidx[:, None], top_s // per_page].reshape(-1)
    halves_slc = (top_s % per_page).reshape(-1)
    spos = top_s[:, None, :, None] * SLC_BLOCK + jnp.arange(SLC_BLOCK)
    bs = _bias_heads(tab[MOBA_HEADS:], jnp.broadcast_to(past - spos, (B, NSA_HEADS, SLC_TOPK - 1, SLC_BLOCK)))
    bs_new = jnp.zeros((B, NSA_HEADS, 1, SLC_BLOCK), F32).at[..., 0].set(tab[None, MOBA_HEADS:, 0, None])
    bias_slc = jnp.concatenate([bs, bs_new], axis=2).transpose(0, 2, 1, 3)
    bias_slc = jnp.pad(bias_slc, ((0, 0), (0, 0), (0, 8 - NSA_HEADS), (0, 0)))
    o_a, o_slc, o_win = _dec_attend(cache_kv, cache_win, l, za_s, pages_moba, pages_slc, halves_slc,
                                    bias_moba, bias_slc, bias_win, B)
    pad = lambda a: jnp.pad(a, ((0, DEC_ROWS - B), (0, 0)))
    heads = lambda a: pad(a[:, :NSA_HEADS].reshape(B, NSA_W))
    return pad(o_a[:, 0].reshape(B, MOBA_W)), heads(o_cmp), heads(o_slc), heads(o_win)


def kernel(x_prompt, x_sample, cache_kv, cache_win, state_lru_h, state_lru_conv, page_table, c_prompt, c_sample,
           w_ada, b_ada, norm_mix, norm_ffn, w_in, cmp_pos_k, cmp_pos_v, w_cmp_k, w_cmp_v, conv_w, conv_b,
           w_rg, b_rg, w_ig, b_ig, lru_lambda, w_br_moba, w_br_nsa, w_br_lru, w_out, w_e_gate, w_e_up, w_e_down,
           t5_bias, w_router, b_router, norm_final):
    n_p, T, D = x_prompt.shape
    n_s = x_sample.shape[0]
    xp = x_prompt
    xs = x_sample.reshape(n_s, D)
    tabs = _prompt_tables(t5_bias, T)
    w_router_pad = jnp.pad(w_router, ((0, 0), (0, LANE - N_EXPERTS))).astype(BF16)
    b_router_pad = jnp.pad(b_router, (0, LANE - N_EXPERTS)).reshape(1, LANE)
    c_all = jax.nn.silu(jnp.concatenate([c_prompt, c_sample], axis=0))
    c_all = jnp.pad(c_all, ((0, 16 - n_p - n_s), (0, 0)))
    dtabs = _decode_tables(t5_bias, page_table.shape[1] * PAGE_SIZE, cache_win.shape[2])
    moe_rows = LANE
    outs = [[] for _ in range(8)]
    for l in range(DEPTH):
        lp = {'norm_mix': norm_mix[l], 'norm_ffn': norm_ffn[l],
              'w_a': w_in[l][:, :ZA_W].astype(BF16), 'w_b': w_in[l][:, COL_LX:].astype(BF16),
              'cmp_pos_k': cmp_pos_k[l], 'cmp_pos_v': cmp_pos_v[l],
              'w_cmp_k': w_cmp_k[l].astype(BF16), 'w_cmp_v': w_cmp_v[l].astype(BF16),
              'conv_w': conv_w[l], 'conv_b': conv_b[l], 'w_rg': w_rg[l].astype(BF16), 'b_rg': b_rg[l],
              'w_ig': w_ig[l].astype(BF16), 'b_ig': b_ig[l], 'lru_lambda': lru_lambda[l],
              'w_br_moba': w_br_moba[l].astype(BF16), 'w_br_nsa': w_br_nsa[l].astype(BF16),
              'w_br_lru': w_br_lru[l].astype(BF16), 'w_out': w_out[l].astype(BF16)}
        mod = _matmul(c_all, w_ada[l], 16, 2048) + b_ada[l]
        mod3 = mod.reshape(16 * 6, 1, D)
        mod_s = mod[n_p:n_p + n_s]
        xp, h2_p, lg_p, kv_p, win_p, h_p, conv_p = _mixer_prompt(xp, mod[:n_p], mod3, lp, tabs, w_router_pad,
                                                                 b_router_pad)
        xs, h2_s, lg_s, kv_s, win_s, h_s, conv_s = _mixer_decode(l, xs, mod_s, lp, cache_kv, cache_win, page_table,
                                                                 state_lru_h[l], state_lru_conv[l], dtabs, t5_bias,
                                                                 w_router_pad, b_router_pad)
        n_tok = n_p * T
        tail = ((0, moe_rows - h2_s.shape[0]), (0, 0))
        h2 = jnp.concatenate([h2_p, jnp.pad(h2_s, tail)], axis=0)
        logits = jnp.concatenate([lg_p, jnp.pad(lg_s, tail)], axis=0)[:, :N_EXPERTS]
        moe = _moe_prompt(h2, logits, w_e_gate[l].astype(BF16), w_e_up[l].astype(BF16), w_e_down[l].astype(BF16))
        xp = xp.reshape(n_p, T, D) + mod[:n_p, None, 5 * D:] * moe[:n_tok].reshape(n_p, T, D)
        xs = xs + mod_s[:, 5 * D:] * moe[n_tok:n_tok + n_s]
        for lst, val in zip(outs, (kv_p, kv_s[:, None, :], win_p, win_s, h_p, h_s, conv_p, conv_s)):
            lst.append(val)
    y_prompt = _rms(xp, norm_final)
    y_sample = _rms(xs, norm_final).reshape(n_s, 1, D)
    return (y_prompt, y_sample) + tuple(jnp.stack(o) for o in outs)
```

```python
import functools
import math

import jax
import jax.numpy as jnp
from jax import lax
from jax.experimental import pallas as pl
from jax.experimental.pallas import tpu as pltpu

D_MODEL = 2048
DEPTH = 2
PAGE_SIZE = 128
HEAD_DIM = 128
MOBA_HEADS = 4
MOBA_BLOCK = 256
MOBA_TOPK = 3
NSA_HEADS = 4
CMP_STRIDE = 16
CMP_LEN = 32
SLC_BLOCK = 64
SLC_TOPK = 16
WINDOW = 512
LRU_WIDTH = D_MODEL // 2
LRU_BLOCKS = 8
LRU_BLOCK = LRU_WIDTH // LRU_BLOCKS
LRU_C = 8.0
CONV_W = 4
N_EXPERTS = 16
N_GROUPS = 4
EXPERTS_PER_GROUP = N_EXPERTS // N_GROUPS
TOP_K = 2
D_EXPERT = D_MODEL // 2
T5_BUCKETS = 32
T5_MAX_DIST = 128
EPS = 1e-6
NEG = -1e30

MOBA_W = MOBA_HEADS * HEAD_DIM
NSA_W = NSA_HEADS * HEAD_DIM
KV_DIM = 2 * MOBA_W + 4 * HEAD_DIM
WIN_DIM = 2 * HEAD_DIM
IN_WIDTHS = (MOBA_W, MOBA_W, MOBA_W, NSA_W, HEAD_DIM, HEAD_DIM, HEAD_DIM, HEAD_DIM, HEAD_DIM, HEAD_DIM,
             3 * NSA_HEADS, LRU_WIDTH, LRU_WIDTH, D_MODEL, D_MODEL, D_MODEL)
COL_NG = 3 * MOBA_W + NSA_W + 6 * HEAD_DIM
COL_LX = COL_NG + 3 * NSA_HEADS
ZA_TILE = 512
ZA_W = -(-(COL_NG + 3 * NSA_HEADS) // ZA_TILE) * ZA_TILE
ZB_TILE = 256
ZB_W = 2 * LRU_WIDTH + 3 * D_MODEL
SCALE = HEAD_DIM ** -0.5

LANE = 128
ATT_TILE = 512
MOE_TILE = 256
VMEM_LIMIT = 56 * 1024 * 1024

F32 = jnp.float32
BF16 = jnp.bfloat16


def _cparams(sem):
    return pltpu.CompilerParams(dimension_semantics=sem, vmem_limit_bytes=VMEM_LIMIT)


def _mm_kernel(x_ref, w_ref, o_ref):
    o_ref[...] = jnp.dot(x_ref[...].astype(BF16), w_ref[...].astype(BF16), preferred_element_type=F32)


def _matmul(x, w, tm, tn):
    M, K = x.shape
    N = w.shape[1]
    assert M % tm == 0 and N % tn == 0
    return pl.pallas_call(
        _mm_kernel,
        grid=(N // tn, M // tm),
        in_specs=[pl.BlockSpec((tm, K), lambda j, i: (i, 0)),
                  pl.BlockSpec((K, tn), lambda j, i: (0, j))],
        out_specs=pl.BlockSpec((tm, tn), lambda j, i: (i, j)),
        out_shape=jax.ShapeDtypeStruct((M, N), F32),
        compiler_params=_cparams(("parallel", "parallel")),
        name="matmul",
    )(x, w)


def _mm_w32_kernel(x_ref, w_ref, *rest, shift):
    if shift:
        wn_ref, o_ref, wb_scr = rest
    else:
        o_ref, wb_scr = rest

    @pl.when(pl.program_id(1) == 0)
    def _():
        w = w_ref[0]
        if shift:
            both = jnp.concatenate([w, wn_ref[0]], axis=1)
            w = pltpu.roll(both, both.shape[1] - shift, 1)[:, :w.shape[1]]
        wb_scr[...] = w.astype(BF16)

    o_ref[...] = jnp.dot(x_ref[...], wb_scr[...], preferred_element_type=F32)


def _matmul_w32(x, w3, l, col0, n_out, tn, tm):
    M, K = x.shape
    shift = col0 % LANE
    blk0 = (col0 - shift) // tn
    assert n_out % tn == 0 and M % tm == 0 and (col0 - shift) % tn == 0
    in_specs = [pl.BlockSpec((tm, K), lambda j, i: (i, 0)),
                pl.BlockSpec((1, K, tn), lambda j, i: (l, 0, blk0 + j))]
    args = [x, w3]
    if shift:
        in_specs.append(pl.BlockSpec((1, K, LANE), lambda j, i: (l, 0, (blk0 + j + 1) * (tn // LANE))))
        args.append(w3)
    return pl.pallas_call(
        functools.partial(_mm_w32_kernel, shift=shift),
        grid=(n_out // tn, M // tm),
        in_specs=in_specs,
        out_specs=pl.BlockSpec((tm, tn), lambda j, i: (i, j)),
        out_shape=jax.ShapeDtypeStruct((M, n_out), F32),
        scratch_shapes=[pltpu.VMEM((K, tn), BF16)],
        compiler_params=_cparams(("parallel", "arbitrary")),
        name="matmul_w32",
    )(*args)


def _merge_kernel(oa_ref, ocmp_ref, oslc_ref, owin_ref, ng_ref, oc_ref, ga_ref, gb_ref, gc_ref,
                  wa_ref, wb_ref, wc_ref, o_ref):
    gt = jax.nn.sigmoid(ng_ref[...])
    heads = []
    for g in range(NSA_HEADS):
        sl = slice(g * HEAD_DIM, (g + 1) * HEAD_DIM)
        heads.append(gt[:, 3 * g:3 * g + 1] * ocmp_ref[:, sl] + gt[:, 3 * g + 1:3 * g + 2] * oslc_ref[:, sl]
                     + gt[:, 3 * g + 2:3 * g + 3] * owin_ref[:, sl])
    ob = jnp.concatenate(heads, axis=1).astype(BF16)
    pa = jnp.dot(oa_ref[...].astype(BF16), wa_ref[...], preferred_element_type=F32)
    pb = jnp.dot(ob, wb_ref[...], preferred_element_type=F32)
    pc = jnp.dot(oc_ref[...].astype(BF16), wc_ref[...], preferred_element_type=F32)
    merged = (jax.nn.sigmoid(ga_ref[...]) * pa + jax.nn.sigmoid(gb_ref[...]) * pb
              + jax.nn.sigmoid(gc_ref[...]) * pc)
    o_ref[...] = merged.astype(BF16)


def _merge(o_a, o_cmp, o_slc, o_win, za, o_c, zb, w_a, w_b, w_c, tm):
    N = o_a.shape[0]
    D = D_MODEL
    row = lambda w, c: pl.BlockSpec((tm, w), lambda i: (i, c))
    full = lambda a: pl.BlockSpec(a.shape, lambda i: (0, 0))
    return pl.pallas_call(
        _merge_kernel,
        grid=(N // tm,),
        in_specs=[row(MOBA_W, 0), row(NSA_W, 0), row(NSA_W, 0), row(NSA_W, 0), row(LANE, COL_NG // LANE),
                  row(LRU_WIDTH, 0), row(D, 1), row(D, 2), row(D, 3), full(w_a), full(w_b), full(w_c)],
        out_specs=pl.BlockSpec((tm, D), lambda i: (i, 0)),
        out_shape=jax.ShapeDtypeStruct((N, D), BF16),
        compiler_params=_cparams(("parallel",)),
        name="merge",
    )(o_a, o_cmp, o_slc, o_win, za, o_c, zb, zb, zb, w_a, w_b, w_c)


def _outproj_kernel(m_ref, w_ref, x_ref, g1_ref, sc2_ref, sh2_ref, nf_ref, wr_ref, br_ref,
                    xo_ref, h2_ref, lg_ref):
    y = jnp.dot(m_ref[...], w_ref[...], preferred_element_type=F32)
    x = x_ref[...] + g1_ref[0] * y
    xo_ref[...] = x
    h2 = x * lax.rsqrt(jnp.mean(x * x, axis=-1, keepdims=True) + EPS) * nf_ref[...]
    h2 = (h2 * (1.0 + sc2_ref[0]) + sh2_ref[0]).astype(BF16)
    h2_ref[...] = h2
    lg_ref[...] = jnp.dot(h2, wr_ref[...], preferred_element_type=F32) + br_ref[...]


def _outproj(merged, w_out, x, mod3, norm_ffn, w_router_pad, b_router_pad, tm, rows_per_mod, mod_base):
    N, D = x.shape
    R = mod3.shape[1]
    mspec = lambda k: pl.BlockSpec((1, R, D), lambda i: ((mod_base + (i * tm) // rows_per_mod) * 6 + k, 0, 0))
    full = lambda a: pl.BlockSpec(a.shape, lambda i: (0, 0))
    return pl.pallas_call(
        _outproj_kernel,
        grid=(N // tm,),
        in_specs=[pl.BlockSpec((tm, D), lambda i: (i, 0)), full(w_out), pl.BlockSpec((tm, D), lambda i: (i, 0)),
                  mspec(2), mspec(4), mspec(3), full(norm_ffn), full(w_router_pad), full(b_router_pad)],
        out_specs=[pl.BlockSpec((tm, D), lambda i: (i, 0)), pl.BlockSpec((tm, D), lambda i: (i, 0)),
                   pl.BlockSpec((tm, LANE), lambda i: (i, 0))],
        out_shape=[jax.ShapeDtypeStruct((N, D), F32), jax.ShapeDtypeStruct((N, D), BF16),
                   jax.ShapeDtypeStruct((N, LANE), F32)],
        compiler_params=_cparams(("parallel",)),
        name="outproj",
    )(merged, w_out, x, mod3, mod3, mod3, norm_ffn, w_router_pad, b_router_pad)


def _rank_select(score, own, nblk, topk):
    lane = lax.broadcasted_iota(jnp.int32, score.shape, 1)
    past = lane < own
    s = jnp.where(past, score, NEG)
    rank = jnp.zeros(score.shape, jnp.int32)
    for j in range(nblk):
        col = s[:, j:j + 1]
        beats = (col > s) | ((col == s) & (lane > j))
        rank = rank + beats.astype(jnp.int32)
    return (past & (rank < topk)) | (lane == own)


def _moba_select_kernel(q_ref, k_ref, sel_ref):
    T = q_ref.shape[1]
    nb = T // MOBA_BLOCK
    k = k_ref[0]
    kmean = jnp.sum(k.reshape(nb, MOBA_BLOCK, HEAD_DIM), axis=1) * (1.0 / MOBA_BLOCK)
    kmean = jnp.concatenate([kmean, jnp.zeros((LANE - nb, HEAD_DIM), F32)], axis=0)
    score = lax.dot_general(q_ref[0].astype(BF16), kmean.astype(BF16), (((1,), (1,)), ((), ())),
                            preferred_element_type=F32)
    own = lax.broadcasted_iota(jnp.int32, (T, 1), 0) // MOBA_BLOCK
    sel = _rank_select(score, own, nb, MOBA_TOPK)
    sel_ref[0, 0] = sel.astype(BF16)


def _moba_select(za, B, T):
    return pl.pallas_call(
        _moba_select_kernel,
        grid=(B, MOBA_HEADS),
        in_specs=[pl.BlockSpec((1, T, HEAD_DIM), lambda b, h: (b, 0, h)),
                  pl.BlockSpec((1, T, HEAD_DIM), lambda b, h: (b, 0, MOBA_HEADS + h))],
        out_specs=pl.BlockSpec((1, 1, T, LANE), lambda b, h: (b, h, 0, 0)),
        out_shape=jax.ShapeDtypeStruct((B, MOBA_HEADS, T, LANE), BF16),
        compiler_params=_cparams(("parallel", "parallel")),
        name="moba_select",
    )(za, za)


def _flash_kernel(*refs, G, window_tiles, use_sel, per_head_kv):
    if use_sel:
        q_ref, k_ref, v_ref, bias_ref, sel_ref, e_ref, o_ref, m_scr, l_scr, acc_scr = refs
    else:
        q_ref, k_ref, v_ref, bias_ref, o_ref, m_scr, l_scr, acc_scr = refs
    qi = pl.program_id(1)
    kj = pl.program_id(2)
    nk = pl.num_programs(2)
    tq, tk = ATT_TILE, ATT_TILE

    @pl.when(kj == 0)
    def _():
        m_scr[...] = jnp.full(m_scr.shape, NEG, F32)
        l_scr[...] = jnp.zeros(l_scr.shape, F32)
        acc_scr[...] = jnp.zeros(acc_scr.shape, F32)

    lo = jnp.maximum(qi - window_tiles, 0) if window_tiles is not None else 0

    @pl.when((kj >= lo) & (kj <= qi))
    def _():
        row = lax.broadcasted_iota(jnp.int32, (tq, tk), 0)
        col = lax.broadcasted_iota(jnp.int32, (tq, tk), 1)
        dist = (qi - kj) * tq + row - col
        band = dist >= 0
        if window_tiles is not None:
            band = band & (dist < WINDOW)
        mask = band
        for g in range(G):
            kv = slice(g * HEAD_DIM, (g + 1) * HEAD_DIM) if per_head_kv else slice(0, HEAD_DIM)
            if g == 0 or per_head_kv:
                k = k_ref[0, :, kv].astype(BF16)
                v = v_ref[0, :, kv].astype(BF16)
                if use_sel:
                    hit = jnp.dot(sel_ref[0, g], e_ref[...], preferred_element_type=F32)
                    mask = band & (hit > 0.5)
            q = q_ref[0, :, g * HEAD_DIM:(g + 1) * HEAD_DIM].astype(BF16)
            s = lax.dot_general(q, k, (((1,), (1,)), ((), ())), preferred_element_type=F32)
            s = s * SCALE + bias_ref[g, 0]
            s = jnp.where(mask, s, NEG)
            m_prev = m_scr[g]
            m_new = jnp.maximum(m_prev, jnp.max(s, axis=1, keepdims=True))
            p = jnp.exp(s - m_new)
            alpha = jnp.exp(m_prev - m_new)
            l_scr[g] = alpha * l_scr[g] + jnp.sum(p, axis=1, keepdims=True)
            acc_scr[g] = alpha * acc_scr[g] + jnp.dot(p.astype(BF16), v, preferred_element_type=F32)
            m_scr[g] = m_new

    @pl.when(kj == nk - 1)
    def _():
        for g in range(G):
            l = l_scr[g]
            o = acc_scr[g] / jnp.where(l > 0.0, l, 1.0)
            o_ref[0, :, g * HEAD_DIM:(g + 1) * HEAD_DIM] = o


def _flash(za, bias, *, B, T, q_col, k_col, v_col, bias_blk, per_head_kv, window_tiles=None, sel=None, emat=None):
    tq = tk = ATT_TILE
    nq, nk = T // tq, T // tk
    G = NSA_HEADS
    use_sel = sel is not None
    kv_w = G * HEAD_DIM if per_head_kv else HEAD_DIM

    def kv_blk(i, j):
        lo = jnp.maximum(i - window_tiles, 0) if window_tiles is not None else 0
        return jnp.clip(j, lo, i)

    in_specs = [
        pl.BlockSpec((1, tq, G * HEAD_DIM), lambda b, i, j: (b, i, q_col)),
        pl.BlockSpec((1, tk, kv_w), lambda b, i, j: (b, kv_blk(i, j), k_col)),
        pl.BlockSpec((1, tk, kv_w), lambda b, i, j: (b, kv_blk(i, j), v_col)),
        pl.BlockSpec((G, 1, tq, tk), lambda b, i, j: (bias_blk, jnp.clip(i - j, 0, 2), 0, 0)),
    ]
    args = [za, za, za, bias]
    if use_sel:
        in_specs += [pl.BlockSpec((1, sel.shape[1], tq, LANE), lambda b, i, j: (b, 0, i, 0)),
                     pl.BlockSpec((LANE, tk), lambda b, i, j: (0, kv_blk(i, j)))]
        args += [sel, emat]
    return pl.pallas_call(
        functools.partial(_flash_kernel, G=G, window_tiles=window_tiles, use_sel=use_sel, per_head_kv=per_head_kv),
        grid=(B, nq, nk),
        in_specs=in_specs,
        out_specs=pl.BlockSpec((1, tq, G * HEAD_DIM), lambda b, i, j: (b, i, 0)),
        out_shape=jax.ShapeDtypeStruct((B, T, G * HEAD_DIM), F32),
        scratch_shapes=[pltpu.VMEM((G, tq, 1), F32), pltpu.VMEM((G, tq, 1), F32),
                        pltpu.VMEM((G, tq, HEAD_DIM), F32)],
        compiler_params=_cparams(("parallel", "parallel", "arbitrary")),
        name="flash",
    )(*args)


def _cmp_kernel(ck_ref, cv_ref, q_ref, pek_ref, pev_ref, wk_ref, wv_ref, bias_ref, ovl_ref,
                o_ref, sel_ref, kc_scr, vc_scr):
    T = ck_ref.shape[1]
    n_ch = T // CMP_STRIDE
    i = pl.program_id(1)
    tq = ATT_TILE

    @pl.when(i == 0)
    def _():
        for src, pe, w, dst in ((ck_ref, pek_ref, wk_ref, kc_scr), (cv_ref, pev_ref, wv_ref, vc_scr)):
            first = jnp.zeros((n_ch, HEAD_DIM), F32)
            second = jnp.zeros((n_ch, HEAD_DIM), F32)
            for l in range(CMP_STRIDE):
                x = src[0, pl.ds(l, n_ch, stride=CMP_STRIDE), :]
                l2 = CMP_STRIDE + l
                first = first + jnp.dot((x + pe[l:l + 1, :]).astype(BF16), w[l * HEAD_DIM:(l + 1) * HEAD_DIM, :],
                                        preferred_element_type=F32)
                second = second + jnp.dot((x + pe[l2:l2 + 1, :]).astype(BF16),
                                          w[l2 * HEAD_DIM:(l2 + 1) * HEAD_DIM, :], preferred_element_type=F32)
            dst[...] = (first + pltpu.roll(second, n_ch - 1, 0)).astype(BF16)

    kc = kc_scr[...]
    vc = vc_scr[...]
    t = i * tq + lax.broadcasted_iota(jnp.int32, (tq, n_ch), 0)
    n = lax.broadcasted_iota(jnp.int32, (tq, n_ch), 1)
    vis = (n * CMP_STRIDE + (CMP_LEN - 1)) <= t
    imp = jnp.zeros((tq, LANE), F32)
    ovl = ovl_ref[...]
    for g in range(NSA_HEADS):
        q = q_ref[0, :, g * HEAD_DIM:(g + 1) * HEAD_DIM].astype(BF16)
        s = lax.dot_general(q, kc, (((1,), (1,)), ((), ())), preferred_element_type=F32)
        s = s * SCALE + bias_ref[g]
        s = jnp.where(vis, s, NEG)
        m = jnp.max(s, axis=1, keepdims=True)
        e = jnp.where(vis, jnp.exp(s - m), 0.0)
        l = jnp.sum(e, axis=1, keepdims=True)
        p = e / jnp.where(l > 0.0, l, 1.0)
        pb = p.astype(BF16)
        o_ref[0, :, g * HEAD_DIM:(g + 1) * HEAD_DIM] = jnp.dot(pb, vc, preferred_element_type=F32)
        imp = imp + jnp.dot(pb, ovl, preferred_element_type=F32)
    own = (i * tq + lax.broadcasted_iota(jnp.int32, (tq, 1), 0)) // SLC_BLOCK
    sel = _rank_select(imp, own, T // SLC_BLOCK, SLC_TOPK - 1)
    sel_ref[0, 0] = sel.astype(BF16)


def _cmp_branch(za, pe_k, pe_v, w_k, w_v, bias_cmp, overlap, B, T):
    assert T // CMP_STRIDE == LANE
    tq = ATT_TILE
    col = 3 * MOBA_HEADS + NSA_HEADS
    return pl.pallas_call(
        _cmp_kernel,
        grid=(B, T // tq),
        in_specs=[
            pl.BlockSpec((1, T, HEAD_DIM), lambda b, i: (b, 0, col)),
            pl.BlockSpec((1, T, HEAD_DIM), lambda b, i: (b, 0, col + 1)),
            pl.BlockSpec((1, tq, NSA_W), lambda b, i: (b, i, 3)),
            pl.BlockSpec((CMP_LEN, HEAD_DIM), lambda b, i: (0, 0)),
            pl.BlockSpec((CMP_LEN, HEAD_DIM), lambda b, i: (0, 0)),
            pl.BlockSpec((CMP_LEN * HEAD_DIM, HEAD_DIM), lambda b, i: (0, 0)),
            pl.BlockSpec((CMP_LEN * HEAD_DIM, HEAD_DIM), lambda b, i: (0, 0)),
            pl.BlockSpec((NSA_HEADS, tq, LANE), lambda b, i: (0, i, 0)),
            pl.BlockSpec((LANE, LANE), lambda b, i: (0, 0)),
        ],
        out_specs=[pl.BlockSpec((1, tq, NSA_W), lambda b, i: (b, i, 0)),
                   pl.BlockSpec((1, 1, tq, LANE), lambda b, i: (b, 0, i, 0))],
        out_shape=[jax.ShapeDtypeStruct((B, T, NSA_W), F32),
                   jax.ShapeDtypeStruct((B, 1, T, LANE), BF16)],
        scratch_shapes=[pltpu.VMEM((LANE, HEAD_DIM), BF16), pltpu.VMEM((LANE, HEAD_DIM), BF16)],
        compiler_params=_cparams(("parallel", "arbitrary")),
        name="nsa_cmp",
    )(za, za, za, pe_k, pe_v, w_k, w_v, bias_cmp, overlap)


def _gelu_tanh(x):
    return 0.5 * x * (1.0 + jnp.tanh(math.sqrt(2.0 / math.pi) * (x + 0.044715 * (x * x * x))))


def _softplus(x):
    return jnp.maximum(x, 0.0) + jnp.log1p(jnp.exp(-jnp.abs(x)))


def _lru_kernel(x_ref, g_ref, c0_ref, h0_ref, cw_ref, cb_ref, wr_ref, br_ref, wi_ref, bi_ref, lam_ref,
                y_ref, hl_ref, cl_ref, a_scr, x_scr):
    T = x_ref.shape[1]
    x = x_ref[0]
    row = lax.broadcasted_iota(jnp.int32, (T, LRU_BLOCK), 0)
    c0 = c0_ref[0]

    def shifted(d):
        r = pltpu.roll(x, d, 0)
        for t in range(d):
            r = jnp.where(row == t, c0[t + CONV_W - 1 - d:t + CONV_W - d, :], r)
        return r

    u = cb_ref[...] + shifted(3) * cw_ref[0:1, :]
    u = u + shifted(2) * cw_ref[1:2, :]
    u = u + shifted(1) * cw_ref[2:3, :]
    u = u + x * cw_ref[3:4, :]
    ub = u.astype(BF16)
    r = jax.nn.sigmoid(jnp.dot(ub, wr_ref[0], preferred_element_type=F32) + br_ref[...])
    ig = jax.nn.sigmoid(jnp.dot(ub, wi_ref[0], preferred_element_type=F32) + bi_ref[...])
    log_a = (-LRU_C * r) * _softplus(-lam_ref[...])
    a = jnp.exp(log_a)
    b = jnp.sqrt(1.0 - jnp.exp(2.0 * log_a)) * (ig * u)
    sub = row % 8
    for d in (1, 2, 4):
        ok = sub >= d
        b = jnp.where(ok, a * pltpu.roll(b, d, 0) + b, b)
        a = jnp.where(ok, a * pltpu.roll(a, d, 0), a)
    a_scr[...] = a
    x_scr[...] = b

    def group(k, carry):
        s = pl.multiple_of(k * 8, 8)
        h = x_scr[pl.ds(s, 8), :] + a_scr[pl.ds(s, 8), :] * carry
        x_scr[pl.ds(s, 8), :] = h
        return h[7:8, :]

    h_last = lax.fori_loop(0, T // 8, group, h0_ref[0], unroll=8)
    y_ref[0] = x_scr[...] * _gelu_tanh(g_ref[0])
    hl_ref[0] = h_last
    cl_ref[0] = x[T - (CONV_W - 1):, :]


def _rglru(zb, conv0, h0, conv_w, conv_b, w_r, b_r, w_i, b_i, lam, B, T):
    nblk = LRU_BLOCKS
    vec = lambda a: a.reshape(1, LRU_WIDTH)
    row_spec = pl.BlockSpec((1, LRU_BLOCK), lambda b, n: (0, n))
    return pl.pallas_call(
        _lru_kernel,
        grid=(B, nblk),
        in_specs=[
            pl.BlockSpec((1, T, LRU_BLOCK), lambda b, n: (b, 0, n)),
            pl.BlockSpec((1, T, LRU_BLOCK), lambda b, n: (b, 0, nblk + n)),
            pl.BlockSpec((1, CONV_W - 1, LRU_BLOCK), lambda b, n: (b, 0, n)),
            pl.BlockSpec((1, 1, LRU_BLOCK), lambda b, n: (b, 0, n)),
            pl.BlockSpec((CONV_W, LRU_BLOCK), lambda b, n: (0, n)),
            row_spec,
            pl.BlockSpec((1, LRU_BLOCK, LRU_BLOCK), lambda b, n: (n, 0, 0)),
            row_spec,
            pl.BlockSpec((1, LRU_BLOCK, LRU_BLOCK), lambda b, n: (n, 0, 0)),
            row_spec,
            row_spec,
        ],
        out_specs=[pl.BlockSpec((1, T, LRU_BLOCK), lambda b, n: (b, 0, n)),
                   pl.BlockSpec((1, 1, LRU_BLOCK), lambda b, n: (b, 0, n)),
                   pl.BlockSpec((1, CONV_W - 1, LRU_BLOCK), lambda b, n: (b, 0, n))],
        out_shape=[jax.ShapeDtypeStruct((B, T, LRU_WIDTH), F32),
                   jax.ShapeDtypeStruct((B, 1, LRU_WIDTH), F32),
                   jax.ShapeDtypeStruct((B, CONV_W - 1, LRU_WIDTH), F32)],
        scratch_shapes=[pltpu.VMEM((T, LRU_BLOCK), F32), pltpu.VMEM((T, LRU_BLOCK), F32)],
        compiler_params=_cparams(("parallel", "parallel")),
        name="rglru",
    )(zb, zb, conv0, h0.reshape(B, 1, LRU_WIDTH), conv_w, vec(conv_b), w_r.astype(BF16), vec(b_r),
      w_i.astype(BF16), vec(b_i), vec(lam))


def _moe_kernel(eid_ref, nt_ref, x_ref, wg_ref, wu_ref, wd_ref, cw_ref, o_ref, wg_scr, wu_scr, wd_scr):
    t = pl.program_id(0)
    live = t < nt_ref[0]
    new_expert = (t == 0) | (eid_ref[t] != eid_ref[jnp.maximum(t - 1, 0)])

    @pl.when(live & new_expert)
    def _():
        wg_scr[...] = wg_ref[0, 0].astype(BF16)
        wu_scr[...] = wu_ref[0, 0].astype(BF16)
        wd_scr[...] = wd_ref[0, 0].astype(BF16)

    @pl.when(live)
    def _():
        x = x_ref[...]
        g = jnp.dot(x, wg_scr[...], preferred_element_type=F32)
        u = jnp.dot(x, wu_scr[...], preferred_element_type=F32)
        act = (g * jax.nn.sigmoid(g)) * u
        y = jnp.dot(act.astype(BF16), wd_scr[...], preferred_element_type=F32)
        o_ref[...] = y * cw_ref[...]

    @pl.when(t >= nt_ref[0])
    def _():
        o_ref[...] = jnp.zeros(o_ref.shape, F32)


def _moe_ffn(x_pad, cw_pad, tile_eid, n_tiles, l, w_g, w_u, w_d):
    P, D = x_pad.shape
    tm = MOE_TILE
    grid_spec = pltpu.PrefetchScalarGridSpec(
        num_scalar_prefetch=2,
        grid=(P // tm,),
        in_specs=[
            pl.BlockSpec((tm, D), lambda t, eid, nt: (t, 0)),
            pl.BlockSpec((1, 1, D, D_EXPERT), lambda t, eid, nt: (l, eid[t], 0, 0), pipeline_mode=pl.Buffered(1)),
            pl.BlockSpec((1, 1, D, D_EXPERT), lambda t, eid, nt: (l, eid[t], 0, 0), pipeline_mode=pl.Buffered(1)),
            pl.BlockSpec((1, 1, D_EXPERT, D), lambda t, eid, nt: (l, eid[t], 0, 0), pipeline_mode=pl.Buffered(1)),
            pl.BlockSpec((tm, 1), lambda t, eid, nt: (t, 0)),
        ],
        out_specs=pl.BlockSpec((tm, D), lambda t, eid, nt: (t, 0)),
        scratch_shapes=[pltpu.VMEM((D, D_EXPERT), BF16), pltpu.VMEM((D, D_EXPERT), BF16),
                        pltpu.VMEM((D_EXPERT, D), BF16)],
    )
    return pl.pallas_call(
        _moe_kernel,
        grid_spec=grid_spec,
        out_shape=jax.ShapeDtypeStruct((P, D), F32),
        compiler_params=_cparams(("arbitrary",)),
        name="moe_ffn",
    )(tile_eid, n_tiles, x_pad, w_g, w_u, w_d, cw_pad)


def _route(logits):
    probs = jax.nn.softmax(logits.astype(F32), axis=-1)
    grp = probs.reshape(-1, N_GROUPS, EXPERTS_PER_GROUP)
    pairs = [grp[..., a] + grp[..., b] for a in range(EXPERTS_PER_GROUP) for b in range(a + 1, EXPERTS_PER_GROUP)]
    g_score = functools.reduce(jnp.maximum, pairs)
    g_best = jnp.argmax(g_score, axis=-1)
    pick = g_best[:, None, None] == jnp.arange(N_GROUPS)[None, :, None]
    in_grp = jnp.sum(jnp.where(pick, grp, 0.0), axis=1)
    i1 = jnp.argmax(in_grp, axis=-1)
    first = jnp.arange(EXPERTS_PER_GROUP)[None, :] == i1[:, None]
    w1 = jnp.max(in_grp, axis=-1)
    rest = jnp.where(first, -jnp.inf, in_grp)
    i2 = jnp.argmax(rest, axis=-1)
    w2 = jnp.max(rest, axis=-1)
    w_top = jnp.stack([w1, w2], axis=-1)
    w_top = w_top / jnp.sum(w_top, axis=-1, keepdims=True)
    e_idx = g_best[:, None] * EXPERTS_PER_GROUP + jnp.stack([i1, i2], axis=-1)
    return e_idx.astype(jnp.int32), w_top


def _moe_prompt(h2, logits, l, w_g, w_u, w_d):
    N, D = h2.shape
    tm = MOE_TILE
    e_idx, w_top = _route(logits)
    flat_e = e_idx.reshape(-1)
    onehot = (flat_e[:, None] == jnp.arange(N_EXPERTS)[None, :]).astype(jnp.int32)
    within = jnp.sum(onehot * jnp.cumsum(onehot, axis=0), axis=1) - 1
    sizes = jnp.sum(onehot, axis=0)
    padded = ((sizes + tm - 1) // tm) * tm
    pend = jnp.cumsum(padded)
    pstart = pend - padded
    dest = jnp.sum(onehot * pstart[None, :], axis=1) + within
    P = 2 * N + N_EXPERTS * tm
    src_tok = jnp.zeros((P,), jnp.int32).at[dest].set(jnp.arange(2 * N, dtype=jnp.int32) // TOP_K)
    cw_pad = jnp.zeros((P,), F32).at[dest].set(w_top.reshape(-1))
    x_pad = h2[src_tok]
    tile_start = jnp.arange(P // tm, dtype=jnp.int32) * tm
    tile_eid = jnp.minimum(jnp.sum(tile_start[:, None] >= pend[None, :], axis=1), N_EXPERTS - 1).astype(jnp.int32)
    n_tiles = (pend[-1] // tm).astype(jnp.int32).reshape(1)
    out = _moe_ffn(x_pad, cw_pad.reshape(P, 1), tile_eid, n_tiles, l, w_g, w_u, w_d)
    pos = dest.reshape(N, TOP_K)
    return out[pos[:, 0]] + out[pos[:, 1]]


def _t5_bucket(dist):
    n = jnp.maximum(dist, 0)
    exact = T5_BUCKETS // 2
    nf = jnp.maximum(n, 1).astype(F32)
    large = exact + (jnp.log(nf / exact) / math.log(T5_MAX_DIST / exact) * (T5_BUCKETS - exact)).astype(jnp.int32)
    return jnp.where(n < exact, n, jnp.minimum(large, T5_BUCKETS - 1))


def _bias_lookup(tab, dist):
    bucket = _t5_bucket(dist)
    out = jnp.zeros((tab.shape[0],) + bucket.shape, F32)
    for b in range(T5_BUCKETS):
        out = jnp.where(bucket[None] == b, tab[:, b].reshape((-1,) + (1,) * bucket.ndim), out)
    return out


def _rms(x, g):
    return x * lax.rsqrt(jnp.mean(x * x, axis=-1, keepdims=True) + EPS) * g


def _split(z, widths):
    outs, s = [], 0
    for w in widths:
        outs.append(z[..., s:s + w])
        s += w
    return outs


def _prompt_tables(t5_bias, T):
    t = ATT_TILE
    i = jnp.arange(t)
    dist = jnp.arange(3)[:, None, None] * t + i[None, :, None] - i[None, None, :]
    bias_tiles = _bias_lookup(t5_bias.T, dist)
    pos = jnp.arange(T)
    lane = jnp.arange(LANE)
    e_moba = (pos[None, :] // MOBA_BLOCK == lane[:, None]).astype(BF16)
    e_slc = (pos[None, :] // SLC_BLOCK == lane[:, None]).astype(BF16)
    d_cmp = pos[:, None] - (lane[None, :] * CMP_STRIDE + CMP_LEN - 1)
    bias_cmp = _bias_lookup(t5_bias.T[MOBA_HEADS:], d_cmp)
    c_start = lane * CMP_STRIDE
    s_start = lane * SLC_BLOCK
    n_cmp = T // CMP_STRIDE - 1
    overlap = ((c_start[:, None] < s_start[None, :] + SLC_BLOCK) & (c_start[:, None] + CMP_LEN > s_start[None, :])
               & (lane[:, None] < n_cmp) & (lane[None, :] < T // SLC_BLOCK)).astype(BF16)
    return bias_tiles, e_moba, e_slc, bias_cmp, overlap


def _mixer_prompt(x, mod, mod3, lp, tabs, w_router_pad, b_router_pad):
    B, T, D = x.shape
    N = B * T
    bias_tiles, e_moba, e_slc, bias_cmp, overlap = tabs
    sh1, sc1 = mod[:, None, :D], mod[:, None, D:2 * D]
    h = (_rms(x, lp['norm_mix']) * (1.0 + sc1) + sh1).astype(BF16).reshape(N, D)
    za = _matmul_w32(h, lp['w_in'], lp['l'], 0, ZA_W, ZA_TILE, 1024).reshape(B, T, ZA_W)
    zb = _matmul_w32(h, lp['w_in'], lp['l'], COL_LX, ZB_W, ZB_TILE, 1024).reshape(B, T, ZB_W)
    new_kv = jnp.concatenate([za[..., MOBA_W:3 * MOBA_W], za[..., 3 * MOBA_W + NSA_W:COL_NG - WIN_DIM]], axis=-1)
    win = za[:, T - WINDOW:, COL_NG - WIN_DIM:COL_NG]

    sel_moba = _moba_select(za, B, T)
    o_a = _flash(za, bias_tiles, B=B, T=T, q_col=0, k_col=1, v_col=2, bias_blk=0, per_head_kv=True,
                 sel=sel_moba, emat=e_moba)
    o_cmp, sel_slc = _cmp_branch(za, lp['cmp_pos_k'], lp['cmp_pos_v'], lp['w_cmp_k'].reshape(-1, HEAD_DIM),
                                 lp['w_cmp_v'].reshape(-1, HEAD_DIM), bias_cmp, overlap, B, T)
    o_slc = _flash(za, bias_tiles, B=B, T=T, q_col=3, k_col=18, v_col=19, bias_blk=1, per_head_kv=False,
                   sel=sel_slc, emat=e_slc)
    o_win = _flash(za, bias_tiles, B=B, T=T, q_col=3, k_col=20, v_col=21, bias_blk=1, per_head_kv=False,
                   window_tiles=WINDOW // ATT_TILE)
    o_c, h_last, conv_last = _rglru(zb, jnp.zeros((B, CONV_W - 1, LRU_WIDTH), F32), jnp.zeros((B, LRU_WIDTH), F32),
                                    lp['conv_w'], lp['conv_b'], lp['w_rg'], lp['b_rg'], lp['w_ig'], lp['b_ig'],
                                    lp['lru_lambda'], B, T)
    flat = lambda a: a.reshape(N, a.shape[-1])
    merged = _merge(flat(o_a), flat(o_cmp), flat(o_slc), flat(o_win), flat(za), flat(o_c), flat(zb),
                    lp['w_br_moba'], lp['w_br_nsa'], lp['w_br_lru'], 256)
    x, h2, logits = _outproj(merged, lp['w_out'], x.reshape(N, D), mod3, lp['norm_ffn'].reshape(1, D),
                             w_router_pad, b_router_pad, 256, T, 0)
    return x, h2, logits, new_kv, win, h_last.reshape(B, LRU_WIDTH), conv_last


def _mixer_decode(l, x, mod, lp, cache_kv, cache_win, page_table, h0, conv0, dtabs, t5_bias, w_router_pad,
                  b_router_pad):
    B, D = x.shape
    pad = lambda a: jnp.pad(a, ((0, DEC_ROWS - B),) + ((0, 0),) * (a.ndim - 1))
    sh1, sc1 = mod[:, :D], mod[:, D:2 * D]
    h = pad((_rms(x, lp['norm_mix']) * (1.0 + sc1) + sh1).astype(BF16))
    za = _matmul_w32(h, lp['w_in'], l, 0, ZA_W, ZA_TILE, DEC_ROWS)
    zb = _matmul_w32(h, lp['w_in'], l, COL_LX, ZB_W, ZB_TILE, DEC_ROWS)
    new_kv = jnp.concatenate([za[:B, MOBA_W:3 * MOBA_W], za[:B, 3 * MOBA_W + NSA_W:COL_NG - WIN_DIM]], axis=-1)
    n_win = cache_win.shape[2]
    win = jnp.concatenate([cache_win[l], za[:B, None, COL_NG - WIN_DIM:COL_NG]], axis=1)
    win = win[:, win.shape[1] - min(WINDOW, n_win + 1):]
    cmp_w = (lp['cmp_pos_k'], lp['cmp_pos_v'], lp['w_cmp_k'], lp['w_cmp_v'])
    o_a, o_cmp, o_slc, o_win = _decode_attention(l, za, cache_kv, cache_win, page_table, cmp_w, dtabs, t5_bias)
    o_c, h_new, conv_t = _lru_step(zb, conv0.transpose(1, 0, 2), h0, lp['conv_w'], lp['conv_b'], lp['w_rg'],
                                   lp['b_rg'], lp['w_ig'], lp['b_ig'], lp['lru_lambda'])
    merged = _merge(o_a, o_cmp, o_slc, o_win, za, pad(o_c), zb, lp['w_br_moba'], lp['w_br_nsa'], lp['w_br_lru'],
                    DEC_ROWS)
    mod3 = pad(mod).reshape(DEC_ROWS, 6, D).transpose(1, 0, 2)
    x, h2, logits = _outproj(merged, lp['w_out'], pad(x), mod3, lp['norm_ffn'].reshape(1, D), w_router_pad,
                             b_router_pad, DEC_ROWS, DEC_ROWS, 0)
    return x[:B], h2, logits, new_kv, win, h_new, conv_t.transpose(1, 0, 2)


SCAN_PAGES = 32
DEC_ROWS = 16


def _page_scan_kernel(pt_ref, *refs):
    del pt_ref
    ck_refs, cv_refs, mk_refs = refs[:SCAN_PAGES], refs[SCAN_PAGES:2 * SCAN_PAGES], refs[2 * SCAN_PAGES:3 * SCAN_PAGES]
    pek_ref, pev_ref, wk_ref, wv_ref, abk_ref, abv_ref, km_ref = refs[3 * SCAN_PAGES:]
    chunks = PAGE_SIZE // CMP_STRIDE
    for src, pe, w, dst in ((ck_refs, pek_ref, wk_ref, abk_ref), (cv_refs, pev_ref, wv_ref, abv_ref)):
        first = jnp.zeros((SCAN_PAGES * chunks, HEAD_DIM), F32)
        second = jnp.zeros((SCAN_PAGES * chunks, HEAD_DIM), F32)
        for l in range(CMP_STRIDE):
            x = jnp.concatenate([r[0, 0, pl.ds(l, chunks, stride=CMP_STRIDE), :] for r in src], axis=0)
            l2 = CMP_STRIDE + l
            first = first + jnp.dot((x + pe[l:l + 1, :]).astype(BF16), w[l], preferred_element_type=F32)
            second = second + jnp.dot((x + pe[l2:l2 + 1, :]).astype(BF16), w[l2], preferred_element_type=F32)
        dst[0, :, 0:HEAD_DIM] = first
        dst[0, :, HEAD_DIM:2 * HEAD_DIM] = second
    sums = [jnp.sum(r[0, 0], axis=0, keepdims=True) for r in mk_refs]
    per_blk = MOBA_BLOCK // PAGE_SIZE
    means = [functools.reduce(lambda a, b: a + b, sums[j * per_blk:(j + 1) * per_blk])
             for j in range(SCAN_PAGES // per_blk)]
    km_ref[0] = jnp.concatenate(means, axis=0) * (1.0 / MOBA_BLOCK)


def _page_scan(cache_kv, l, page_table, pe_k, pe_v, w_k, w_v):
    B, n_pages = page_table.shape
    steps = n_pages // SCAN_PAGES
    chunks = PAGE_SIZE // CMP_STRIDE
    blocks = SCAN_PAGES * PAGE_SIZE // MOBA_BLOCK
    cmp_col = 2 * MOBA_W // HEAD_DIM

    def cmp_spec(j, col):
        return pl.BlockSpec((1, 1, PAGE_SIZE, HEAD_DIM), lambda b, s, pt: (l, pt[b, s * SCAN_PAGES + j], 0, col))

    def mk_spec(j):
        return pl.BlockSpec((1, 1, PAGE_SIZE, MOBA_W), lambda b, s, pt: (l, pt[b, s * SCAN_PAGES + j], 0, 0))

    const = lambda a: pl.BlockSpec(a.shape, lambda b, s, pt: (0,) * a.ndim)
    grid_spec = pltpu.PrefetchScalarGridSpec(
        num_scalar_prefetch=1,
        grid=(B, steps),
        in_specs=([cmp_spec(j, cmp_col) for j in range(SCAN_PAGES)]
                  + [cmp_spec(j, cmp_col + 1) for j in range(SCAN_PAGES)] + [mk_spec(j) for j in range(SCAN_PAGES)]
                  + [const(pe_k), const(pe_v), const(w_k), const(w_v)]),
        out_specs=[pl.BlockSpec((1, SCAN_PAGES * chunks, 2 * HEAD_DIM), lambda b, s, pt: (b, s, 0)),
                   pl.BlockSpec((1, SCAN_PAGES * chunks, 2 * HEAD_DIM), lambda b, s, pt: (b, s, 0)),
                   pl.BlockSpec((1, blocks, MOBA_W), lambda b, s, pt: (b, s, 0))],
    )
    n_ch = n_pages * chunks
    return pl.pallas_call(
        _page_scan_kernel,
        grid_spec=grid_spec,
        out_shape=[jax.ShapeDtypeStruct((B, n_ch, 2 * HEAD_DIM), F32),
                   jax.ShapeDtypeStruct((B, n_ch, 2 * HEAD_DIM), F32),
                   jax.ShapeDtypeStruct((B, steps * blocks, MOBA_W), F32)],
        compiler_params=_cparams(("parallel", "arbitrary")),
        name="page_scan",
    )(page_table, *([cache_kv] * (3 * SCAN_PAGES)), pe_k, pe_v, w_k, w_v)


def _topk_lanes(score, k):
    lane = lax.broadcasted_iota(jnp.int32, score.shape, 1)
    out = jnp.zeros(score.shape, jnp.int32)
    for r in range(k):
        m = jnp.max(score, axis=1, keepdims=True)
        idx = jnp.min(jnp.where(score == m, lane, score.shape[1]), axis=1, keepdims=True)
        out = jnp.where(lane == r, idx, out)
        score = jnp.where(lane == idx, -jnp.inf, score)
    return out


def _head_rows(q):
    rows = [q[:, g * HEAD_DIM:(g + 1) * HEAD_DIM] for g in range(NSA_HEADS)]
    return jnp.concatenate(rows + [jnp.zeros((8 - NSA_HEADS, HEAD_DIM), q.dtype)], axis=0)


def _dec_select_kernel(abk_ref, abv_ref, km_ref, mq_ref, nq_ref, bias_ref, ovl_ref, ocmp_ref, tm_ref, ts_ref,
                       *, past):
    b = pl.program_id(0)
    n_ch = abk_ref.shape[1]
    nb = km_ref.shape[1]
    row = lax.broadcasted_iota(jnp.int32, (8, LANE), 0)
    mq = mq_ref[pl.ds(b, 1), :]
    sc = jnp.full((8, LANE), NEG, F32)
    for h in range(MOBA_HEADS):
        hs = slice(h * HEAD_DIM, (h + 1) * HEAD_DIM)
        qh = jnp.broadcast_to(mq[:, hs], (8, HEAD_DIM)).astype(BF16)
        s = lax.dot_general(qh, km_ref[0, :, hs].astype(BF16), (((1,), (1,)), ((), ())), preferred_element_type=F32)
        s = jnp.concatenate([s, jnp.full((8, LANE - nb), NEG, F32)], axis=1)
        sc = jnp.where(row == h, s, sc)
    tm_ref[0] = _topk_lanes(sc, MOBA_TOPK)
    abk = abk_ref[0]
    abv = abv_ref[0]
    kc = (abk[:, :HEAD_DIM] + pltpu.roll(abk[:, HEAD_DIM:], n_ch - 1, 0)).astype(BF16)
    vc = (abv[:, :HEAD_DIM] + pltpu.roll(abv[:, HEAD_DIM:], n_ch - 1, 0)).astype(BF16)
    q4 = _head_rows(nq_ref[pl.ds(b, 1), :]).astype(BF16)
    s = lax.dot_general(q4, kc, (((1,), (1,)), ((), ())), preferred_element_type=F32) * SCALE + bias_ref[...]
    n = lax.broadcasted_iota(jnp.int32, (8, n_ch), 1)
    vis = (n < n_ch - 1) & (n * CMP_STRIDE + (CMP_LEN - 1) <= past)
    s = jnp.where(vis, s, NEG)
    m = jnp.max(s, axis=1, keepdims=True)
    e = jnp.where(vis, jnp.exp(s - m), 0.0)
    den = jnp.sum(e, axis=1, keepdims=True)
    pb = (e / jnp.where(den > 0.0, den, 1.0)).astype(BF16)
    ocmp_ref[0] = jnp.dot(pb, vc, preferred_element_type=F32)
    imp = jnp.dot(pb, ovl_ref[...], preferred_element_type=F32)
    rows = lax.broadcasted_iota(jnp.int32, imp.shape, 0)
    imp = jnp.sum(jnp.where(rows < NSA_HEADS, imp, 0.0), axis=0, keepdims=True)
    ts_ref[0] = _topk_lanes(jnp.broadcast_to(imp, (8, imp.shape[1])), SLC_TOPK - 1)[:, :LANE]


def _dec_select(abk, abv, kmean, za_s, bias_cmp, overlap, past):
    B, n_ch, _ = abk.shape
    full = lambda a: pl.BlockSpec(a.shape, lambda b: (0,) * a.ndim)
    per_b = lambda a: pl.BlockSpec((1,) + a.shape[1:], lambda b: (b, 0, 0))
    out = jax.ShapeDtypeStruct((B, 8, LANE), F32)
    outi = jax.ShapeDtypeStruct((B, 8, LANE), jnp.int32)
    return pl.pallas_call(
        functools.partial(_dec_select_kernel, past=past),
        grid=(B,),
        in_specs=[per_b(abk), per_b(abv), per_b(kmean),
                  pl.BlockSpec((DEC_ROWS, MOBA_W), lambda b: (0, 0)),
                  pl.BlockSpec((DEC_ROWS, NSA_W), lambda b: (0, 3)),
                  full(bias_cmp), full(overlap)],
        out_specs=[pl.BlockSpec((1, 8, LANE), lambda b: (b, 0, 0))] * 3,
        out_shape=[out, outi, outi],
        compiler_params=_cparams(("parallel",)),
        name="dec_select",
    )(abk, abv, kmean, za_s, za_s, bias_cmp, overlap)


def _softmax_parts(scores):
    m = functools.reduce(jnp.maximum, [jnp.max(s, axis=1, keepdims=True) for s in scores])
    es = [jnp.exp(s - m) for s in scores]
    den = functools.reduce(lambda a, b: a + b, [jnp.sum(e, axis=1, keepdims=True) for e in es])
    return [e / den for e in es]


def _dec_attend_kernel(pm_ref, ps_ref, hs_ref, *refs, n_moba, n_slc, n_win):
    del pm_ref, ps_ref, hs_ref
    mk, mv = refs[:n_moba], refs[n_moba:2 * n_moba]
    sk, sv = refs[2 * n_moba:2 * n_moba + n_slc], refs[2 * n_moba + n_slc:2 * (n_moba + n_slc)]
    (kw_ref, vw_ref, mq_ref, mkn_ref, mvn_ref, nq_ref, skn_ref, svn_ref, wkn_ref, wvn_ref, bm_ref, bs_ref, bw_ref,
     oa_ref, oslc_ref, owin_ref) = refs[2 * (n_moba + n_slc):]
    b = pl.program_id(0)
    h = pl.program_id(1)
    nt = (((1,), (1,)), ((), ()))
    rnd = lambda a: a.astype(BF16).astype(F32)

    q = mq_ref[pl.ds(b, 1), :]
    q8 = jnp.broadcast_to(q, (8, HEAD_DIM)).astype(BF16)
    bm = bm_ref[0]
    scores = [lax.dot_general(q8, mk[j][0, 0].astype(BF16), nt, preferred_element_type=F32)[0:1] * SCALE
              + bm[j:j + 1] for j in range(n_moba)]
    s_new = jnp.sum(rnd(q) * rnd(mkn_ref[pl.ds(b, 1), :]), axis=1, keepdims=True) * SCALE + bm[n_moba:n_moba + 1, 0:1]
    probs = _softmax_parts(scores + [s_new])
    o = rnd(probs[-1]) * rnd(mvn_ref[pl.ds(b, 1), :])
    for j in range(n_moba):
        pj = jnp.broadcast_to(probs[j], (8, PAGE_SIZE)).astype(BF16)
        o = o + jnp.dot(pj, mv[j][0, 0].astype(BF16), preferred_element_type=F32)[0:1]
    oa_ref[0] = jnp.broadcast_to(o, (8, HEAD_DIM))

    @pl.when(h == 0)
    def _():
        q4f = _head_rows(nq_ref[pl.ds(b, 1), :])
        q4 = q4f.astype(BF16)
        scores = [lax.dot_general(q4, sk[j][0, 0].astype(BF16), nt, preferred_element_type=F32) * SCALE + bs_ref[0, j]
                  for j in range(n_slc)]
        s_new = (jnp.sum(rnd(q4f) * rnd(skn_ref[pl.ds(b, 1), :]), axis=1, keepdims=True) * SCALE
                 + bs_ref[0, n_slc][:, 0:1])
        probs = _softmax_parts(scores + [s_new])
        o = rnd(probs[-1]) * rnd(svn_ref[pl.ds(b, 1), :])
        for j in range(n_slc):
            o = o + jnp.dot(probs[j].astype(BF16), sv[j][0, 0].astype(BF16), preferred_element_type=F32)
        oslc_ref[0] = o
        s = lax.dot_general(q4, kw_ref[0, 0].astype(BF16), nt, preferred_element_type=F32) * SCALE + bw_ref[:, :n_win]
        i = lax.broadcasted_iota(jnp.int32, (8, n_win), 1)
        s = jnp.where(n_win - i < WINDOW, s, NEG)
        s_new = (jnp.sum(rnd(q4f) * rnd(wkn_ref[pl.ds(b, 1), :]), axis=1, keepdims=True) * SCALE
                 + bw_ref[:, n_win:n_win + 1])
        p_win, p_new = _softmax_parts([s, s_new])
        p_win = jnp.where(n_win - i < WINDOW, p_win, 0.0)
        owin_ref[0] = (rnd(p_new) * rnd(wvn_ref[pl.ds(b, 1), :])
                       + jnp.dot(p_win.astype(BF16), vw_ref[0, 0].astype(BF16), preferred_element_type=F32))


def _dec_attend(cache_kv, cache_win, l, za_s, pages_moba, pages_slc, halves_slc, bias_moba, bias_slc, bias_win, B):
    n_moba = MOBA_TOPK * MOBA_BLOCK // PAGE_SIZE
    n_slc = SLC_TOPK - 1
    n_win = cache_win.shape[2]

    def moba_spec(j, col0):
        return pl.BlockSpec((1, 1, PAGE_SIZE, HEAD_DIM),
                            lambda b, h, pm, ps, hs: (l, pm[(b * MOBA_HEADS + h) * n_moba + j], 0, col0 + h))

    def slc_spec(j, col):
        return pl.BlockSpec((1, 1, SLC_BLOCK, HEAD_DIM),
                            lambda b, h, pm, ps, hs: (l, ps[b * n_slc + j], hs[b * n_slc + j], col))

    zcol = lambda w, c: pl.BlockSpec((DEC_ROWS, w), lambda b, h, pm, ps, hs: (0, c))
    zhead = lambda c0: pl.BlockSpec((DEC_ROWS, HEAD_DIM), lambda b, h, pm, ps, hs: (0, c0 + h))
    slc_col = (2 * MOBA_W + 2 * HEAD_DIM) // HEAD_DIM
    in_specs = ([moba_spec(j, 0) for j in range(n_moba)] + [moba_spec(j, MOBA_HEADS) for j in range(n_moba)]
                + [slc_spec(j, slc_col) for j in range(n_slc)] + [slc_spec(j, slc_col + 1) for j in range(n_slc)]
                + [pl.BlockSpec((1, 1, n_win, HEAD_DIM), lambda b, h, pm, ps, hs: (l, b, 0, 0)),
                   pl.BlockSpec((1, 1, n_win, HEAD_DIM), lambda b, h, pm, ps, hs: (l, b, 0, 1)),
                   zhead(0), zhead(MOBA_HEADS), zhead(2 * MOBA_HEADS), zcol(NSA_W, 3),
                   zcol(HEAD_DIM, 18), zcol(HEAD_DIM, 19), zcol(HEAD_DIM, 20), zcol(HEAD_DIM, 21),
                   pl.BlockSpec((1, 8, LANE), lambda b, h, pm, ps, hs: (b * MOBA_HEADS + h, 0, 0)),
                   pl.BlockSpec((1, n_slc + 1, 8, SLC_BLOCK), lambda b, h, pm, ps, hs: (b, 0, 0, 0)),
                   pl.BlockSpec(bias_win.shape, lambda b, h, pm, ps, hs: (0, 0))])
    grid_spec = pltpu.PrefetchScalarGridSpec(
        num_scalar_prefetch=3,
        grid=(B, MOBA_HEADS),
        in_specs=in_specs,
        out_specs=[pl.BlockSpec((1, 8, HEAD_DIM), lambda b, h, pm, ps, hs: (b * MOBA_HEADS + h, 0, 0)),
                   pl.BlockSpec((1, 8, HEAD_DIM), lambda b, h, pm, ps, hs: (b, 0, 0)),
                   pl.BlockSpec((1, 8, HEAD_DIM), lambda b, h, pm, ps, hs: (b, 0, 0))],
    )
    return pl.pallas_call(
        functools.partial(_dec_attend_kernel, n_moba=n_moba, n_slc=n_slc, n_win=n_win),
        grid_spec=grid_spec,
        out_shape=[jax.ShapeDtypeStruct((B * MOBA_HEADS, 8, HEAD_DIM), F32),
                   jax.ShapeDtypeStruct((B, 8, HEAD_DIM), F32), jax.ShapeDtypeStruct((B, 8, HEAD_DIM), F32)],
        compiler_params=_cparams(("parallel", "arbitrary")),
        name="dec_attend",
    )(pages_moba, pages_slc, halves_slc, *([cache_kv] * (2 * n_moba + 2 * n_slc)), cache_win, cache_win,
      za_s, za_s, za_s, za_s, za_s, za_s, za_s, za_s, bias_moba, bias_slc, bias_win)


def _lru_step_kernel(x_ref, g_ref, c0_ref, h0_ref, cw_ref, cb_ref, wr_ref, br_ref, wi_ref, bi_ref, lam_ref,
                     y_ref, h_ref, cl_ref):
    B = h0_ref.shape[0]
    x = x_ref[0:B, :]
    u = cb_ref[...] + c0_ref[0] * cw_ref[0:1, :]
    u = u + c0_ref[1] * cw_ref[1:2, :]
    u = u + c0_ref[2] * cw_ref[2:3, :]
    u = u + x * cw_ref[3:4, :]
    ub = u.astype(BF16)
    r = jax.nn.sigmoid(jnp.dot(ub, wr_ref[0], preferred_element_type=F32) + br_ref[...])
    ig = jax.nn.sigmoid(jnp.dot(ub, wi_ref[0], preferred_element_type=F32) + bi_ref[...])
    log_a = (-LRU_C * r) * _softplus(-lam_ref[...])
    h = jnp.exp(log_a) * h0_ref[...] + jnp.sqrt(1.0 - jnp.exp(2.0 * log_a)) * (ig * u)
    h_ref[...] = h
    y_ref[...] = h * _gelu_tanh(g_ref[0:B, :])
    cl_ref[0] = c0_ref[1]
    cl_ref[1] = c0_ref[2]
    cl_ref[2] = x


def _lru_step(zb_s, conv0_t, h0, conv_w, conv_b, w_r, b_r, w_i, b_i, lam):
    B = h0.shape[0]
    nblk = LRU_BLOCKS
    vec = lambda a: a.reshape(1, LRU_WIDTH)
    row_spec = pl.BlockSpec((1, LRU_BLOCK), lambda n: (0, n))
    bw = lambda: pl.BlockSpec((B, LRU_BLOCK), lambda n: (0, n))
    return pl.pallas_call(
        _lru_step_kernel,
        grid=(nblk,),
        in_specs=[pl.BlockSpec((DEC_ROWS, LRU_BLOCK), lambda n: (0, n)),
                  pl.BlockSpec((DEC_ROWS, LRU_BLOCK), lambda n: (0, nblk + n)),
                  pl.BlockSpec((CONV_W - 1, B, LRU_BLOCK), lambda n: (0, 0, n)), bw(),
                  pl.BlockSpec((CONV_W, LRU_BLOCK), lambda n: (0, n)), row_spec,
                  pl.BlockSpec((1, LRU_BLOCK, LRU_BLOCK), lambda n: (n, 0, 0)), row_spec,
                  pl.BlockSpec((1, LRU_BLOCK, LRU_BLOCK), lambda n: (n, 0, 0)), row_spec, row_spec],
        out_specs=[bw(), bw(), pl.BlockSpec((CONV_W - 1, B, LRU_BLOCK), lambda n: (0, 0, n))],
        out_shape=[jax.ShapeDtypeStruct((B, LRU_WIDTH), F32), jax.ShapeDtypeStruct((B, LRU_WIDTH), F32),
                   jax.ShapeDtypeStruct((CONV_W - 1, B, LRU_WIDTH), F32)],
        compiler_params=_cparams(("parallel",)),
        name="lru_step",
    )(zb_s, zb_s, conv0_t, h0, conv_w, vec(conv_b), w_r, vec(b_r), w_i, vec(b_i), vec(lam))


def _bias_heads(tab, dist):
    bucket = _t5_bucket(dist)
    out = jnp.zeros(bucket.shape, F32)
    shape = (1, -1) + (1,) * (bucket.ndim - 2)
    for bkt in range(T5_BUCKETS):
        out = jnp.where(bucket == bkt, tab[:, bkt].reshape(shape), out)
    return out


def _decode_tables(t5_bias, past, n_win):
    tab_n = t5_bias.T[MOBA_HEADS:]
    n_ch = past // CMP_STRIDE
    n = jnp.arange(n_ch)
    bias_cmp = jnp.pad(_bias_lookup(tab_n, past - (n * CMP_STRIDE + CMP_LEN - 1)), ((0, 8 - NSA_HEADS), (0, 0)))
    s_start = jnp.arange(past // SLC_BLOCK) * SLC_BLOCK
    c_start = n * CMP_STRIDE
    overlap = ((c_start[:, None] < s_start[None, :] + SLC_BLOCK) & (c_start[:, None] + CMP_LEN > s_start[None, :])
               & (n[:, None] < n_ch - 1)).astype(BF16)
    d_win = jnp.concatenate([n_win - jnp.arange(n_win), jnp.zeros((LANE,), jnp.int32)])
    bias_win = jnp.pad(_bias_lookup(tab_n, d_win), ((0, 8 - NSA_HEADS), (0, 0)))
    return bias_cmp, overlap, bias_win


def _decode_attention(l, za_s, cache_kv, cache_win, page_table, lp_cmp, dtabs, t5_bias):
    B, n_pages = page_table.shape
    past = n_pages * PAGE_SIZE
    bias_cmp, overlap, bias_win = dtabs
    pe_k, pe_v, w_k, w_v = lp_cmp
    abk, abv, kmean = _page_scan(cache_kv, l, page_table, pe_k, pe_v, w_k, w_v)
    o_cmp, top_m, top_s = _dec_select(abk, abv, kmean, za_s, bias_cmp, overlap, past)
    top_m = top_m[:, :MOBA_HEADS, :MOBA_TOPK]
    top_s = top_s[:, 0, :SLC_TOPK - 1]
    tab = t5_bias.T
    bidx = jnp.arange(B)
    per_blk = MOBA_BLOCK // PAGE_SIZE
    pg_off = top_m[..., None] * per_blk + jnp.arange(per_blk)
    pages_moba = page_table[bidx[:, None, None, None], pg_off].reshape(-1)
    kpos = (pg_off.reshape(B, MOBA_HEADS, -1, 1) * PAGE_SIZE + jnp.arange(PAGE_SIZE))
    bm = _bias_heads(tab[:MOBA_HEADS], past - kpos)
    bm_new = jnp.zeros((B, MOBA_HEADS, 1, LANE), F32).at[..., 0].set(tab[None, :MOBA_HEADS, 0, None])
    bias_moba = jnp.concatenate([bm, bm_new, jnp.zeros((B, MOBA_HEADS, 1, LANE), F32)], axis=2)
    bias_moba = bias_moba.reshape(B * MOBA_HEADS, 8, LANE)
    per_page = PAGE_SIZE // SLC_BLOCK
    pages_slc = page_table[bidx[:, None], top_s // per_page].reshape(-1)
    halves_slc = (top_s % per_page).reshape(-1)
    spos = top_s[:, None, :, None] * SLC_BLOCK + jnp.arange(SLC_BLOCK)
    bs = _bias_heads(tab[MOBA_HEADS:], jnp.broadcast_to(past - spos, (B, NSA_HEADS, SLC_TOPK - 1, SLC_BLOCK)))
    bs_new = jnp.zeros((B, NSA_HEADS, 1, SLC_BLOCK), F32).at[..., 0].set(tab[None, MOBA_HEADS:, 0, None])
    bias_slc = jnp.concatenate([bs, bs_new], axis=2).transpose(0, 2, 1, 3)
    bias_slc = jnp.pad(bias_slc, ((0, 0), (0, 0), (0, 8 - NSA_HEADS), (0, 0)))
    o_a, o_slc, o_win = _dec_attend(cache_kv, cache_win, l, za_s, pages_moba, pages_slc, halves_slc,
                                    bias_moba, bias_slc, bias_win, B)
    pad = lambda a: jnp.pad(a, ((0, DEC_ROWS - B), (0, 0)))
    heads = lambda a: pad(a[:, :NSA_HEADS].reshape(B, NSA_W))
    return pad(o_a[:, 0].reshape(B, MOBA_W)), heads(o_cmp), heads(o_slc), heads(o_win)


def kernel(x_prompt, x_sample, cache_kv, cache_win, state_lru_h, state_lru_conv, page_table, c_prompt, c_sample,
           w_ada, b_ada, norm_mix, norm_ffn, w_in, cmp_pos_k, cmp_pos_v, w_cmp_k, w_cmp_v, conv_w, conv_b,
           w_rg, b_rg, w_ig, b_ig, lru_lambda, w_br_moba, w_br_nsa, w_br_lru, w_out, w_e_gate, w_e_up, w_e_down,
           t5_bias, w_router, b_router, norm_final):
    n_p, T, D = x_prompt.shape
    n_s = x_sample.shape[0]
    xp = x_prompt
    xs = x_sample.reshape(n_s, D)
    tabs = _prompt_tables(t5_bias, T)
    w_router_pad = jnp.pad(w_router, ((0, 0), (0, LANE - N_EXPERTS))).astype(BF16)
    b_router_pad = jnp.pad(b_router, (0, LANE - N_EXPERTS)).reshape(1, LANE)
    c_all = jax.nn.silu(jnp.concatenate([c_prompt, c_sample], axis=0))
    c_all = jnp.pad(c_all, ((0, 16 - n_p - n_s), (0, 0)))
    dtabs = _decode_tables(t5_bias, page_table.shape[1] * PAGE_SIZE, cache_win.shape[2])
    moe_rows = LANE
    outs = [[] for _ in range(8)]
    for l in range(DEPTH):
        lp = {'l': l, 'norm_mix': norm_mix[l], 'norm_ffn': norm_ffn[l], 'w_in': w_in,
              'cmp_pos_k': cmp_pos_k[l], 'cmp_pos_v': cmp_pos_v[l],
              'w_cmp_k': w_cmp_k[l].astype(BF16), 'w_cmp_v': w_cmp_v[l].astype(BF16),
              'conv_w': conv_w[l], 'conv_b': conv_b[l], 'w_rg': w_rg[l].astype(BF16), 'b_rg': b_rg[l],
              'w_ig': w_ig[l].astype(BF16), 'b_ig': b_ig[l], 'lru_lambda': lru_lambda[l],
              'w_br_moba': w_br_moba[l].astype(BF16), 'w_br_nsa': w_br_nsa[l].astype(BF16),
              'w_br_lru': w_br_lru[l].astype(BF16), 'w_out': w_out[l].astype(BF16)}
        mod = _matmul(c_all, w_ada[l], 16, 2048) + b_ada[l]
        mod3 = mod.reshape(16 * 6, 1, D)
        mod_s = mod[n_p:n_p + n_s]
        xp, h2_p, lg_p, kv_p, win_p, h_p, conv_p = _mixer_prompt(xp, mod[:n_p], mod3, lp, tabs, w_router_pad,
                                                                 b_router_pad)
        xs, h2_s, lg_s, kv_s, win_s, h_s, conv_s = _mixer_decode(l, xs, mod_s, lp, cache_kv, cache_win, page_table,
                                                                 state_lru_h[l], state_lru_conv[l], dtabs, t5_bias,
                                                                 w_router_pad, b_router_pad)
        n_tok = n_p * T
        tail = ((0, moe_rows - h2_s.shape[0]), (0, 0))
        h2 = jnp.concatenate([h2_p, jnp.pad(h2_s, tail)], axis=0)
        logits = jnp.concatenate([lg_p, jnp.pad(lg_s, tail)], axis=0)[:, :N_EXPERTS]
        moe = _moe_prompt(h2, logits, l, w_e_gate, w_e_up, w_e_down)
        xp = xp.reshape(n_p, T, D) + mod[:n_p, None, 5 * D:] * moe[:n_tok].reshape(n_p, T, D)
        xs = xs + mod_s[:, 5 * D:] * moe[n_tok:n_tok + n_s]
        for lst, val in zip(outs, (kv_p, kv_s[:, None, :], win_p, win_s, h_p, h_s, conv_p, conv_s)):
            lst.append(val)
    y_prompt = _rms(xp, norm_final)
    y_sample = _rms(xs, norm_final).reshape(n_s, 1, D)
    return (y_prompt, y_sample) + tuple(jnp.stack(o) for o in outs)
```

```python
import functools
import math

import jax
import jax.numpy as jnp
from jax import lax
from jax.experimental import pallas as pl
from jax.experimental.pallas import tpu as pltpu

D_MODEL = 2048
DEPTH = 2
PAGE_SIZE = 128
HEAD_DIM = 128
MOBA_HEADS = 4
MOBA_BLOCK = 256
MOBA_TOPK = 3
NSA_HEADS = 4
CMP_STRIDE = 16
CMP_LEN = 32
SLC_BLOCK = 64
SLC_TOPK = 16
WINDOW = 512
LRU_WIDTH = D_MODEL // 2
LRU_BLOCKS = 8
LRU_BLOCK = LRU_WIDTH // LRU_BLOCKS
LRU_C = 8.0
CONV_W = 4
N_EXPERTS = 16
N_GROUPS = 4
EXPERTS_PER_GROUP = N_EXPERTS // N_GROUPS
TOP_K = 2
D_EXPERT = D_MODEL // 2
T5_BUCKETS = 32
T5_MAX_DIST = 128
EPS = 1e-6
NEG = -1e30

MOBA_W = MOBA_HEADS * HEAD_DIM
NSA_W = NSA_HEADS * HEAD_DIM
KV_DIM = 2 * MOBA_W + 4 * HEAD_DIM
WIN_DIM = 2 * HEAD_DIM
IN_WIDTHS = (MOBA_W, MOBA_W, MOBA_W, NSA_W, HEAD_DIM, HEAD_DIM, HEAD_DIM, HEAD_DIM, HEAD_DIM, HEAD_DIM,
             3 * NSA_HEADS, LRU_WIDTH, LRU_WIDTH, D_MODEL, D_MODEL, D_MODEL)
COL_NG = 3 * MOBA_W + NSA_W + 6 * HEAD_DIM
COL_LX = COL_NG + 3 * NSA_HEADS
ZA_TILE = 1024
ZA_W = -(-(COL_NG + 3 * NSA_HEADS) // ZA_TILE) * ZA_TILE
ZB_TILE = 1024
ZB_W = 2 * LRU_WIDTH + 3 * D_MODEL
SCALE = HEAD_DIM ** -0.5

LANE = 128
ATT_TILE = 512
MOE_TILE = 256
VMEM_LIMIT = 56 * 1024 * 1024

F32 = jnp.float32
BF16 = jnp.bfloat16


def _cparams(sem):
    return pltpu.CompilerParams(dimension_semantics=sem, vmem_limit_bytes=VMEM_LIMIT)


def _mm_kernel(x_ref, w_ref, o_ref):
    o_ref[...] = jnp.dot(x_ref[...].astype(BF16), w_ref[...].astype(BF16), preferred_element_type=F32)


def _matmul(x, w, tm, tn):
    M, K = x.shape
    N = w.shape[1]
    assert M % tm == 0 and N % tn == 0
    return pl.pallas_call(
        _mm_kernel,
        grid=(N // tn, M // tm),
        in_specs=[pl.BlockSpec((tm, K), lambda j, i: (i, 0)),
                  pl.BlockSpec((K, tn), lambda j, i: (0, j))],
        out_specs=pl.BlockSpec((tm, tn), lambda j, i: (i, j)),
        out_shape=jax.ShapeDtypeStruct((M, N), F32),
        compiler_params=_cparams(("parallel", "parallel")),
        name="matmul",
    )(x, w)


def _mm_w32_kernel(x_ref, w_ref, *rest, shift):
    if shift:
        wn_ref, o_ref, wb_scr = rest
    else:
        o_ref, wb_scr = rest

    @pl.when(pl.program_id(1) == 0)
    def _():
        if not shift:
            wb_scr[...] = w_ref[0].astype(BF16)
            return
        nblk = w_ref.shape[2] // LANE
        q, r = divmod(shift, LANE)

        def window_block(i):
            ref = w_ref if i < nblk else wn_ref
            i = i % nblk
            return ref[0, :, i * LANE:(i + 1) * LANE]

        for c in range(nblk):
            blk = window_block(q + c)
            if r:
                pair = jnp.concatenate([blk, window_block(q + c + 1)], axis=1)
                blk = pltpu.roll(pair, 2 * LANE - r, 1)[:, :LANE]
            wb_scr[:, c * LANE:(c + 1) * LANE] = blk.astype(BF16)

    o_ref[...] = jnp.dot(x_ref[...], wb_scr[...], preferred_element_type=F32)


def _matmul_w32(x, w3, l, col0, n_out, tn, tm):
    M, K = x.shape
    shift = col0 % tn
    blk0 = col0 // tn
    assert n_out % tn == 0 and M % tm == 0
    wspec = lambda off: pl.BlockSpec((1, K, tn), lambda j, i: (l, 0, blk0 + j + off), pipeline_mode=pl.Buffered(1))
    in_specs = [pl.BlockSpec((tm, K), lambda j, i: (i, 0)), wspec(0)]
    args = [x, w3]
    if shift:
        in_specs.append(wspec(1))
        args.append(w3)
    return pl.pallas_call(
        functools.partial(_mm_w32_kernel, shift=shift),
        grid=(n_out // tn, M // tm),
        in_specs=in_specs,
        out_specs=pl.BlockSpec((tm, tn), lambda j, i: (i, j)),
        out_shape=jax.ShapeDtypeStruct((M, n_out), F32),
        scratch_shapes=[pltpu.VMEM((K, tn), BF16)],
        compiler_params=_cparams(("parallel", "arbitrary")),
        name="matmul_w32",
    )(*args)


def _merge_kernel(oa_ref, ocmp_ref, oslc_ref, owin_ref, ng_ref, oc_ref, ga_ref, gb_ref, gc_ref,
                  wa_ref, wb_ref, wc_ref, o_ref):
    gt = jax.nn.sigmoid(ng_ref[...])
    heads = []
    for g in range(NSA_HEADS):
        sl = slice(g * HEAD_DIM, (g + 1) * HEAD_DIM)
        heads.append(gt[:, 3 * g:3 * g + 1] * ocmp_ref[:, sl] + gt[:, 3 * g + 1:3 * g + 2] * oslc_ref[:, sl]
                     + gt[:, 3 * g + 2:3 * g + 3] * owin_ref[:, sl])
    ob = jnp.concatenate(heads, axis=1).astype(BF16)
    pa = jnp.dot(oa_ref[...].astype(BF16), wa_ref[...], preferred_element_type=F32)
    pb = jnp.dot(ob, wb_ref[...], preferred_element_type=F32)
    pc = jnp.dot(oc_ref[...].astype(BF16), wc_ref[...], preferred_element_type=F32)
    merged = (jax.nn.sigmoid(ga_ref[...]) * pa + jax.nn.sigmoid(gb_ref[...]) * pb
              + jax.nn.sigmoid(gc_ref[...]) * pc)
    o_ref[...] = merged.astype(BF16)


def _merge(o_a, o_cmp, o_slc, o_win, za, o_c, zb, w_a, w_b, w_c, tm):
    N = o_a.shape[0]
    D = D_MODEL
    row = lambda w, c: pl.BlockSpec((tm, w), lambda i: (i, c))
    full = lambda a: pl.BlockSpec(a.shape, lambda i: (0, 0))
    return pl.pallas_call(
        _merge_kernel,
        grid=(N // tm,),
        in_specs=[row(MOBA_W, 0), row(NSA_W, 0), row(NSA_W, 0), row(NSA_W, 0), row(LANE, COL_NG // LANE),
                  row(LRU_WIDTH, 0), row(D, 1), row(D, 2), row(D, 3), full(w_a), full(w_b), full(w_c)],
        out_specs=pl.BlockSpec((tm, D), lambda i: (i, 0)),
        out_shape=jax.ShapeDtypeStruct((N, D), BF16),
        compiler_params=_cparams(("parallel",)),
        name="merge",
    )(o_a, o_cmp, o_slc, o_win, za, o_c, zb, zb, zb, w_a, w_b, w_c)


def _outproj_kernel(m_ref, w_ref, x_ref, g1_ref, sc2_ref, sh2_ref, nf_ref, wr_ref, br_ref,
                    xo_ref, h2_ref, lg_ref):
    y = jnp.dot(m_ref[...], w_ref[...], preferred_element_type=F32)
    x = x_ref[...] + g1_ref[0] * y
    xo_ref[...] = x
    h2 = x * lax.rsqrt(jnp.mean(x * x, axis=-1, keepdims=True) + EPS) * nf_ref[...]
    h2 = (h2 * (1.0 + sc2_ref[0]) + sh2_ref[0]).astype(BF16)
    h2_ref[...] = h2
    lg_ref[...] = jnp.dot(h2, wr_ref[...], preferred_element_type=F32) + br_ref[...]


def _outproj(merged, w_out, x, mod3, norm_ffn, w_router_pad, b_router_pad, tm, rows_per_mod, mod_base):
    N, D = x.shape
    R = mod3.shape[1]
    mspec = lambda k: pl.BlockSpec((1, R, D), lambda i: ((mod_base + (i * tm) // rows_per_mod) * 6 + k, 0, 0))
    full = lambda a: pl.BlockSpec(a.shape, lambda i: (0, 0))
    return pl.pallas_call(
        _outproj_kernel,
        grid=(N // tm,),
        in_specs=[pl.BlockSpec((tm, D), lambda i: (i, 0)), full(w_out), pl.BlockSpec((tm, D), lambda i: (i, 0)),
                  mspec(2), mspec(4), mspec(3), full(norm_ffn), full(w_router_pad), full(b_router_pad)],
        out_specs=[pl.BlockSpec((tm, D), lambda i: (i, 0)), pl.BlockSpec((tm, D), lambda i: (i, 0)),
                   pl.BlockSpec((tm, LANE), lambda i: (i, 0))],
        out_shape=[jax.ShapeDtypeStruct((N, D), F32), jax.ShapeDtypeStruct((N, D), BF16),
                   jax.ShapeDtypeStruct((N, LANE), F32)],
        compiler_params=_cparams(("parallel",)),
        name="outproj",
    )(merged, w_out, x, mod3, mod3, mod3, norm_ffn, w_router_pad, b_router_pad)


def _rank_select(score, own, nblk, topk):
    lane = lax.broadcasted_iota(jnp.int32, score.shape, 1)
    past = lane < own
    s = jnp.where(past, score, NEG)
    rank = jnp.zeros(score.shape, jnp.int32)
    for j in range(nblk):
        col = s[:, j:j + 1]
        beats = (col > s) | ((col == s) & (lane > j))
        rank = rank + beats.astype(jnp.int32)
    return (past & (rank < topk)) | (lane == own)


def _moba_select_kernel(q_ref, k_ref, sel_ref):
    T = q_ref.shape[1]
    nb = T // MOBA_BLOCK
    k = k_ref[0]
    kmean = jnp.sum(k.reshape(nb, MOBA_BLOCK, HEAD_DIM), axis=1) * (1.0 / MOBA_BLOCK)
    kmean = jnp.concatenate([kmean, jnp.zeros((LANE - nb, HEAD_DIM), F32)], axis=0)
    score = lax.dot_general(q_ref[0].astype(BF16), kmean.astype(BF16), (((1,), (1,)), ((), ())),
                            preferred_element_type=F32)
    own = lax.broadcasted_iota(jnp.int32, (T, 1), 0) // MOBA_BLOCK
    sel = _rank_select(score, own, nb, MOBA_TOPK)
    sel_ref[0, 0] = sel.astype(BF16)


def _moba_select(za, B, T):
    return pl.pallas_call(
        _moba_select_kernel,
        grid=(B, MOBA_HEADS),
        in_specs=[pl.BlockSpec((1, T, HEAD_DIM), lambda b, h: (b, 0, h)),
                  pl.BlockSpec((1, T, HEAD_DIM), lambda b, h: (b, 0, MOBA_HEADS + h))],
        out_specs=pl.BlockSpec((1, 1, T, LANE), lambda b, h: (b, h, 0, 0)),
        out_shape=jax.ShapeDtypeStruct((B, MOBA_HEADS, T, LANE), BF16),
        compiler_params=_cparams(("parallel", "parallel")),
        name="moba_select",
    )(za, za)


def _flash_kernel(*refs, G, window_tiles, use_sel, per_head_kv):
    if use_sel:
        q_ref, k_ref, v_ref, bias_ref, sel_ref, e_ref, o_ref, m_scr, l_scr, acc_scr = refs
    else:
        q_ref, k_ref, v_ref, bias_ref, o_ref, m_scr, l_scr, acc_scr = refs
    qi = pl.program_id(1)
    kj = pl.program_id(2)
    nk = pl.num_programs(2)
    tq, tk = ATT_TILE, ATT_TILE

    @pl.when(kj == 0)
    def _():
        m_scr[...] = jnp.full(m_scr.shape, NEG, F32)
        l_scr[...] = jnp.zeros(l_scr.shape, F32)
        acc_scr[...] = jnp.zeros(acc_scr.shape, F32)

    lo = jnp.maximum(qi - window_tiles, 0) if window_tiles is not None else 0

    @pl.when((kj >= lo) & (kj <= qi))
    def _():
        row = lax.broadcasted_iota(jnp.int32, (tq, tk), 0)
        col = lax.broadcasted_iota(jnp.int32, (tq, tk), 1)
        dist = (qi - kj) * tq + row - col
        band = dist >= 0
        if window_tiles is not None:
            band = band & (dist < WINDOW)
        mask = band
        for g in range(G):
            kv = slice(g * HEAD_DIM, (g + 1) * HEAD_DIM) if per_head_kv else slice(0, HEAD_DIM)
            if g == 0 or per_head_kv:
                k = k_ref[0, :, kv].astype(BF16)
                v = v_ref[0, :, kv].astype(BF16)
                if use_sel:
                    hit = jnp.dot(sel_ref[0, g], e_ref[...], preferred_element_type=F32)
                    mask = band & (hit > 0.5)
            q = q_ref[0, :, g * HEAD_DIM:(g + 1) * HEAD_DIM].astype(BF16)
            s = lax.dot_general(q, k, (((1,), (1,)), ((), ())), preferred_element_type=F32)
            s = s * SCALE + bias_ref[g, 0]
            s = jnp.where(mask, s, NEG)
            m_prev = m_scr[g]
            m_new = jnp.maximum(m_prev, jnp.max(s, axis=1, keepdims=True))
            p = jnp.exp(s - m_new)
            alpha = jnp.exp(m_prev - m_new)
            l_scr[g] = alpha * l_scr[g] + jnp.sum(p, axis=1, keepdims=True)
            acc_scr[g] = alpha * acc_scr[g] + jnp.dot(p.astype(BF16), v, preferred_element_type=F32)
            m_scr[g] = m_new

    @pl.when(kj == nk - 1)
    def _():
        for g in range(G):
            l = l_scr[g]
            o = acc_scr[g] / jnp.where(l > 0.0, l, 1.0)
            o_ref[0, :, g * HEAD_DIM:(g + 1) * HEAD_DIM] = o


def _flash(za, bias, *, B, T, q_col, k_col, v_col, bias_blk, per_head_kv, window_tiles=None, sel=None, emat=None):
    tq = tk = ATT_TILE
    nq, nk = T // tq, T // tk
    G = NSA_HEADS
    use_sel = sel is not None
    kv_w = G * HEAD_DIM if per_head_kv else HEAD_DIM

    def kv_blk(i, j):
        lo = jnp.maximum(i - window_tiles, 0) if window_tiles is not None else 0
        return jnp.clip(j, lo, i)

    in_specs = [
        pl.BlockSpec((1, tq, G * HEAD_DIM), lambda b, i, j: (b, i, q_col)),
        pl.BlockSpec((1, tk, kv_w), lambda b, i, j: (b, kv_blk(i, j), k_col)),
        pl.BlockSpec((1, tk, kv_w), lambda b, i, j: (b, kv_blk(i, j), v_col)),
        pl.BlockSpec((G, 1, tq, tk), lambda b, i, j: (bias_blk, jnp.clip(i - j, 0, 2), 0, 0)),
    ]
    args = [za, za, za, bias]
    if use_sel:
        in_specs += [pl.BlockSpec((1, sel.shape[1], tq, LANE), lambda b, i, j: (b, 0, i, 0)),
                     pl.BlockSpec((LANE, tk), lambda b, i, j: (0, kv_blk(i, j)))]
        args += [sel, emat]
    return pl.pallas_call(
        functools.partial(_flash_kernel, G=G, window_tiles=window_tiles, use_sel=use_sel, per_head_kv=per_head_kv),
        grid=(B, nq, nk),
        in_specs=in_specs,
        out_specs=pl.BlockSpec((1, tq, G * HEAD_DIM), lambda b, i, j: (b, i, 0)),
        out_shape=jax.ShapeDtypeStruct((B, T, G * HEAD_DIM), F32),
        scratch_shapes=[pltpu.VMEM((G, tq, 1), F32), pltpu.VMEM((G, tq, 1), F32),
                        pltpu.VMEM((G, tq, HEAD_DIM), F32)],
        compiler_params=_cparams(("parallel", "parallel", "arbitrary")),
        name="flash",
    )(*args)


def _cmp_kernel(ck_ref, cv_ref, q_ref, pek_ref, pev_ref, wk_ref, wv_ref, bias_ref, ovl_ref,
                o_ref, sel_ref, kc_scr, vc_scr):
    T = ck_ref.shape[1]
    n_ch = T // CMP_STRIDE
    i = pl.program_id(1)
    tq = ATT_TILE

    @pl.when(i == 0)
    def _():
        for src, pe, w, dst in ((ck_ref, pek_ref, wk_ref, kc_scr), (cv_ref, pev_ref, wv_ref, vc_scr)):
            first = jnp.zeros((n_ch, HEAD_DIM), F32)
            second = jnp.zeros((n_ch, HEAD_DIM), F32)
            for l in range(CMP_STRIDE):
                x = src[0, pl.ds(l, n_ch, stride=CMP_STRIDE), :]
                l2 = CMP_STRIDE + l
                first = first + jnp.dot((x + pe[l:l + 1, :]).astype(BF16), w[l * HEAD_DIM:(l + 1) * HEAD_DIM, :],
                                        preferred_element_type=F32)
                second = second + jnp.dot((x + pe[l2:l2 + 1, :]).astype(BF16),
                                          w[l2 * HEAD_DIM:(l2 + 1) * HEAD_DIM, :], preferred_element_type=F32)
            dst[...] = (first + pltpu.roll(second, n_ch - 1, 0)).astype(BF16)

    kc = kc_scr[...]
    vc = vc_scr[...]
    t = i * tq + lax.broadcasted_iota(jnp.int32, (tq, n_ch), 0)
    n = lax.broadcasted_iota(jnp.int32, (tq, n_ch), 1)
    vis = (n * CMP_STRIDE + (CMP_LEN - 1)) <= t
    imp = jnp.zeros((tq, LANE), F32)
    ovl = ovl_ref[...]
    for g in range(NSA_HEADS):
        q = q_ref[0, :, g * HEAD_DIM:(g + 1) * HEAD_DIM].astype(BF16)
        s = lax.dot_general(q, kc, (((1,), (1,)), ((), ())), preferred_element_type=F32)
        s = s * SCALE + bias_ref[g]
        s = jnp.where(vis, s, NEG)
        m = jnp.max(s, axis=1, keepdims=True)
        e = jnp.where(vis, jnp.exp(s - m), 0.0)
        l = jnp.sum(e, axis=1, keepdims=True)
        p = e / jnp.where(l > 0.0, l, 1.0)
        pb = p.astype(BF16)
        o_ref[0, :, g * HEAD_DIM:(g + 1) * HEAD_DIM] = jnp.dot(pb, vc, preferred_element_type=F32)
        imp = imp + jnp.dot(pb, ovl, preferred_element_type=F32)
    own = (i * tq + lax.broadcasted_iota(jnp.int32, (tq, 1), 0)) // SLC_BLOCK
    sel = _rank_select(imp, own, T // SLC_BLOCK, SLC_TOPK - 1)
    sel_ref[0, 0] = sel.astype(BF16)


def _cmp_branch(za, pe_k, pe_v, w_k, w_v, bias_cmp, overlap, B, T):
    assert T // CMP_STRIDE == LANE
    tq = ATT_TILE
    col = 3 * MOBA_HEADS + NSA_HEADS
    return pl.pallas_call(
        _cmp_kernel,
        grid=(B, T // tq),
        in_specs=[
            pl.BlockSpec((1, T, HEAD_DIM), lambda b, i: (b, 0, col)),
            pl.BlockSpec((1, T, HEAD_DIM), lambda b, i: (b, 0, col + 1)),
            pl.BlockSpec((1, tq, NSA_W), lambda b, i: (b, i, 3)),
            pl.BlockSpec((CMP_LEN, HEAD_DIM), lambda b, i: (0, 0)),
            pl.BlockSpec((CMP_LEN, HEAD_DIM), lambda b, i: (0, 0)),
            pl.BlockSpec((CMP_LEN * HEAD_DIM, HEAD_DIM), lambda b, i: (0, 0)),
            pl.BlockSpec((CMP_LEN * HEAD_DIM, HEAD_DIM), lambda b, i: (0, 0)),
            pl.BlockSpec((NSA_HEADS, tq, LANE), lambda b, i: (0, i, 0)),
            pl.BlockSpec((LANE, LANE), lambda b, i: (0, 0)),
        ],
        out_specs=[pl.BlockSpec((1, tq, NSA_W), lambda b, i: (b, i, 0)),
                   pl.BlockSpec((1, 1, tq, LANE), lambda b, i: (b, 0, i, 0))],
        out_shape=[jax.ShapeDtypeStruct((B, T, NSA_W), F32),
                   jax.ShapeDtypeStruct((B, 1, T, LANE), BF16)],
        scratch_shapes=[pltpu.VMEM((LANE, HEAD_DIM), BF16), pltpu.VMEM((LANE, HEAD_DIM), BF16)],
        compiler_params=_cparams(("parallel", "arbitrary")),
        name="nsa_cmp",
    )(za, za, za, pe_k, pe_v, w_k, w_v, bias_cmp, overlap)


def _gelu_tanh(x):
    return 0.5 * x * (1.0 + jnp.tanh(math.sqrt(2.0 / math.pi) * (x + 0.044715 * (x * x * x))))


def _softplus(x):
    return jnp.maximum(x, 0.0) + jnp.log1p(jnp.exp(-jnp.abs(x)))


def _lru_kernel(x_ref, g_ref, c0_ref, h0_ref, cw_ref, cb_ref, wr_ref, br_ref, wi_ref, bi_ref, lam_ref,
                y_ref, hl_ref, cl_ref, a_scr, x_scr):
    T = x_ref.shape[1]
    x = x_ref[0]
    row = lax.broadcasted_iota(jnp.int32, (T, LRU_BLOCK), 0)
    c0 = c0_ref[0]

    def shifted(d):
        r = pltpu.roll(x, d, 0)
        for t in range(d):
            r = jnp.where(row == t, c0[t + CONV_W - 1 - d:t + CONV_W - d, :], r)
        return r

    u = cb_ref[...] + shifted(3) * cw_ref[0:1, :]
    u = u + shifted(2) * cw_ref[1:2, :]
    u = u + shifted(1) * cw_ref[2:3, :]
    u = u + x * cw_ref[3:4, :]
    ub = u.astype(BF16)
    r = jax.nn.sigmoid(jnp.dot(ub, wr_ref[0], preferred_element_type=F32) + br_ref[...])
    ig = jax.nn.sigmoid(jnp.dot(ub, wi_ref[0], preferred_element_type=F32) + bi_ref[...])
    log_a = (-LRU_C * r) * _softplus(-lam_ref[...])
    a = jnp.exp(log_a)
    b = jnp.sqrt(1.0 - jnp.exp(2.0 * log_a)) * (ig * u)
    sub = row % 8
    for d in (1, 2, 4):
        ok = sub >= d
        b = jnp.where(ok, a * pltpu.roll(b, d, 0) + b, b)
        a = jnp.where(ok, a * pltpu.roll(a, d, 0), a)
    a_scr[...] = a
    x_scr[...] = b

    def group(k, carry):
        s = pl.multiple_of(k * 8, 8)
        h = x_scr[pl.ds(s, 8), :] + a_scr[pl.ds(s, 8), :] * carry
        x_scr[pl.ds(s, 8), :] = h
        return h[7:8, :]

    h_last = lax.fori_loop(0, T // 8, group, h0_ref[0], unroll=8)
    y_ref[0] = x_scr[...] * _gelu_tanh(g_ref[0])
    hl_ref[0] = h_last
    cl_ref[0] = x[T - (CONV_W - 1):, :]


def _rglru(zb, conv0, h0, conv_w, conv_b, w_r, b_r, w_i, b_i, lam, B, T):
    nblk = LRU_BLOCKS
    vec = lambda a: a.reshape(1, LRU_WIDTH)
    row_spec = pl.BlockSpec((1, LRU_BLOCK), lambda b, n: (0, n))
    return pl.pallas_call(
        _lru_kernel,
        grid=(B, nblk),
        in_specs=[
            pl.BlockSpec((1, T, LRU_BLOCK), lambda b, n: (b, 0, n)),
            pl.BlockSpec((1, T, LRU_BLOCK), lambda b, n: (b, 0, nblk + n)),
            pl.BlockSpec((1, CONV_W - 1, LRU_BLOCK), lambda b, n: (b, 0, n)),
            pl.BlockSpec((1, 1, LRU_BLOCK), lambda b, n: (b, 0, n)),
            pl.BlockSpec((CONV_W, LRU_BLOCK), lambda b, n: (0, n)),
            row_spec,
            pl.BlockSpec((1, LRU_BLOCK, LRU_BLOCK), lambda b, n: (n, 0, 0)),
            row_spec,
            pl.BlockSpec((1, LRU_BLOCK, LRU_BLOCK), lambda b, n: (n, 0, 0)),
            row_spec,
            row_spec,
        ],
        out_specs=[pl.BlockSpec((1, T, LRU_BLOCK), lambda b, n: (b, 0, n)),
                   pl.BlockSpec((1, 1, LRU_BLOCK), lambda b, n: (b, 0, n)),
                   pl.BlockSpec((1, CONV_W - 1, LRU_BLOCK), lambda b, n: (b, 0, n))],
        out_shape=[jax.ShapeDtypeStruct((B, T, LRU_WIDTH), F32),
                   jax.ShapeDtypeStruct((B, 1, LRU_WIDTH), F32),
                   jax.ShapeDtypeStruct((B, CONV_W - 1, LRU_WIDTH), F32)],
        scratch_shapes=[pltpu.VMEM((T, LRU_BLOCK), F32), pltpu.VMEM((T, LRU_BLOCK), F32)],
        compiler_params=_cparams(("parallel", "parallel")),
        name="rglru",
    )(zb, zb, conv0, h0.reshape(B, 1, LRU_WIDTH), conv_w, vec(conv_b), w_r.astype(BF16), vec(b_r),
      w_i.astype(BF16), vec(b_i), vec(lam))


def _moe_kernel(eid_ref, nt_ref, x_ref, wg_ref, wu_ref, wd_ref, cw_ref, o_ref, wg_scr, wu_scr, wd_scr):
    t = pl.program_id(0)
    live = t < nt_ref[0]
    new_expert = (t == 0) | (eid_ref[t] != eid_ref[jnp.maximum(t - 1, 0)])

    @pl.when(live & new_expert)
    def _():
        wg_scr[...] = wg_ref[0, 0].astype(BF16)
        wu_scr[...] = wu_ref[0, 0].astype(BF16)
        wd_scr[...] = wd_ref[0, 0].astype(BF16)

    @pl.when(live)
    def _():
        x = x_ref[...]
        g = jnp.dot(x, wg_scr[...], preferred_element_type=F32)
        u = jnp.dot(x, wu_scr[...], preferred_element_type=F32)
        act = (g * jax.nn.sigmoid(g)) * u
        y = jnp.dot(act.astype(BF16), wd_scr[...], preferred_element_type=F32)
        o_ref[...] = y * cw_ref[...]

    @pl.when(t >= nt_ref[0])
    def _():
        o_ref[...] = jnp.zeros(o_ref.shape, F32)


def _moe_ffn(x_pad, cw_pad, tile_eid, n_tiles, l, w_g, w_u, w_d):
    P, D = x_pad.shape
    tm = MOE_TILE
    grid_spec = pltpu.PrefetchScalarGridSpec(
        num_scalar_prefetch=2,
        grid=(P // tm,),
        in_specs=[
            pl.BlockSpec((tm, D), lambda t, eid, nt: (t, 0)),
            pl.BlockSpec((1, 1, D, D_EXPERT), lambda t, eid, nt: (l, eid[t], 0, 0), pipeline_mode=pl.Buffered(1)),
            pl.BlockSpec((1, 1, D, D_EXPERT), lambda t, eid, nt: (l, eid[t], 0, 0), pipeline_mode=pl.Buffered(1)),
            pl.BlockSpec((1, 1, D_EXPERT, D), lambda t, eid, nt: (l, eid[t], 0, 0), pipeline_mode=pl.Buffered(1)),
            pl.BlockSpec((tm, 1), lambda t, eid, nt: (t, 0)),
        ],
        out_specs=pl.BlockSpec((tm, D), lambda t, eid, nt: (t, 0)),
        scratch_shapes=[pltpu.VMEM((D, D_EXPERT), BF16), pltpu.VMEM((D, D_EXPERT), BF16),
                        pltpu.VMEM((D_EXPERT, D), BF16)],
    )
    return pl.pallas_call(
        _moe_kernel,
        grid_spec=grid_spec,
        out_shape=jax.ShapeDtypeStruct((P, D), F32),
        compiler_params=_cparams(("arbitrary",)),
        name="moe_ffn",
    )(tile_eid, n_tiles, x_pad, w_g, w_u, w_d, cw_pad)


def _route(logits):
    probs = jax.nn.softmax(logits.astype(F32), axis=-1)
    grp = probs.reshape(-1, N_GROUPS, EXPERTS_PER_GROUP)
    pairs = [grp[..., a] + grp[..., b] for a in range(EXPERTS_PER_GROUP) for b in range(a + 1, EXPERTS_PER_GROUP)]
    g_score = functools.reduce(jnp.maximum, pairs)
    g_best = jnp.argmax(g_score, axis=-1)
    pick = g_best[:, None, None] == jnp.arange(N_GROUPS)[None, :, None]
    in_grp = jnp.sum(jnp.where(pick, grp, 0.0), axis=1)
    i1 = jnp.argmax(in_grp, axis=-1)
    first = jnp.arange(EXPERTS_PER_GROUP)[None, :] == i1[:, None]
    w1 = jnp.max(in_grp, axis=-1)
    rest = jnp.where(first, -jnp.inf, in_grp)
    i2 = jnp.argmax(rest, axis=-1)
    w2 = jnp.max(rest, axis=-1)
    w_top = jnp.stack([w1, w2], axis=-1)
    w_top = w_top / jnp.sum(w_top, axis=-1, keepdims=True)
    e_idx = g_best[:, None] * EXPERTS_PER_GROUP + jnp.stack([i1, i2], axis=-1)
    return e_idx.astype(jnp.int32), w_top


def _moe_prompt(h2, logits, l, w_g, w_u, w_d):
    N, D = h2.shape
    tm = MOE_TILE
    e_idx, w_top = _route(logits)
    flat_e = e_idx.reshape(-1)
    onehot = (flat_e[:, None] == jnp.arange(N_EXPERTS)[None, :]).astype(jnp.int32)
    within = jnp.sum(onehot * jnp.cumsum(onehot, axis=0), axis=1) - 1
    sizes = jnp.sum(onehot, axis=0)
    padded = ((sizes + tm - 1) // tm) * tm
    pend = jnp.cumsum(padded)
    pstart = pend - padded
    dest = jnp.sum(onehot * pstart[None, :], axis=1) + within
    P = 2 * N + N_EXPERTS * tm
    src_tok = jnp.zeros((P,), jnp.int32).at[dest].set(jnp.arange(2 * N, dtype=jnp.int32) // TOP_K)
    cw_pad = jnp.zeros((P,), F32).at[dest].set(w_top.reshape(-1))
    x_pad = h2[src_tok]
    tile_start = jnp.arange(P // tm, dtype=jnp.int32) * tm
    tile_eid = jnp.minimum(jnp.sum(tile_start[:, None] >= pend[None, :], axis=1), N_EXPERTS - 1).astype(jnp.int32)
    n_tiles = (pend[-1] // tm).astype(jnp.int32).reshape(1)
    out = _moe_ffn(x_pad, cw_pad.reshape(P, 1), tile_eid, n_tiles, l, w_g, w_u, w_d)
    pos = dest.reshape(N, TOP_K)
    return out[pos[:, 0]] + out[pos[:, 1]]


def _t5_bucket(dist):
    n = jnp.maximum(dist, 0)
    exact = T5_BUCKETS // 2
    nf = jnp.maximum(n, 1).astype(F32)
    large = exact + (jnp.log(nf / exact) / math.log(T5_MAX_DIST / exact) * (T5_BUCKETS - exact)).astype(jnp.int32)
    return jnp.where(n < exact, n, jnp.minimum(large, T5_BUCKETS - 1))


def _bias_lookup(tab, dist):
    bucket = _t5_bucket(dist)
    out = jnp.zeros((tab.shape[0],) + bucket.shape, F32)
    for b in range(T5_BUCKETS):
        out = jnp.where(bucket[None] == b, tab[:, b].reshape((-1,) + (1,) * bucket.ndim), out)
    return out


def _rms(x, g):
    return x * lax.rsqrt(jnp.mean(x * x, axis=-1, keepdims=True) + EPS) * g


def _split(z, widths):
    outs, s = [], 0
    for w in widths:
        outs.append(z[..., s:s + w])
        s += w
    return outs


def _prompt_tables(t5_bias, T):
    t = ATT_TILE
    i = jnp.arange(t)
    dist = jnp.arange(3)[:, None, None] * t + i[None, :, None] - i[None, None, :]
    bias_tiles = _bias_lookup(t5_bias.T, dist)
    pos = jnp.arange(T)
    lane = jnp.arange(LANE)
    e_moba = (pos[None, :] // MOBA_BLOCK == lane[:, None]).astype(BF16)
    e_slc = (pos[None, :] // SLC_BLOCK == lane[:, None]).astype(BF16)
    d_cmp = pos[:, None] - (lane[None, :] * CMP_STRIDE + CMP_LEN - 1)
    bias_cmp = _bias_lookup(t5_bias.T[MOBA_HEADS:], d_cmp)
    c_start = lane * CMP_STRIDE
    s_start = lane * SLC_BLOCK
    n_cmp = T // CMP_STRIDE - 1
    overlap = ((c_start[:, None] < s_start[None, :] + SLC_BLOCK) & (c_start[:, None] + CMP_LEN > s_start[None, :])
               & (lane[:, None] < n_cmp) & (lane[None, :] < T // SLC_BLOCK)).astype(BF16)
    return bias_tiles, e_moba, e_slc, bias_cmp, overlap


def _mixer_prompt(x, mod, mod3, lp, tabs, w_router_pad, b_router_pad):
    B, T, D = x.shape
    N = B * T
    bias_tiles, e_moba, e_slc, bias_cmp, overlap = tabs
    sh1, sc1 = mod[:, None, :D], mod[:, None, D:2 * D]
    h = (_rms(x, lp['norm_mix']) * (1.0 + sc1) + sh1).astype(BF16).reshape(N, D)
    za = _matmul_w32(h, lp['w_in'], lp['l'], 0, ZA_W, ZA_TILE, 1024).reshape(B, T, ZA_W)
    zb = _matmul_w32(h, lp['w_in'], lp['l'], COL_LX, ZB_W, ZB_TILE, 1024).reshape(B, T, ZB_W)
    new_kv = jnp.concatenate([za[..., MOBA_W:3 * MOBA_W], za[..., 3 * MOBA_W + NSA_W:COL_NG - WIN_DIM]], axis=-1)
    win = za[:, T - WINDOW:, COL_NG - WIN_DIM:COL_NG]

    sel_moba = _moba_select(za, B, T)
    o_a = _flash(za, bias_tiles, B=B, T=T, q_col=0, k_col=1, v_col=2, bias_blk=0, per_head_kv=True,
                 sel=sel_moba, emat=e_moba)
    o_cmp, sel_slc = _cmp_branch(za, lp['cmp_pos_k'], lp['cmp_pos_v'], lp['w_cmp_k'].reshape(-1, HEAD_DIM),
                                 lp['w_cmp_v'].reshape(-1, HEAD_DIM), bias_cmp, overlap, B, T)
    o_slc = _flash(za, bias_tiles, B=B, T=T, q_col=3, k_col=18, v_col=19, bias_blk=1, per_head_kv=False,
                   sel=sel_slc, emat=e_slc)
    o_win = _flash(za, bias_tiles, B=B, T=T, q_col=3, k_col=20, v_col=21, bias_blk=1, per_head_kv=False,
                   window_tiles=WINDOW // ATT_TILE)
    o_c, h_last, conv_last = _rglru(zb, jnp.zeros((B, CONV_W - 1, LRU_WIDTH), F32), jnp.zeros((B, LRU_WIDTH), F32),
                                    lp['conv_w'], lp['conv_b'], lp['w_rg'], lp['b_rg'], lp['w_ig'], lp['b_ig'],
                                    lp['lru_lambda'], B, T)
    flat = lambda a: a.reshape(N, a.shape[-1])
    merged = _merge(flat(o_a), flat(o_cmp), flat(o_slc), flat(o_win), flat(za), flat(o_c), flat(zb),
                    lp['w_br_moba'], lp['w_br_nsa'], lp['w_br_lru'], 256)
    x, h2, logits = _outproj(merged, lp['w_out'], x.reshape(N, D), mod3, lp['norm_ffn'].reshape(1, D),
                             w_router_pad, b_router_pad, 256, T, 0)
    return x, h2, logits, new_kv, win, h_last.reshape(B, LRU_WIDTH), conv_last


def _mixer_decode(l, x, mod, lp, cache_kv, cache_win, page_table, h0, conv0, dtabs, t5_bias, w_router_pad,
                  b_router_pad):
    B, D = x.shape
    pad = lambda a: jnp.pad(a, ((0, DEC_ROWS - B),) + ((0, 0),) * (a.ndim - 1))
    sh1, sc1 = mod[:, :D], mod[:, D:2 * D]
    h = pad((_rms(x, lp['norm_mix']) * (1.0 + sc1) + sh1).astype(BF16))
    za = _matmul_w32(h, lp['w_in'], l, 0, ZA_W, ZA_TILE, DEC_ROWS)
    zb = _matmul_w32(h, lp['w_in'], l, COL_LX, ZB_W, ZB_TILE, DEC_ROWS)
    new_kv = jnp.concatenate([za[:B, MOBA_W:3 * MOBA_W], za[:B, 3 * MOBA_W + NSA_W:COL_NG - WIN_DIM]], axis=-1)
    n_win = cache_win.shape[2]
    win = jnp.concatenate([cache_win[l], za[:B, None, COL_NG - WIN_DIM:COL_NG]], axis=1)
    win = win[:, win.shape[1] - min(WINDOW, n_win + 1):]
    cmp_w = (lp['cmp_pos_k'], lp['cmp_pos_v'], lp['w_cmp_k'], lp['w_cmp_v'])
    o_a, o_cmp, o_slc, o_win = _decode_attention(l, za, cache_kv, cache_win, page_table, cmp_w, dtabs, t5_bias)
    o_c, h_new, conv_t = _lru_step(zb, conv0.transpose(1, 0, 2), h0, lp['conv_w'], lp['conv_b'], lp['w_rg'],
                                   lp['b_rg'], lp['w_ig'], lp['b_ig'], lp['lru_lambda'])
    merged = _merge(o_a, o_cmp, o_slc, o_win, za, pad(o_c), zb, lp['w_br_moba'], lp['w_br_nsa'], lp['w_br_lru'],
                    DEC_ROWS)
    mod3 = pad(mod).reshape(DEC_ROWS, 6, D).transpose(1, 0, 2)
    x, h2, logits = _outproj(merged, lp['w_out'], pad(x), mod3, lp['norm_ffn'].reshape(1, D), w_router_pad,
                             b_router_pad, DEC_ROWS, DEC_ROWS, 0)
    return x[:B], h2, logits, new_kv, win, h_new, conv_t.transpose(1, 0, 2)


SCAN_PAGES = 32
DEC_ROWS = 16


def _page_scan_kernel(pt_ref, *refs):
    del pt_ref
    ck_refs, cv_refs, mk_refs = refs[:SCAN_PAGES], refs[SCAN_PAGES:2 * SCAN_PAGES], refs[2 * SCAN_PAGES:3 * SCAN_PAGES]
    pek_ref, pev_ref, wk_ref, wv_ref, abk_ref, abv_ref, km_ref = refs[3 * SCAN_PAGES:]
    chunks = PAGE_SIZE // CMP_STRIDE
    for src, pe, w, dst in ((ck_refs, pek_ref, wk_ref, abk_ref), (cv_refs, pev_ref, wv_ref, abv_ref)):
        first = jnp.zeros((SCAN_PAGES * chunks, HEAD_DIM), F32)
        second = jnp.zeros((SCAN_PAGES * chunks, HEAD_DIM), F32)
        for l in range(CMP_STRIDE):
            x = jnp.concatenate([r[0, 0, pl.ds(l, chunks, stride=CMP_STRIDE), :] for r in src], axis=0)
            l2 = CMP_STRIDE + l
            first = first + jnp.dot((x + pe[l:l + 1, :]).astype(BF16), w[l], preferred_element_type=F32)
            second = second + jnp.dot((x + pe[l2:l2 + 1, :]).astype(BF16), w[l2], preferred_element_type=F32)
        dst[0, :, 0:HEAD_DIM] = first
        dst[0, :, HEAD_DIM:2 * HEAD_DIM] = second
    sums = [jnp.sum(r[0, 0], axis=0, keepdims=True) for r in mk_refs]
    per_blk = MOBA_BLOCK // PAGE_SIZE
    means = [functools.reduce(lambda a, b: a + b, sums[j * per_blk:(j + 1) * per_blk])
             for j in range(SCAN_PAGES // per_blk)]
    km_ref[0] = jnp.concatenate(means, axis=0) * (1.0 / MOBA_BLOCK)


def _page_scan(cache_kv, l, page_table, pe_k, pe_v, w_k, w_v):
    B, n_pages = page_table.shape
    steps = n_pages // SCAN_PAGES
    chunks = PAGE_SIZE // CMP_STRIDE
    blocks = SCAN_PAGES * PAGE_SIZE // MOBA_BLOCK
    cmp_col = 2 * MOBA_W // HEAD_DIM

    def cmp_spec(j, col):
        return pl.BlockSpec((1, 1, PAGE_SIZE, HEAD_DIM), lambda b, s, pt: (l, pt[b, s * SCAN_PAGES + j], 0, col))

    def mk_spec(j):
        return pl.BlockSpec((1, 1, PAGE_SIZE, MOBA_W), lambda b, s, pt: (l, pt[b, s * SCAN_PAGES + j], 0, 0))

    const = lambda a: pl.BlockSpec(a.shape, lambda b, s, pt: (0,) * a.ndim)
    grid_spec = pltpu.PrefetchScalarGridSpec(
        num_scalar_prefetch=1,
        grid=(B, steps),
        in_specs=([cmp_spec(j, cmp_col) for j in range(SCAN_PAGES)]
                  + [cmp_spec(j, cmp_col + 1) for j in range(SCAN_PAGES)] + [mk_spec(j) for j in range(SCAN_PAGES)]
                  + [const(pe_k), const(pe_v), const(w_k), const(w_v)]),
        out_specs=[pl.BlockSpec((1, SCAN_PAGES * chunks, 2 * HEAD_DIM), lambda b, s, pt: (b, s, 0)),
                   pl.BlockSpec((1, SCAN_PAGES * chunks, 2 * HEAD_DIM), lambda b, s, pt: (b, s, 0)),
                   pl.BlockSpec((1, blocks, MOBA_W), lambda b, s, pt: (b, s, 0))],
    )
    n_ch = n_pages * chunks
    return pl.pallas_call(
        _page_scan_kernel,
        grid_spec=grid_spec,
        out_shape=[jax.ShapeDtypeStruct((B, n_ch, 2 * HEAD_DIM), F32),
                   jax.ShapeDtypeStruct((B, n_ch, 2 * HEAD_DIM), F32),
                   jax.ShapeDtypeStruct((B, steps * blocks, MOBA_W), F32)],
        compiler_params=_cparams(("parallel", "arbitrary")),
        name="page_scan",
    )(page_table, *([cache_kv] * (3 * SCAN_PAGES)), pe_k, pe_v, w_k, w_v)


def _topk_lanes(score, k):
    lane = lax.broadcasted_iota(jnp.int32, score.shape, 1)
    out = jnp.zeros(score.shape, jnp.int32)
    for r in range(k):
        m = jnp.max(score, axis=1, keepdims=True)
        idx = jnp.min(jnp.where(score == m, lane, score.shape[1]), axis=1, keepdims=True)
        out = jnp.where(lane == r, idx, out)
        score = jnp.where(lane == idx, -jnp.inf, score)
    return out


def _head_rows(q):
    rows = [q[:, g * HEAD_DIM:(g + 1) * HEAD_DIM] for g in range(NSA_HEADS)]
    return jnp.concatenate(rows + [jnp.zeros((8 - NSA_HEADS, HEAD_DIM), q.dtype)], axis=0)


def _dec_select_kernel(abk_ref, abv_ref, km_ref, mq_ref, nq_ref, bias_ref, ovl_ref, ocmp_ref, tm_ref, ts_ref,
                       *, past):
    b = pl.program_id(0)
    n_ch = abk_ref.shape[1]
    nb = km_ref.shape[1]
    row = lax.broadcasted_iota(jnp.int32, (8, LANE), 0)
    mq = mq_ref[pl.ds(b, 1), :]
    sc = jnp.full((8, LANE), NEG, F32)
    for h in range(MOBA_HEADS):
        hs = slice(h * HEAD_DIM, (h + 1) * HEAD_DIM)
        qh = jnp.broadcast_to(mq[:, hs], (8, HEAD_DIM)).astype(BF16)
        s = lax.dot_general(qh, km_ref[0, :, hs].astype(BF16), (((1,), (1,)), ((), ())), preferred_element_type=F32)
        s = jnp.concatenate([s, jnp.full((8, LANE - nb), NEG, F32)], axis=1)
        sc = jnp.where(row == h, s, sc)
    tm_ref[0] = _topk_lanes(sc, MOBA_TOPK)
    abk = abk_ref[0]
    abv = abv_ref[0]
    kc = (abk[:, :HEAD_DIM] + pltpu.roll(abk[:, HEAD_DIM:], n_ch - 1, 0)).astype(BF16)
    vc = (abv[:, :HEAD_DIM] + pltpu.roll(abv[:, HEAD_DIM:], n_ch - 1, 0)).astype(BF16)
    q4 = _head_rows(nq_ref[pl.ds(b, 1), :]).astype(BF16)
    s = lax.dot_general(q4, kc, (((1,), (1,)), ((), ())), preferred_element_type=F32) * SCALE + bias_ref[...]
    n = lax.broadcasted_iota(jnp.int32, (8, n_ch), 1)
    vis = (n < n_ch - 1) & (n * CMP_STRIDE + (CMP_LEN - 1) <= past)
    s = jnp.where(vis, s, NEG)
    m = jnp.max(s, axis=1, keepdims=True)
    e = jnp.where(vis, jnp.exp(s - m), 0.0)
    den = jnp.sum(e, axis=1, keepdims=True)
    pb = (e / jnp.where(den > 0.0, den, 1.0)).astype(BF16)
    ocmp_ref[0] = jnp.dot(pb, vc, preferred_element_type=F32)
    imp = jnp.dot(pb, ovl_ref[...], preferred_element_type=F32)
    rows = lax.broadcasted_iota(jnp.int32, imp.shape, 0)
    imp = jnp.sum(jnp.where(rows < NSA_HEADS, imp, 0.0), axis=0, keepdims=True)
    ts_ref[0] = _topk_lanes(jnp.broadcast_to(imp, (8, imp.shape[1])), SLC_TOPK - 1)[:, :LANE]


def _dec_select(abk, abv, kmean, za_s, bias_cmp, overlap, past):
    B, n_ch, _ = abk.shape
    full = lambda a: pl.BlockSpec(a.shape, lambda b: (0,) * a.ndim)
    per_b = lambda a: pl.BlockSpec((1,) + a.shape[1:], lambda b: (b, 0, 0))
    out = jax.ShapeDtypeStruct((B, 8, LANE), F32)
    outi = jax.ShapeDtypeStruct((B, 8, LANE), jnp.int32)
    return pl.pallas_call(
        functools.partial(_dec_select_kernel, past=past),
        grid=(B,),
        in_specs=[per_b(abk), per_b(abv), per_b(kmean),
                  pl.BlockSpec((DEC_ROWS, MOBA_W), lambda b: (0, 0)),
                  pl.BlockSpec((DEC_ROWS, NSA_W), lambda b: (0, 3)),
                  full(bias_cmp), full(overlap)],
        out_specs=[pl.BlockSpec((1, 8, LANE), lambda b: (b, 0, 0))] * 3,
        out_shape=[out, outi, outi],
        compiler_params=_cparams(("parallel",)),
        name="dec_select",
    )(abk, abv, kmean, za_s, za_s, bias_cmp, overlap)


def _softmax_parts(scores):
    m = functools.reduce(jnp.maximum, [jnp.max(s, axis=1, keepdims=True) for s in scores])
    es = [jnp.exp(s - m) for s in scores]
    den = functools.reduce(lambda a, b: a + b, [jnp.sum(e, axis=1, keepdims=True) for e in es])
    return [e / den for e in es]


def _dec_attend_kernel(pm_ref, ps_ref, hs_ref, *refs, n_moba, n_slc, n_win):
    del pm_ref, ps_ref, hs_ref
    mk, mv = refs[:n_moba], refs[n_moba:2 * n_moba]
    sk, sv = refs[2 * n_moba:2 * n_moba + n_slc], refs[2 * n_moba + n_slc:2 * (n_moba + n_slc)]
    (kw_ref, vw_ref, mq_ref, mkn_ref, mvn_ref, nq_ref, skn_ref, svn_ref, wkn_ref, wvn_ref, bm_ref, bs_ref, bw_ref,
     oa_ref, oslc_ref, owin_ref) = refs[2 * (n_moba + n_slc):]
    b = pl.program_id(0)
    h = pl.program_id(1)
    nt = (((1,), (1,)), ((), ()))
    rnd = lambda a: a.astype(BF16).astype(F32)

    q = mq_ref[pl.ds(b, 1), :]
    q8 = jnp.broadcast_to(q, (8, HEAD_DIM)).astype(BF16)
    bm = bm_ref[0]
    scores = [lax.dot_general(q8, mk[j][0, 0].astype(BF16), nt, preferred_element_type=F32)[0:1] * SCALE
              + bm[j:j + 1] for j in range(n_moba)]
    s_new = jnp.sum(rnd(q) * rnd(mkn_ref[pl.ds(b, 1), :]), axis=1, keepdims=True) * SCALE + bm[n_moba:n_moba + 1, 0:1]
    probs = _softmax_parts(scores + [s_new])
    o = rnd(probs[-1]) * rnd(mvn_ref[pl.ds(b, 1), :])
    for j in range(n_moba):
        pj = jnp.broadcast_to(probs[j], (8, PAGE_SIZE)).astype(BF16)
        o = o + jnp.dot(pj, mv[j][0, 0].astype(BF16), preferred_element_type=F32)[0:1]
    oa_ref[0] = jnp.broadcast_to(o, (8, HEAD_DIM))

    @pl.when(h == 0)
    def _():
        q4f = _head_rows(nq_ref[pl.ds(b, 1), :])
        q4 = q4f.astype(BF16)
        scores = [lax.dot_general(q4, sk[j][0, 0].astype(BF16), nt, preferred_element_type=F32) * SCALE + bs_ref[0, j]
                  for j in range(n_slc)]
        s_new = (jnp.sum(rnd(q4f) * rnd(skn_ref[pl.ds(b, 1), :]), axis=1, keepdims=True) * SCALE
                 + bs_ref[0, n_slc][:, 0:1])
        probs = _softmax_parts(scores + [s_new])
        o = rnd(probs[-1]) * rnd(svn_ref[pl.ds(b, 1), :])
        for j in range(n_slc):
            o = o + jnp.dot(probs[j].astype(BF16), sv[j][0, 0].astype(BF16), preferred_element_type=F32)
        oslc_ref[0] = o
        s = lax.dot_general(q4, kw_ref[0, 0].astype(BF16), nt, preferred_element_type=F32) * SCALE + bw_ref[:, :n_win]
        i = lax.broadcasted_iota(jnp.int32, (8, n_win), 1)
        s = jnp.where(n_win - i < WINDOW, s, NEG)
        s_new = (jnp.sum(rnd(q4f) * rnd(wkn_ref[pl.ds(b, 1), :]), axis=1, keepdims=True) * SCALE
                 + bw_ref[:, n_win:n_win + 1])
        p_win, p_new = _softmax_parts([s, s_new])
        p_win = jnp.where(n_win - i < WINDOW, p_win, 0.0)
        owin_ref[0] = (rnd(p_new) * rnd(wvn_ref[pl.ds(b, 1), :])
                       + jnp.dot(p_win.astype(BF16), vw_ref[0, 0].astype(BF16), preferred_element_type=F32))


def _dec_attend(cache_kv, cache_win, l, za_s, pages_moba, pages_slc, halves_slc, bias_moba, bias_slc, bias_win, B):
    n_moba = MOBA_TOPK * MOBA_BLOCK // PAGE_SIZE
    n_slc = SLC_TOPK - 1
    n_win = cache_win.shape[2]

    def moba_spec(j, col0):
        return pl.BlockSpec((1, 1, PAGE_SIZE, HEAD_DIM),
                            lambda b, h, pm, ps, hs: (l, pm[(b * MOBA_HEADS + h) * n_moba + j], 0, col0 + h))

    def slc_spec(j, col):
        return pl.BlockSpec((1, 1, SLC_BLOCK, HEAD_DIM),
                            lambda b, h, pm, ps, hs: (l, ps[b * n_slc + j], hs[b * n_slc + j], col))

    zcol = lambda w, c: pl.BlockSpec((DEC_ROWS, w), lambda b, h, pm, ps, hs: (0, c))
    zhead = lambda c0: pl.BlockSpec((DEC_ROWS, HEAD_DIM), lambda b, h, pm, ps, hs: (0, c0 + h))
    slc_col = (2 * MOBA_W + 2 * HEAD_DIM) // HEAD_DIM
    in_specs = ([moba_spec(j, 0) for j in range(n_moba)] + [moba_spec(j, MOBA_HEADS) for j in range(n_moba)]
                + [slc_spec(j, slc_col) for j in range(n_slc)] + [slc_spec(j, slc_col + 1) for j in range(n_slc)]
                + [pl.BlockSpec((1, 1, n_win, HEAD_DIM), lambda b, h, pm, ps, hs: (l, b, 0, 0)),
                   pl.BlockSpec((1, 1, n_win, HEAD_DIM), lambda b, h, pm, ps, hs: (l, b, 0, 1)),
                   zhead(0), zhead(MOBA_HEADS), zhead(2 * MOBA_HEADS), zcol(NSA_W, 3),
                   zcol(HEAD_DIM, 18), zcol(HEAD_DIM, 19), zcol(HEAD_DIM, 20), zcol(HEAD_DIM, 21),
                   pl.BlockSpec((1, 8, LANE), lambda b, h, pm, ps, hs: (b * MOBA_HEADS + h, 0, 0)),
                   pl.BlockSpec((1, n_slc + 1, 8, SLC_BLOCK), lambda b, h, pm, ps, hs: (b, 0, 0, 0)),
                   pl.BlockSpec(bias_win.shape, lambda b, h, pm, ps, hs: (0, 0))])
    grid_spec = pltpu.PrefetchScalarGridSpec(
        num_scalar_prefetch=3,
        grid=(B, MOBA_HEADS),
        in_specs=in_specs,
        out_specs=[pl.BlockSpec((1, 8, HEAD_DIM), lambda b, h, pm, ps, hs: (b * MOBA_HEADS + h, 0, 0)),
                   pl.BlockSpec((1, 8, HEAD_DIM), lambda b, h, pm, ps, hs: (b, 0, 0)),
                   pl.BlockSpec((1, 8, HEAD_DIM), lambda b, h, pm, ps, hs: (b, 0, 0))],
    )
    return pl.pallas_call(
        functools.partial(_dec_attend_kernel, n_moba=n_moba, n_slc=n_slc, n_win=n_win),
        grid_spec=grid_spec,
        out_shape=[jax.ShapeDtypeStruct((B * MOBA_HEADS, 8, HEAD_DIM), F32),
                   jax.ShapeDtypeStruct((B, 8, HEAD_DIM), F32), jax.ShapeDtypeStruct((B, 8, HEAD_DIM), F32)],
        compiler_params=_cparams(("parallel", "arbitrary")),
        name="dec_attend",
    )(pages_moba, pages_slc, halves_slc, *([cache_kv] * (2 * n_moba + 2 * n_slc)), cache_win, cache_win,
      za_s, za_s, za_s, za_s, za_s, za_s, za_s, za_s, bias_moba, bias_slc, bias_win)


def _lru_step_kernel(x_ref, g_ref, c0_ref, h0_ref, cw_ref, cb_ref, wr_ref, br_ref, wi_ref, bi_ref, lam_ref,
                     y_ref, h_ref, cl_ref):
    B = h0_ref.shape[0]
    x = x_ref[0:B, :]
    u = cb_ref[...] + c0_ref[0] * cw_ref[0:1, :]
    u = u + c0_ref[1] * cw_ref[1:2, :]
    u = u + c0_ref[2] * cw_ref[2:3, :]
    u = u + x * cw_ref[3:4, :]
    ub = u.astype(BF16)
    r = jax.nn.sigmoid(jnp.dot(ub, wr_ref[0], preferred_element_type=F32) + br_ref[...])
    ig = jax.nn.sigmoid(jnp.dot(ub, wi_ref[0], preferred_element_type=F32) + bi_ref[...])
    log_a = (-LRU_C * r) * _softplus(-lam_ref[...])
    h = jnp.exp(log_a) * h0_ref[...] + jnp.sqrt(1.0 - jnp.exp(2.0 * log_a)) * (ig * u)
    h_ref[...] = h
    y_ref[...] = h * _gelu_tanh(g_ref[0:B, :])
    cl_ref[0] = c0_ref[1]
    cl_ref[1] = c0_ref[2]
    cl_ref[2] = x


def _lru_step(zb_s, conv0_t, h0, conv_w, conv_b, w_r, b_r, w_i, b_i, lam):
    B = h0.shape[0]
    nblk = LRU_BLOCKS
    vec = lambda a: a.reshape(1, LRU_WIDTH)
    row_spec = pl.BlockSpec((1, LRU_BLOCK), lambda n: (0, n))
    bw = lambda: pl.BlockSpec((B, LRU_BLOCK), lambda n: (0, n))
    return pl.pallas_call(
        _lru_step_kernel,
        grid=(nblk,),
        in_specs=[pl.BlockSpec((DEC_ROWS, LRU_BLOCK), lambda n: (0, n)),
                  pl.BlockSpec((DEC_ROWS, LRU_BLOCK), lambda n: (0, nblk + n)),
                  pl.BlockSpec((CONV_W - 1, B, LRU_BLOCK), lambda n: (0, 0, n)), bw(),
                  pl.BlockSpec((CONV_W, LRU_BLOCK), lambda n: (0, n)), row_spec,
                  pl.BlockSpec((1, LRU_BLOCK, LRU_BLOCK), lambda n: (n, 0, 0)), row_spec,
                  pl.BlockSpec((1, LRU_BLOCK, LRU_BLOCK), lambda n: (n, 0, 0)), row_spec, row_spec],
        out_specs=[bw(), bw(), pl.BlockSpec((CONV_W - 1, B, LRU_BLOCK), lambda n: (0, 0, n))],
        out_shape=[jax.ShapeDtypeStruct((B, LRU_WIDTH), F32), jax.ShapeDtypeStruct((B, LRU_WIDTH), F32),
                   jax.ShapeDtypeStruct((CONV_W - 1, B, LRU_WIDTH), F32)],
        compiler_params=_cparams(("parallel",)),
        name="lru_step",
    )(zb_s, zb_s, conv0_t, h0, conv_w, vec(conv_b), w_r, vec(b_r), w_i, vec(b_i), vec(lam))


def _bias_heads(tab, dist):
    bucket = _t5_bucket(dist)
    out = jnp.zeros(bucket.shape, F32)
    shape = (1, -1) + (1,) * (bucket.ndim - 2)
    for bkt in range(T5_BUCKETS):
        out = jnp.where(bucket == bkt, tab[:, bkt].reshape(shape), out)
    return out


def _decode_tables(t5_bias, past, n_win):
    tab_n = t5_bias.T[MOBA_HEADS:]
    n_ch = past // CMP_STRIDE
    n = jnp.arange(n_ch)
    bias_cmp = jnp.pad(_bias_lookup(tab_n, past - (n * CMP_STRIDE + CMP_LEN - 1)), ((0, 8 - NSA_HEADS), (0, 0)))
    s_start = jnp.arange(past // SLC_BLOCK) * SLC_BLOCK
    c_start = n * CMP_STRIDE
    overlap = ((c_start[:, None] < s_start[None, :] + SLC_BLOCK) & (c_start[:, None] + CMP_LEN > s_start[None, :])
               & (n[:, None] < n_ch - 1)).astype(BF16)
    d_win = jnp.concatenate([n_win - jnp.arange(n_win), jnp.zeros((LANE,), jnp.int32)])
    bias_win = jnp.pad(_bias_lookup(tab_n, d_win), ((0, 8 - NSA_HEADS), (0, 0)))
    return bias_cmp, overlap, bias_win


def _decode_attention(l, za_s, cache_kv, cache_win, page_table, lp_cmp, dtabs, t5_bias):
    B, n_pages = page_table.shape
    past = n_pages * PAGE_SIZE
    bias_cmp, overlap, bias_win = dtabs
    pe_k, pe_v, w_k, w_v = lp_cmp
    abk, abv, kmean = _page_scan(cache_kv, l, page_table, pe_k, pe_v, w_k, w_v)
    o_cmp, top_m, top_s = _dec_select(abk, abv, kmean, za_s, bias_cmp, overlap, past)
    top_m = top_m[:, :MOBA_HEADS, :MOBA_TOPK]
    top_s = top_s[:, 0, :SLC_TOPK - 1]
    tab = t5_bias.T
    bidx = jnp.arange(B)
    per_blk = MOBA_BLOCK // PAGE_SIZE
    pg_off = top_m[..., None] * per_blk + jnp.arange(per_blk)
    pages_moba = page_table[bidx[:, None, None, None], pg_off].reshape(-1)
    kpos = (pg_off.reshape(B, MOBA_HEADS, -1, 1) * PAGE_SIZE + jnp.arange(PAGE_SIZE))
    bm = _bias_heads(tab[:MOBA_HEADS], past - kpos)
    bm_new = jnp.zeros((B, MOBA_HEADS, 1, LANE), F32).at[..., 0].set(tab[None, :MOBA_HEADS, 0, None])
    bias_moba = jnp.concatenate([bm, bm_new, jnp.zeros((B, MOBA_HEADS, 1, LANE), F32)], axis=2)
    bias_moba = bias_moba.reshape(B * MOBA_HEADS, 8, LANE)
    per_page = PAGE_SIZE // SLC_BLOCK
    pages_slc = page_table[bidx[:, None], top_s // per_page].reshape(-1)
    halves_slc = (top_s % per_page).reshape(-1)
    spos = top_s[:, None, :, None] * SLC_BLOCK + jnp.arange(SLC_BLOCK)
    bs = _bias_heads(tab[MOBA_HEADS:], jnp.broadcast_to(past - spos, (B, NSA_HEADS, SLC_TOPK - 1, SLC_BLOCK)))
    bs_new = jnp.zeros((B, NSA_HEADS, 1, SLC_BLOCK), F32).at[..., 0].set(tab[None, MOBA_HEADS:, 0, None])
    bias_slc = jnp.concatenate([bs, bs_new], axis=2).transpose(0, 2, 1, 3)
    bias_slc = jnp.pad(bias_slc, ((0, 0), (0, 0), (0, 8 - NSA_HEADS), (0, 0)))
    o_a, o_slc, o_win = _dec_attend(cache_kv, cache_win, l, za_s, pages_moba, pages_slc, halves_slc,
                                    bias_moba, bias_slc, bias_win, B)
    pad = lambda a: jnp.pad(a, ((0, DEC_ROWS - B), (0, 0)))
    heads = lambda a: pad(a[:, :NSA_HEADS].reshape(B, NSA_W))
    return pad(o_a[:, 0].reshape(B, MOBA_W)), heads(o_cmp), heads(o_slc), heads(o_win)


def kernel(x_prompt, x_sample, cache_kv, cache_win, state_lru_h, state_lru_conv, page_table, c_prompt, c_sample,
           w_ada, b_ada, norm_mix, norm_ffn, w_in, cmp_pos_k, cmp_pos_v, w_cmp_k, w_cmp_v, conv_w, conv_b,
           w_rg, b_rg, w_ig, b_ig, lru_lambda, w_br_moba, w_br_nsa, w_br_lru, w_out, w_e_gate, w_e_up, w_e_down,
           t5_bias, w_router, b_router, norm_final):
    n_p, T, D = x_prompt.shape
    n_s = x_sample.shape[0]
    xp = x_prompt
    xs = x_sample.reshape(n_s, D)
    tabs = _prompt_tables(t5_bias, T)
    w_router_pad = jnp.pad(w_router, ((0, 0), (0, LANE - N_EXPERTS))).astype(BF16)
    b_router_pad = jnp.pad(b_router, (0, LANE - N_EXPERTS)).reshape(1, LANE)
    c_all = jax.nn.silu(jnp.concatenate([c_prompt, c_sample], axis=0))
    c_all = jnp.pad(c_all, ((0, 16 - n_p - n_s), (0, 0))).astype(BF16)
    dtabs = _decode_tables(t5_bias, page_table.shape[1] * PAGE_SIZE, cache_win.shape[2])
    moe_rows = LANE
    outs = [[] for _ in range(8)]
    for l in range(DEPTH):
        lp = {'l': l, 'norm_mix': norm_mix[l], 'norm_ffn': norm_ffn[l], 'w_in': w_in,
              'cmp_pos_k': cmp_pos_k[l], 'cmp_pos_v': cmp_pos_v[l],
              'w_cmp_k': w_cmp_k[l].astype(BF16), 'w_cmp_v': w_cmp_v[l].astype(BF16),
              'conv_w': conv_w[l], 'conv_b': conv_b[l], 'w_rg': w_rg[l].astype(BF16), 'b_rg': b_rg[l],
              'w_ig': w_ig[l].astype(BF16), 'b_ig': b_ig[l], 'lru_lambda': lru_lambda[l],
              'w_br_moba': w_br_moba[l].astype(BF16), 'w_br_nsa': w_br_nsa[l].astype(BF16),
              'w_br_lru': w_br_lru[l].astype(BF16), 'w_out': w_out[l].astype(BF16)}
        mod = _matmul_w32(c_all, w_ada, l, 0, 6 * D, 2048, 16) + b_ada[l]
        mod3 = mod.reshape(16 * 6, 1, D)
        mod_s = mod[n_p:n_p + n_s]
        xp, h2_p, lg_p, kv_p, win_p, h_p, conv_p = _mixer_prompt(xp, mod[:n_p], mod3, lp, tabs, w_router_pad,
                                                                 b_router_pad)
        xs, h2_s, lg_s, kv_s, win_s, h_s, conv_s = _mixer_decode(l, xs, mod_s, lp, cache_kv, cache_win, page_table,
                                                                 state_lru_h[l], state_lru_conv[l], dtabs, t5_bias,
                                                                 w_router_pad, b_router_pad)
        n_tok = n_p * T
        tail = ((0, moe_rows - h2_s.shape[0]), (0, 0))
        h2 = jnp.concatenate([h2_p, jnp.pad(h2_s, tail)], axis=0)
        logits = jnp.concatenate([lg_p, jnp.pad(lg_s, tail)], axis=0)[:, :N_EXPERTS]
        moe = _moe_prompt(h2, logits, l, w_e_gate, w_e_up, w_e_down)
        xp = xp.reshape(n_p, T, D) + mod[:n_p, None, 5 * D:] * moe[:n_tok].reshape(n_p, T, D)
        xs = xs + mod_s[:, 5 * D:] * moe[n_tok:n_tok + n_s]
        for lst, val in zip(outs, (kv_p, kv_s[:, None, :], win_p, win_s, h_p, h_s, conv_p, conv_s)):
            lst.append(val)
    y_prompt = _rms(xp, norm_final)
    y_sample = _rms(xs, norm_final).reshape(n_s, 1, D)
    return (y_prompt, y_sample) + tuple(jnp.stack(o) for o in outs)
```

```python
import functools
import math

import jax
import jax.numpy as jnp
from jax import lax
from jax.experimental import pallas as pl
from jax.experimental.pallas import tpu as pltpu

D_MODEL = 2048
DEPTH = 2
PAGE_SIZE = 128
HEAD_DIM = 128
MOBA_HEADS = 4
MOBA_BLOCK = 256
MOBA_TOPK = 3
NSA_HEADS = 4
CMP_STRIDE = 16
CMP_LEN = 32
SLC_BLOCK = 64
SLC_TOPK = 16
WINDOW = 512
LRU_WIDTH = D_MODEL // 2
LRU_BLOCKS = 8
LRU_BLOCK = LRU_WIDTH // LRU_BLOCKS
LRU_C = 8.0
CONV_W = 4
N_EXPERTS = 16
N_GROUPS = 4
EXPERTS_PER_GROUP = N_EXPERTS // N_GROUPS
TOP_K = 2
D_EXPERT = D_MODEL // 2
T5_BUCKETS = 32
T5_MAX_DIST = 128
EPS = 1e-6
NEG = -1e30

MOBA_W = MOBA_HEADS * HEAD_DIM
NSA_W = NSA_HEADS * HEAD_DIM
KV_DIM = 2 * MOBA_W + 4 * HEAD_DIM
WIN_DIM = 2 * HEAD_DIM
IN_WIDTHS = (MOBA_W, MOBA_W, MOBA_W, NSA_W, HEAD_DIM, HEAD_DIM, HEAD_DIM, HEAD_DIM, HEAD_DIM, HEAD_DIM,
             3 * NSA_HEADS, LRU_WIDTH, LRU_WIDTH, D_MODEL, D_MODEL, D_MODEL)
COL_NG = 3 * MOBA_W + NSA_W + 6 * HEAD_DIM
COL_LX = COL_NG + 3 * NSA_HEADS
ZA_TILE = 1024
ZA_W = -(-(COL_NG + 3 * NSA_HEADS) // ZA_TILE) * ZA_TILE
ZB_TILE = 1024
ZB_W = 2 * LRU_WIDTH + 3 * D_MODEL
SCALE = HEAD_DIM ** -0.5

LANE = 128
ATT_TILE = 512
MOE_TILE = 256
VMEM_LIMIT = 56 * 1024 * 1024

F32 = jnp.float32
BF16 = jnp.bfloat16


def _cparams(sem):
    return pltpu.CompilerParams(dimension_semantics=sem, vmem_limit_bytes=VMEM_LIMIT)


def _mm_kernel(x_ref, w_ref, o_ref):
    o_ref[...] = jnp.dot(x_ref[...].astype(BF16), w_ref[...].astype(BF16), preferred_element_type=F32)


def _matmul(x, w, tm, tn):
    M, K = x.shape
    N = w.shape[1]
    assert M % tm == 0 and N % tn == 0
    return pl.pallas_call(
        _mm_kernel,
        grid=(N // tn, M // tm),
        in_specs=[pl.BlockSpec((tm, K), lambda j, i: (i, 0)),
                  pl.BlockSpec((K, tn), lambda j, i: (0, j))],
        out_specs=pl.BlockSpec((tm, tn), lambda j, i: (i, j)),
        out_shape=jax.ShapeDtypeStruct((M, N), F32),
        compiler_params=_cparams(("parallel", "parallel")),
        name="matmul",
    )(x, w)


def _mm_w32_kernel(x_ref, w_ref, *rest, shift):
    if shift:
        wn_ref, o_ref, wb_scr = rest
    else:
        o_ref, wb_scr = rest

    @pl.when(pl.program_id(1) == 0)
    def _():
        if not shift:
            wb_scr[...] = w_ref[0].astype(BF16)
            return
        nblk = w_ref.shape[2] // LANE
        q, r = divmod(shift, LANE)

        def window_block(i):
            ref = w_ref if i < nblk else wn_ref
            i = i % nblk
            return ref[0, :, i * LANE:(i + 1) * LANE]

        for c in range(nblk):
            blk = window_block(q + c)
            if r:
                pair = jnp.concatenate([blk, window_block(q + c + 1)], axis=1)
                blk = pltpu.roll(pair, 2 * LANE - r, 1)[:, :LANE]
            wb_scr[:, c * LANE:(c + 1) * LANE] = blk.astype(BF16)

    o_ref[...] = jnp.dot(x_ref[...], wb_scr[...], preferred_element_type=F32)


def _matmul_w32(x, w3, l, col0, n_out, tn, tm):
    M, K = x.shape
    shift = col0 % tn
    blk0 = col0 // tn
    assert n_out % tn == 0 and M % tm == 0
    mode = {'pipeline_mode': pl.Buffered(1)} if shift else {}
    wspec = lambda off: pl.BlockSpec((1, K, tn), lambda j, i: (l, 0, blk0 + j + off), **mode)
    in_specs = [pl.BlockSpec((tm, K), lambda j, i: (i, 0)), wspec(0)]
    args = [x, w3]
    if shift:
        in_specs.append(wspec(1))
        args.append(w3)
    return pl.pallas_call(
        functools.partial(_mm_w32_kernel, shift=shift),
        grid=(n_out // tn, M // tm),
        in_specs=in_specs,
        out_specs=pl.BlockSpec((tm, tn), lambda j, i: (i, j)),
        out_shape=jax.ShapeDtypeStruct((M, n_out), F32),
        scratch_shapes=[pltpu.VMEM((K, tn), BF16)],
        compiler_params=_cparams(("parallel", "arbitrary")),
        name="matmul_w32",
    )(*args)


def _merge_kernel(oa_ref, ocmp_ref, oslc_ref, owin_ref, ng_ref, oc_ref, ga_ref, gb_ref, gc_ref,
                  wa_ref, wb_ref, wc_ref, o_ref):
    gt = jax.nn.sigmoid(ng_ref[...])
    heads = []
    for g in range(NSA_HEADS):
        sl = slice(g * HEAD_DIM, (g + 1) * HEAD_DIM)
        heads.append(gt[:, 3 * g:3 * g + 1] * ocmp_ref[:, sl] + gt[:, 3 * g + 1:3 * g + 2] * oslc_ref[:, sl]
                     + gt[:, 3 * g + 2:3 * g + 3] * owin_ref[:, sl])
    ob = jnp.concatenate(heads, axis=1).astype(BF16)
    pa = jnp.dot(oa_ref[...].astype(BF16), wa_ref[...], preferred_element_type=F32)
    pb = jnp.dot(ob, wb_ref[...], preferred_element_type=F32)
    pc = jnp.dot(oc_ref[...].astype(BF16), wc_ref[...], preferred_element_type=F32)
    merged = (jax.nn.sigmoid(ga_ref[...]) * pa + jax.nn.sigmoid(gb_ref[...]) * pb
              + jax.nn.sigmoid(gc_ref[...]) * pc)
    o_ref[...] = merged.astype(BF16)


def _merge(o_a, o_cmp, o_slc, o_win, za, o_c, zb, w_a, w_b, w_c, tm):
    N = o_a.shape[0]
    D = D_MODEL
    row = lambda w, c: pl.BlockSpec((tm, w), lambda i: (i, c))
    full = lambda a: pl.BlockSpec(a.shape, lambda i: (0, 0))
    return pl.pallas_call(
        _merge_kernel,
        grid=(N // tm,),
        in_specs=[row(MOBA_W, 0), row(NSA_W, 0), row(NSA_W, 0), row(NSA_W, 0), row(LANE, COL_NG // LANE),
                  row(LRU_WIDTH, 0), row(D, 1), row(D, 2), row(D, 3), full(w_a), full(w_b), full(w_c)],
        out_specs=pl.BlockSpec((tm, D), lambda i: (i, 0)),
        out_shape=jax.ShapeDtypeStruct((N, D), BF16),
        compiler_params=_cparams(("parallel",)),
        name="merge",
    )(o_a, o_cmp, o_slc, o_win, za, o_c, zb, zb, zb, w_a, w_b, w_c)


def _outproj_kernel(m_ref, w_ref, x_ref, g1_ref, sc2_ref, sh2_ref, nf_ref, wr_ref, br_ref,
                    xo_ref, h2_ref, lg_ref):
    y = jnp.dot(m_ref[...], w_ref[...], preferred_element_type=F32)
    x = x_ref[...] + g1_ref[0] * y
    xo_ref[...] = x
    h2 = x * lax.rsqrt(jnp.mean(x * x, axis=-1, keepdims=True) + EPS) * nf_ref[...]
    h2 = (h2 * (1.0 + sc2_ref[0]) + sh2_ref[0]).astype(BF16)
    h2_ref[...] = h2
    lg_ref[...] = jnp.dot(h2, wr_ref[...], preferred_element_type=F32) + br_ref[...]


def _outproj(merged, w_out, x, mod3, norm_ffn, w_router_pad, b_router_pad, tm, rows_per_mod, mod_base):
    N, D = x.shape
    R = mod3.shape[1]
    mspec = lambda k: pl.BlockSpec((1, R, D), lambda i: ((mod_base + (i * tm) // rows_per_mod) * 6 + k, 0, 0))
    full = lambda a: pl.BlockSpec(a.shape, lambda i: (0, 0))
    return pl.pallas_call(
        _outproj_kernel,
        grid=(N // tm,),
        in_specs=[pl.BlockSpec((tm, D), lambda i: (i, 0)), full(w_out), pl.BlockSpec((tm, D), lambda i: (i, 0)),
                  mspec(2), mspec(4), mspec(3), full(norm_ffn), full(w_router_pad), full(b_router_pad)],
        out_specs=[pl.BlockSpec((tm, D), lambda i: (i, 0)), pl.BlockSpec((tm, D), lambda i: (i, 0)),
                   pl.BlockSpec((tm, LANE), lambda i: (i, 0))],
        out_shape=[jax.ShapeDtypeStruct((N, D), F32), jax.ShapeDtypeStruct((N, D), BF16),
                   jax.ShapeDtypeStruct((N, LANE), F32)],
        compiler_params=_cparams(("parallel",)),
        name="outproj",
    )(merged, w_out, x, mod3, mod3, mod3, norm_ffn, w_router_pad, b_router_pad)


def _ffn_residual_kernel(x_ref, a_ref, b_ref, g2_ref, gain_ref, sc_ref, sh_ref, *out_refs, last):
    x = x_ref[...] + g2_ref[0] * (a_ref[...] + b_ref[...])
    y = x * lax.rsqrt(jnp.mean(x * x, axis=-1, keepdims=True) + EPS) * gain_ref[...]
    if last:
        out_refs[0][...] = y
    else:
        out_refs[0][...] = x
        out_refs[1][...] = (y * (1.0 + sc_ref[0]) + sh_ref[0]).astype(BF16)


def _ffn_residual(x, a, b, mod3, mod3_next, gain, tm, rows_per_mod, last):
    N, D = x.shape
    mrow = lambda k: pl.BlockSpec((1, 1, D), lambda i: (((i * tm) // rows_per_mod) * 6 + k, 0, 0))
    tile = pl.BlockSpec((tm, D), lambda i: (i, 0))
    out_shape = [jax.ShapeDtypeStruct((N, D), F32)] + ([] if last else [jax.ShapeDtypeStruct((N, D), BF16)])
    return pl.pallas_call(
        functools.partial(_ffn_residual_kernel, last=last),
        grid=(N // tm,),
        in_specs=[tile, tile, tile, mrow(5), pl.BlockSpec((1, D), lambda i: (0, 0)), mrow(1), mrow(0)],
        out_specs=[tile] * len(out_shape),
        out_shape=out_shape,
        compiler_params=_cparams(("parallel",)),
        name="ffn_residual",
    )(x, a, b, mod3, gain.reshape(1, D), mod3_next, mod3_next)


def _rank_select(score, own, nblk, topk):
    lane = lax.broadcasted_iota(jnp.int32, score.shape, 1)
    past = lane < own
    s = jnp.where(past, score, NEG)
    rank = jnp.zeros(score.shape, jnp.int32)
    for j in range(nblk):
        col = s[:, j:j + 1]
        beats = (col > s) | ((col == s) & (lane > j))
        rank = rank + beats.astype(jnp.int32)
    return (past & (rank < topk)) | (lane == own)


def _moba_select_kernel(q_ref, k_ref, sel_ref):
    T = q_ref.shape[1]
    nb = T // MOBA_BLOCK
    k = k_ref[0]
    kmean = jnp.sum(k.reshape(nb, MOBA_BLOCK, HEAD_DIM), axis=1) * (1.0 / MOBA_BLOCK)
    kmean = jnp.concatenate([kmean, jnp.zeros((LANE - nb, HEAD_DIM), F32)], axis=0)
    score = lax.dot_general(q_ref[0].astype(BF16), kmean.astype(BF16), (((1,), (1,)), ((), ())),
                            preferred_element_type=F32)
    own = lax.broadcasted_iota(jnp.int32, (T, 1), 0) // MOBA_BLOCK
    sel = _rank_select(score, own, nb, MOBA_TOPK)
    sel_ref[0, 0] = sel.astype(BF16)


def _moba_select(za, B, T):
    return pl.pallas_call(
        _moba_select_kernel,
        grid=(B, MOBA_HEADS),
        in_specs=[pl.BlockSpec((1, T, HEAD_DIM), lambda b, h: (b, 0, h)),
                  pl.BlockSpec((1, T, HEAD_DIM), lambda b, h: (b, 0, MOBA_HEADS + h))],
        out_specs=pl.BlockSpec((1, 1, T, LANE), lambda b, h: (b, h, 0, 0)),
        out_shape=jax.ShapeDtypeStruct((B, MOBA_HEADS, T, LANE), BF16),
        compiler_params=_cparams(("parallel", "parallel")),
        name="moba_select",
    )(za, za)


def _flash_kernel(*refs, G, window_tiles, use_sel, per_head_kv):
    if use_sel:
        q_ref, k_ref, v_ref, bias_ref, sel_ref, e_ref, o_ref, m_scr, l_scr, acc_scr = refs
    else:
        q_ref, k_ref, v_ref, bias_ref, o_ref, m_scr, l_scr, acc_scr = refs
    qi = pl.program_id(1)
    kj = pl.program_id(2)
    nk = pl.num_programs(2)
    tq, tk = ATT_TILE, ATT_TILE

    @pl.when(kj == 0)
    def _():
        m_scr[...] = jnp.full(m_scr.shape, NEG, F32)
        l_scr[...] = jnp.zeros(l_scr.shape, F32)
        acc_scr[...] = jnp.zeros(acc_scr.shape, F32)

    lo = jnp.maximum(qi - window_tiles, 0) if window_tiles is not None else 0

    @pl.when((kj >= lo) & (kj <= qi))
    def _():
        row = lax.broadcasted_iota(jnp.int32, (tq, tk), 0)
        col = lax.broadcasted_iota(jnp.int32, (tq, tk), 1)
        dist = (qi - kj) * tq + row - col
        band = dist >= 0
        if window_tiles is not None:
            band = band & (dist < WINDOW)
        mask = band
        for g in range(G):
            kv = slice(g * HEAD_DIM, (g + 1) * HEAD_DIM) if per_head_kv else slice(0, HEAD_DIM)
            if g == 0 or per_head_kv:
                k = k_ref[0, :, kv].astype(BF16)
                v = v_ref[0, :, kv].astype(BF16)
                if use_sel:
                    hit = jnp.dot(sel_ref[0, g], e_ref[...], preferred_element_type=F32)
                    mask = band & (hit > 0.5)
            q = q_ref[0, :, g * HEAD_DIM:(g + 1) * HEAD_DIM].astype(BF16)
            s = lax.dot_general(q, k, (((1,), (1,)), ((), ())), preferred_element_type=F32)
            s = s * SCALE + bias_ref[g, 0]
            s = jnp.where(mask, s, NEG)
            m_prev = m_scr[g]
            m_new = jnp.maximum(m_prev, jnp.max(s, axis=1, keepdims=True))
            p = jnp.exp(s - m_new)
            alpha = jnp.exp(m_prev - m_new)
            l_scr[g] = alpha * l_scr[g] + jnp.sum(p, axis=1, keepdims=True)
            acc_scr[g] = alpha * acc_scr[g] + jnp.dot(p.astype(BF16), v, preferred_element_type=F32)
            m_scr[g] = m_new

    @pl.when(kj == nk - 1)
    def _():
        for g in range(G):
            l = l_scr[g]
            o = acc_scr[g] / jnp.where(l > 0.0, l, 1.0)
            o_ref[0, :, g * HEAD_DIM:(g + 1) * HEAD_DIM] = o


def _flash(za, bias, *, B, T, q_col, k_col, v_col, bias_blk, per_head_kv, window_tiles=None, sel=None, emat=None):
    tq = tk = ATT_TILE
    nq, nk = T // tq, T // tk
    G = NSA_HEADS
    use_sel = sel is not None
    kv_w = G * HEAD_DIM if per_head_kv else HEAD_DIM

    def kv_blk(i, j):
        lo = jnp.maximum(i - window_tiles, 0) if window_tiles is not None else 0
        return jnp.clip(j, lo, i)

    in_specs = [
        pl.BlockSpec((1, tq, G * HEAD_DIM), lambda b, i, j: (b, i, q_col)),
        pl.BlockSpec((1, tk, kv_w), lambda b, i, j: (b, kv_blk(i, j), k_col)),
        pl.BlockSpec((1, tk, kv_w), lambda b, i, j: (b, kv_blk(i, j), v_col)),
        pl.BlockSpec((G, 1, tq, tk), lambda b, i, j: (bias_blk, jnp.clip(i - j, 0, 2), 0, 0)),
    ]
    args = [za, za, za, bias]
    if use_sel:
        in_specs += [pl.BlockSpec((1, sel.shape[1], tq, LANE), lambda b, i, j: (b, 0, i, 0)),
                     pl.BlockSpec((LANE, tk), lambda b, i, j: (0, kv_blk(i, j)))]
        args += [sel, emat]
    return pl.pallas_call(
        functools.partial(_flash_kernel, G=G, window_tiles=window_tiles, use_sel=use_sel, per_head_kv=per_head_kv),
        grid=(B, nq, nk),
        in_specs=in_specs,
        out_specs=pl.BlockSpec((1, tq, G * HEAD_DIM), lambda b, i, j: (b, i, 0)),
        out_shape=jax.ShapeDtypeStruct((B, T, G * HEAD_DIM), F32),
        scratch_shapes=[pltpu.VMEM((G, tq, 1), F32), pltpu.VMEM((G, tq, 1), F32),
                        pltpu.VMEM((G, tq, HEAD_DIM), F32)],
        compiler_params=_cparams(("parallel", "parallel", "arbitrary")),
        name="flash",
    )(*args)


def _cmp_kernel(ck_ref, cv_ref, q_ref, pek_ref, pev_ref, wk_ref, wv_ref, bias_ref, ovl_ref,
                o_ref, sel_ref, kc_scr, vc_scr):
    T = ck_ref.shape[1]
    n_ch = T // CMP_STRIDE
    i = pl.program_id(1)
    tq = ATT_TILE

    @pl.when(i == 0)
    def _():
        for src, pe, w, dst in ((ck_ref, pek_ref, wk_ref, kc_scr), (cv_ref, pev_ref, wv_ref, vc_scr)):
            first = jnp.zeros((n_ch, HEAD_DIM), F32)
            second = jnp.zeros((n_ch, HEAD_DIM), F32)
            for l in range(CMP_STRIDE):
                x = src[0, pl.ds(l, n_ch, stride=CMP_STRIDE), :]
                l2 = CMP_STRIDE + l
                first = first + jnp.dot((x + pe[l:l + 1, :]).astype(BF16), w[l * HEAD_DIM:(l + 1) * HEAD_DIM, :],
                                        preferred_element_type=F32)
                second = second + jnp.dot((x + pe[l2:l2 + 1, :]).astype(BF16),
                                          w[l2 * HEAD_DIM:(l2 + 1) * HEAD_DIM, :], preferred_element_type=F32)
            dst[...] = (first + pltpu.roll(second, n_ch - 1, 0)).astype(BF16)

    kc = kc_scr[...]
    vc = vc_scr[...]
    t = i * tq + lax.broadcasted_iota(jnp.int32, (tq, n_ch), 0)
    n = lax.broadcasted_iota(jnp.int32, (tq, n_ch), 1)
    vis = (n * CMP_STRIDE + (CMP_LEN - 1)) <= t
    imp = jnp.zeros((tq, LANE), F32)
    ovl = ovl_ref[...]
    for g in range(NSA_HEADS):
        q = q_ref[0, :, g * HEAD_DIM:(g + 1) * HEAD_DIM].astype(BF16)
        s = lax.dot_general(q, kc, (((1,), (1,)), ((), ())), preferred_element_type=F32)
        s = s * SCALE + bias_ref[g]
        s = jnp.where(vis, s, NEG)
        m = jnp.max(s, axis=1, keepdims=True)
        e = jnp.where(vis, jnp.exp(s - m), 0.0)
        l = jnp.sum(e, axis=1, keepdims=True)
        p = e / jnp.where(l > 0.0, l, 1.0)
        pb = p.astype(BF16)
        o_ref[0, :, g * HEAD_DIM:(g + 1) * HEAD_DIM] = jnp.dot(pb, vc, preferred_element_type=F32)
        imp = imp + jnp.dot(pb, ovl, preferred_element_type=F32)
    own = (i * tq + lax.broadcasted_iota(jnp.int32, (tq, 1), 0)) // SLC_BLOCK
    sel = _rank_select(imp, own, T // SLC_BLOCK, SLC_TOPK - 1)
    sel_ref[0, 0] = sel.astype(BF16)


def _cmp_branch(za, pe_k, pe_v, w_k, w_v, bias_cmp, overlap, B, T):
    assert T // CMP_STRIDE == LANE
    tq = ATT_TILE
    col = 3 * MOBA_HEADS + NSA_HEADS
    return pl.pallas_call(
        _cmp_kernel,
        grid=(B, T // tq),
        in_specs=[
            pl.BlockSpec((1, T, HEAD_DIM), lambda b, i: (b, 0, col)),
            pl.BlockSpec((1, T, HEAD_DIM), lambda b, i: (b, 0, col + 1)),
            pl.BlockSpec((1, tq, NSA_W), lambda b, i: (b, i, 3)),
            pl.BlockSpec((CMP_LEN, HEAD_DIM), lambda b, i: (0, 0)),
            pl.BlockSpec((CMP_LEN, HEAD_DIM), lambda b, i: (0, 0)),
            pl.BlockSpec((CMP_LEN * HEAD_DIM, HEAD_DIM), lambda b, i: (0, 0)),
            pl.BlockSpec((CMP_LEN * HEAD_DIM, HEAD_DIM), lambda b, i: (0, 0)),
            pl.BlockSpec((NSA_HEADS, tq, LANE), lambda b, i: (0, i, 0)),
            pl.BlockSpec((LANE, LANE), lambda b, i: (0, 0)),
        ],
        out_specs=[pl.BlockSpec((1, tq, NSA_W), lambda b, i: (b, i, 0)),
                   pl.BlockSpec((1, 1, tq, LANE), lambda b, i: (b, 0, i, 0))],
        out_shape=[jax.ShapeDtypeStruct((B, T, NSA_W), F32),
                   jax.ShapeDtypeStruct((B, 1, T, LANE), BF16)],
        scratch_shapes=[pltpu.VMEM((LANE, HEAD_DIM), BF16), pltpu.VMEM((LANE, HEAD_DIM), BF16)],
        compiler_params=_cparams(("parallel", "arbitrary")),
        name="nsa_cmp",
    )(za, za, za, pe_k, pe_v, w_k, w_v, bias_cmp, overlap)


def _gelu_tanh(x):
    return 0.5 * x * (1.0 + jnp.tanh(math.sqrt(2.0 / math.pi) * (x + 0.044715 * (x * x * x))))


def _softplus(x):
    return jnp.maximum(x, 0.0) + jnp.log1p(jnp.exp(-jnp.abs(x)))


def _lru_kernel(x_ref, g_ref, c0_ref, h0_ref, cw_ref, cb_ref, wr_ref, br_ref, wi_ref, bi_ref, lam_ref,
                y_ref, hl_ref, cl_ref, a_scr, x_scr):
    T = x_ref.shape[1]
    x = x_ref[0]
    row = lax.broadcasted_iota(jnp.int32, (T, LRU_BLOCK), 0)
    c0 = c0_ref[0]

    def shifted(d):
        r = pltpu.roll(x, d, 0)
        for t in range(d):
            r = jnp.where(row == t, c0[t + CONV_W - 1 - d:t + CONV_W - d, :], r)
        return r

    u = cb_ref[...] + shifted(3) * cw_ref[0:1, :]
    u = u + shifted(2) * cw_ref[1:2, :]
    u = u + shifted(1) * cw_ref[2:3, :]
    u = u + x * cw_ref[3:4, :]
    ub = u.astype(BF16)
    r = jax.nn.sigmoid(jnp.dot(ub, wr_ref[0], preferred_element_type=F32) + br_ref[...])
    ig = jax.nn.sigmoid(jnp.dot(ub, wi_ref[0], preferred_element_type=F32) + bi_ref[...])
    log_a = (-LRU_C * r) * _softplus(-lam_ref[...])
    a = jnp.exp(log_a)
    b = jnp.sqrt(1.0 - jnp.exp(2.0 * log_a)) * (ig * u)
    sub = row % 8
    for d in (1, 2, 4):
        ok = sub >= d
        b = jnp.where(ok, a * pltpu.roll(b, d, 0) + b, b)
        a = jnp.where(ok, a * pltpu.roll(a, d, 0), a)
    a_scr[...] = a
    x_scr[...] = b

    def group(k, carry):
        s = pl.multiple_of(k * 8, 8)
        h = x_scr[pl.ds(s, 8), :] + a_scr[pl.ds(s, 8), :] * carry
        x_scr[pl.ds(s, 8), :] = h
        return h[7:8, :]

    h_last = lax.fori_loop(0, T // 8, group, h0_ref[0], unroll=8)
    y_ref[0] = x_scr[...] * _gelu_tanh(g_ref[0])
    hl_ref[0] = h_last
    cl_ref[0] = x[T - (CONV_W - 1):, :]


def _rglru(zb, conv0, h0, conv_w, conv_b, w_r, b_r, w_i, b_i, lam, B, T):
    nblk = LRU_BLOCKS
    vec = lambda a: a.reshape(1, LRU_WIDTH)
    row_spec = pl.BlockSpec((1, LRU_BLOCK), lambda b, n: (0, n))
    return pl.pallas_call(
        _lru_kernel,
        grid=(B, nblk),
        in_specs=[
            pl.BlockSpec((1, T, LRU_BLOCK), lambda b, n: (b, 0, n)),
            pl.BlockSpec((1, T, LRU_BLOCK), lambda b, n: (b, 0, nblk + n)),
            pl.BlockSpec((1, CONV_W - 1, LRU_BLOCK), lambda b, n: (b, 0, n)),
            pl.BlockSpec((1, 1, LRU_BLOCK), lambda b, n: (b, 0, n)),
            pl.BlockSpec((CONV_W, LRU_BLOCK), lambda b, n: (0, n)),
            row_spec,
            pl.BlockSpec((1, LRU_BLOCK, LRU_BLOCK), lambda b, n: (n, 0, 0)),
            row_spec,
            pl.BlockSpec((1, LRU_BLOCK, LRU_BLOCK), lambda b, n: (n, 0, 0)),
            row_spec,
            row_spec,
        ],
        out_specs=[pl.BlockSpec((1, T, LRU_BLOCK), lambda b, n: (b, 0, n)),
                   pl.BlockSpec((1, 1, LRU_BLOCK), lambda b, n: (b, 0, n)),
                   pl.BlockSpec((1, CONV_W - 1, LRU_BLOCK), lambda b, n: (b, 0, n))],
        out_shape=[jax.ShapeDtypeStruct((B, T, LRU_WIDTH), F32),
                   jax.ShapeDtypeStruct((B, 1, LRU_WIDTH), F32),
                   jax.ShapeDtypeStruct((B, CONV_W - 1, LRU_WIDTH), F32)],
        scratch_shapes=[pltpu.VMEM((T, LRU_BLOCK), F32), pltpu.VMEM((T, LRU_BLOCK), F32)],
        compiler_params=_cparams(("parallel", "parallel")),
        name="rglru",
    )(zb, zb, conv0, h0.reshape(B, 1, LRU_WIDTH), conv_w, vec(conv_b), w_r.astype(BF16), vec(b_r),
      w_i.astype(BF16), vec(b_i), vec(lam))


def _moe_kernel(eid_ref, nt_ref, x_ref, wg_ref, wu_ref, wd_ref, cw_ref, o_ref, wg_scr, wu_scr, wd_scr):
    t = pl.program_id(0)
    live = t < nt_ref[0]
    new_expert = (t == 0) | (eid_ref[t] != eid_ref[jnp.maximum(t - 1, 0)])

    @pl.when(live & new_expert)
    def _():
        wg_scr[...] = wg_ref[0, 0].astype(BF16)
        wu_scr[...] = wu_ref[0, 0].astype(BF16)
        wd_scr[...] = wd_ref[0, 0].astype(BF16)

    @pl.when(live)
    def _():
        x = x_ref[...]
        g = jnp.dot(x, wg_scr[...], preferred_element_type=F32)
        u = jnp.dot(x, wu_scr[...], preferred_element_type=F32)
        act = (g * jax.nn.sigmoid(g)) * u
        y = jnp.dot(act.astype(BF16), wd_scr[...], preferred_element_type=F32)
        o_ref[...] = y * cw_ref[...]

    @pl.when(t >= nt_ref[0])
    def _():
        o_ref[...] = jnp.zeros(o_ref.shape, F32)


def _moe_ffn(x_pad, cw_pad, tile_eid, n_tiles, l, w_g, w_u, w_d):
    P, D = x_pad.shape
    tm = MOE_TILE
    grid_spec = pltpu.PrefetchScalarGridSpec(
        num_scalar_prefetch=2,
        grid=(P // tm,),
        in_specs=[
            pl.BlockSpec((tm, D), lambda t, eid, nt: (t, 0)),
            pl.BlockSpec((1, 1, D, D_EXPERT), lambda t, eid, nt: (l, eid[t], 0, 0)),
            pl.BlockSpec((1, 1, D, D_EXPERT), lambda t, eid, nt: (l, eid[t], 0, 0), pipeline_mode=pl.Buffered(1)),
            pl.BlockSpec((1, 1, D_EXPERT, D), lambda t, eid, nt: (l, eid[t], 0, 0), pipeline_mode=pl.Buffered(1)),
            pl.BlockSpec((tm, 1), lambda t, eid, nt: (t, 0)),
        ],
        out_specs=pl.BlockSpec((tm, D), lambda t, eid, nt: (t, 0)),
        scratch_shapes=[pltpu.VMEM((D, D_EXPERT), BF16), pltpu.VMEM((D, D_EXPERT), BF16),
                        pltpu.VMEM((D_EXPERT, D), BF16)],
    )
    return pl.pallas_call(
        _moe_kernel,
        grid_spec=grid_spec,
        out_shape=jax.ShapeDtypeStruct((P, D), F32),
        compiler_params=_cparams(("arbitrary",)),
        name="moe_ffn",
    )(tile_eid, n_tiles, x_pad, w_g, w_u, w_d, cw_pad)


def _route(logits):
    probs = jax.nn.softmax(logits.astype(F32), axis=-1)
    grp = probs.reshape(-1, N_GROUPS, EXPERTS_PER_GROUP)
    pairs = [grp[..., a] + grp[..., b] for a in range(EXPERTS_PER_GROUP) for b in range(a + 1, EXPERTS_PER_GROUP)]
    g_score = functools.reduce(jnp.maximum, pairs)
    g_best = jnp.argmax(g_score, axis=-1)
    pick = g_best[:, None, None] == jnp.arange(N_GROUPS)[None, :, None]
    in_grp = jnp.sum(jnp.where(pick, grp, 0.0), axis=1)
    i1 = jnp.argmax(in_grp, axis=-1)
    first = jnp.arange(EXPERTS_PER_GROUP)[None, :] == i1[:, None]
    w1 = jnp.max(in_grp, axis=-1)
    rest = jnp.where(first, -jnp.inf, in_grp)
    i2 = jnp.argmax(rest, axis=-1)
    w2 = jnp.max(rest, axis=-1)
    w_top = jnp.stack([w1, w2], axis=-1)
    w_top = w_top / jnp.sum(w_top, axis=-1, keepdims=True)
    e_idx = g_best[:, None] * EXPERTS_PER_GROUP + jnp.stack([i1, i2], axis=-1)
    return e_idx.astype(jnp.int32), w_top


def _moe_prompt(h2, logits, l, w_g, w_u, w_d):
    N, D = h2.shape
    tm = MOE_TILE
    e_idx, w_top = _route(logits)
    flat_e = e_idx.reshape(-1)
    onehot = (flat_e[:, None] == jnp.arange(N_EXPERTS)[None, :]).astype(jnp.int32)
    within = jnp.sum(onehot * jnp.cumsum(onehot, axis=0), axis=1) - 1
    sizes = jnp.sum(onehot, axis=0)
    padded = ((sizes + tm - 1) // tm) * tm
    pend = jnp.cumsum(padded)
    pstart = pend - padded
    dest = jnp.sum(onehot * pstart[None, :], axis=1) + within
    P = 2 * N + N_EXPERTS * tm
    src_tok = jnp.zeros((P,), jnp.int32).at[dest].set(jnp.arange(2 * N, dtype=jnp.int32) // TOP_K)
    cw_pad = jnp.zeros((P,), F32).at[dest].set(w_top.reshape(-1))
    x_pad = h2[src_tok]
    tile_start = jnp.arange(P // tm, dtype=jnp.int32) * tm
    tile_eid = jnp.minimum(jnp.sum(tile_start[:, None] >= pend[None, :], axis=1), N_EXPERTS - 1).astype(jnp.int32)
    n_tiles = (pend[-1] // tm).astype(jnp.int32).reshape(1)
    out = _moe_ffn(x_pad, cw_pad.reshape(P, 1), tile_eid, n_tiles, l, w_g, w_u, w_d)
    pos = dest.reshape(N, TOP_K)
    return out[pos[:, 0]], out[pos[:, 1]]


def _t5_bucket(dist):
    n = jnp.maximum(dist, 0)
    exact = T5_BUCKETS // 2
    nf = jnp.maximum(n, 1).astype(F32)
    large = exact + (jnp.log(nf / exact) / math.log(T5_MAX_DIST / exact) * (T5_BUCKETS - exact)).astype(jnp.int32)
    return jnp.where(n < exact, n, jnp.minimum(large, T5_BUCKETS - 1))


def _bias_lookup(tab, dist):
    bucket = _t5_bucket(dist)
    out = jnp.zeros((tab.shape[0],) + bucket.shape, F32)
    for b in range(T5_BUCKETS):
        out = jnp.where(bucket[None] == b, tab[:, b].reshape((-1,) + (1,) * bucket.ndim), out)
    return out


def _rms(x, g):
    return x * lax.rsqrt(jnp.mean(x * x, axis=-1, keepdims=True) + EPS) * g


def _split(z, widths):
    outs, s = [], 0
    for w in widths:
        outs.append(z[..., s:s + w])
        s += w
    return outs


def _prompt_tables(t5_bias, T):
    t = ATT_TILE
    i = jnp.arange(t)
    dist = jnp.arange(3)[:, None, None] * t + i[None, :, None] - i[None, None, :]
    bias_tiles = _bias_lookup(t5_bias.T, dist)
    pos = jnp.arange(T)
    lane = jnp.arange(LANE)
    e_moba = (pos[None, :] // MOBA_BLOCK == lane[:, None]).astype(BF16)
    e_slc = (pos[None, :] // SLC_BLOCK == lane[:, None]).astype(BF16)
    d_cmp = pos[:, None] - (lane[None, :] * CMP_STRIDE + CMP_LEN - 1)
    bias_cmp = _bias_lookup(t5_bias.T[MOBA_HEADS:], d_cmp)
    c_start = lane * CMP_STRIDE
    s_start = lane * SLC_BLOCK
    n_cmp = T // CMP_STRIDE - 1
    overlap = ((c_start[:, None] < s_start[None, :] + SLC_BLOCK) & (c_start[:, None] + CMP_LEN > s_start[None, :])
               & (lane[:, None] < n_cmp) & (lane[None, :] < T // SLC_BLOCK)).astype(BF16)
    return bias_tiles, e_moba, e_slc, bias_cmp, overlap


def _mixer_prompt(x, h, B, T, mod3, lp, tabs, w_router_pad, b_router_pad):
    N, D = x.shape
    bias_tiles, e_moba, e_slc, bias_cmp, overlap = tabs
    za = _matmul_w32(h, lp['w_in'], lp['l'], 0, ZA_W, ZA_TILE, 1024).reshape(B, T, ZA_W)
    zb = _matmul_w32(h, lp['w_in'], lp['l'], COL_LX, ZB_W, ZB_TILE, 1024).reshape(B, T, ZB_W)
    new_kv = jnp.concatenate([za[..., MOBA_W:3 * MOBA_W], za[..., 3 * MOBA_W + NSA_W:COL_NG - WIN_DIM]], axis=-1)
    win = za[:, T - WINDOW:, COL_NG - WIN_DIM:COL_NG]

    sel_moba = _moba_select(za, B, T)
    o_a = _flash(za, bias_tiles, B=B, T=T, q_col=0, k_col=1, v_col=2, bias_blk=0, per_head_kv=True,
                 sel=sel_moba, emat=e_moba)
    o_cmp, sel_slc = _cmp_branch(za, lp['cmp_pos_k'], lp['cmp_pos_v'], lp['w_cmp_k'].reshape(-1, HEAD_DIM),
                                 lp['w_cmp_v'].reshape(-1, HEAD_DIM), bias_cmp, overlap, B, T)
    o_slc = _flash(za, bias_tiles, B=B, T=T, q_col=3, k_col=18, v_col=19, bias_blk=1, per_head_kv=False,
                   sel=sel_slc, emat=e_slc)
    o_win = _flash(za, bias_tiles, B=B, T=T, q_col=3, k_col=20, v_col=21, bias_blk=1, per_head_kv=False,
                   window_tiles=WINDOW // ATT_TILE)
    o_c, h_last, conv_last = _rglru(zb, jnp.zeros((B, CONV_W - 1, LRU_WIDTH), F32), jnp.zeros((B, LRU_WIDTH), F32),
                                    lp['conv_w'], lp['conv_b'], lp['w_rg'], lp['b_rg'], lp['w_ig'], lp['b_ig'],
                                    lp['lru_lambda'], B, T)
    flat = lambda a: a.reshape(N, a.shape[-1])
    merged = _merge(flat(o_a), flat(o_cmp), flat(o_slc), flat(o_win), flat(za), flat(o_c), flat(zb),
                    lp['w_br_moba'], lp['w_br_nsa'], lp['w_br_lru'], 256)
    x, h2, logits = _outproj(merged, lp['w_out'], x, mod3, lp['norm_ffn'].reshape(1, D),
                             w_router_pad, b_router_pad, 256, T, 0)
    return x, h2, logits, new_kv, win, h_last.reshape(B, LRU_WIDTH), conv_last


def _mixer_decode(l, x, mod, lp, cache_kv, cache_win, page_table, h0, conv0, dtabs, t5_bias, w_router_pad,
                  b_router_pad):
    B, D = x.shape
    pad = lambda a: jnp.pad(a, ((0, DEC_ROWS - B),) + ((0, 0),) * (a.ndim - 1))
    sh1, sc1 = mod[:, :D], mod[:, D:2 * D]
    h = pad((_rms(x, lp['norm_mix']) * (1.0 + sc1) + sh1).astype(BF16))
    za = _matmul_w32(h, lp['w_in'], l, 0, ZA_W, ZA_TILE, DEC_ROWS)
    zb = _matmul_w32(h, lp['w_in'], l, COL_LX, ZB_W, ZB_TILE, DEC_ROWS)
    new_kv = jnp.concatenate([za[:B, MOBA_W:3 * MOBA_W], za[:B, 3 * MOBA_W + NSA_W:COL_NG - WIN_DIM]], axis=-1)
    n_win = cache_win.shape[2]
    win = jnp.concatenate([cache_win[l], za[:B, None, COL_NG - WIN_DIM:COL_NG]], axis=1)
    win = win[:, win.shape[1] - min(WINDOW, n_win + 1):]
    cmp_w = (lp['cmp_pos_k'], lp['cmp_pos_v'], lp['w_cmp_k'], lp['w_cmp_v'])
    o_a, o_cmp, o_slc, o_win = _decode_attention(l, za, cache_kv, cache_win, page_table, cmp_w, dtabs, t5_bias)
    o_c, h_new, conv_t = _lru_step(zb, conv0.transpose(1, 0, 2), h0, lp['conv_w'], lp['conv_b'], lp['w_rg'],
                                   lp['b_rg'], lp['w_ig'], lp['b_ig'], lp['lru_lambda'])
    merged = _merge(o_a, o_cmp, o_slc, o_win, za, pad(o_c), zb, lp['w_br_moba'], lp['w_br_nsa'], lp['w_br_lru'],
                    DEC_ROWS)
    mod3 = pad(mod).reshape(DEC_ROWS, 6, D).transpose(1, 0, 2)
    x, h2, logits = _outproj(merged, lp['w_out'], pad(x), mod3, lp['norm_ffn'].reshape(1, D), w_router_pad,
                             b_router_pad, DEC_ROWS, DEC_ROWS, 0)
    return x[:B], h2, logits, new_kv, win, h_new, conv_t.transpose(1, 0, 2)


SCAN_PAGES = 32
DEC_ROWS = 16


def _page_scan_kernel(pt_ref, *refs):
    del pt_ref
    ck_refs, cv_refs, mk_refs = refs[:SCAN_PAGES], refs[SCAN_PAGES:2 * SCAN_PAGES], refs[2 * SCAN_PAGES:3 * SCAN_PAGES]
    pek_ref, pev_ref, wk_ref, wv_ref, abk_ref, abv_ref, km_ref = refs[3 * SCAN_PAGES:]
    chunks = PAGE_SIZE // CMP_STRIDE
    for src, pe, w, dst in ((ck_refs, pek_ref, wk_ref, abk_ref), (cv_refs, pev_ref, wv_ref, abv_ref)):
        first = jnp.zeros((SCAN_PAGES * chunks, HEAD_DIM), F32)
        second = jnp.zeros((SCAN_PAGES * chunks, HEAD_DIM), F32)
        for l in range(CMP_STRIDE):
            x = jnp.concatenate([r[0, 0, pl.ds(l, chunks, stride=CMP_STRIDE), :] for r in src], axis=0)
            l2 = CMP_STRIDE + l
            first = first + jnp.dot((x + pe[l:l + 1, :]).astype(BF16), w[l], preferred_element_type=F32)
            second = second + jnp.dot((x + pe[l2:l2 + 1, :]).astype(BF16), w[l2], preferred_element_type=F32)
        dst[0, :, 0:HEAD_DIM] = first
        dst[0, :, HEAD_DIM:2 * HEAD_DIM] = second
    sums = [jnp.sum(r[0, 0], axis=0, keepdims=True) for r in mk_refs]
    per_blk = MOBA_BLOCK // PAGE_SIZE
    means = [functools.reduce(lambda a, b: a + b, sums[j * per_blk:(j + 1) * per_blk])
             for j in range(SCAN_PAGES // per_blk)]
    km_ref[0] = jnp.concatenate(means, axis=0) * (1.0 / MOBA_BLOCK)


def _page_scan(cache_kv, l, page_table, pe_k, pe_v, w_k, w_v):
    B, n_pages = page_table.shape
    steps = n_pages // SCAN_PAGES
    chunks = PAGE_SIZE // CMP_STRIDE
    blocks = SCAN_PAGES * PAGE_SIZE // MOBA_BLOCK
    cmp_col = 2 * MOBA_W // HEAD_DIM

    def cmp_spec(j, col):
        return pl.BlockSpec((1, 1, PAGE_SIZE, HEAD_DIM), lambda b, s, pt: (l, pt[b, s * SCAN_PAGES + j], 0, col))

    def mk_spec(j):
        return pl.BlockSpec((1, 1, PAGE_SIZE, MOBA_W), lambda b, s, pt: (l, pt[b, s * SCAN_PAGES + j], 0, 0))

    const = lambda a: pl.BlockSpec(a.shape, lambda b, s, pt: (0,) * a.ndim)
    grid_spec = pltpu.PrefetchScalarGridSpec(
        num_scalar_prefetch=1,
        grid=(B, steps),
        in_specs=([cmp_spec(j, cmp_col) for j in range(SCAN_PAGES)]
                  + [cmp_spec(j, cmp_col + 1) for j in range(SCAN_PAGES)] + [mk_spec(j) for j in range(SCAN_PAGES)]
                  + [const(pe_k), const(pe_v), const(w_k), const(w_v)]),
        out_specs=[pl.BlockSpec((1, SCAN_PAGES * chunks, 2 * HEAD_DIM), lambda b, s, pt: (b, s, 0)),
                   pl.BlockSpec((1, SCAN_PAGES * chunks, 2 * HEAD_DIM), lambda b, s, pt: (b, s, 0)),
                   pl.BlockSpec((1, blocks, MOBA_W), lambda b, s, pt: (b, s, 0))],
    )
    n_ch = n_pages * chunks
    return pl.pallas_call(
        _page_scan_kernel,
        grid_spec=grid_spec,
        out_shape=[jax.ShapeDtypeStruct((B, n_ch, 2 * HEAD_DIM), F32),
                   jax.ShapeDtypeStruct((B, n_ch, 2 * HEAD_DIM), F32),
                   jax.ShapeDtypeStruct((B, steps * blocks, MOBA_W), F32)],
        compiler_params=_cparams(("parallel", "arbitrary")),
        name="page_scan",
    )(page_table, *([cache_kv] * (3 * SCAN_PAGES)), pe_k, pe_v, w_k, w_v)


def _topk_lanes(score, k):
    lane = lax.broadcasted_iota(jnp.int32, score.shape, 1)
    out = jnp.zeros(score.shape, jnp.int32)
    for r in range(k):
        m = jnp.max(score, axis=1, keepdims=True)
        idx = jnp.min(jnp.where(score == m, lane, score.shape[1]), axis=1, keepdims=True)
        out = jnp.where(lane == r, idx, out)
        score = jnp.where(lane == idx, -jnp.inf, score)
    return out


def _head_rows(q):
    rows = [q[:, g * HEAD_DIM:(g + 1) * HEAD_DIM] for g in range(NSA_HEADS)]
    return jnp.concatenate(rows + [jnp.zeros((8 - NSA_HEADS, HEAD_DIM), q.dtype)], axis=0)


def _dec_select_kernel(abk_ref, abv_ref, km_ref, mq_ref, nq_ref, bias_ref, ovl_ref, ocmp_ref, tm_ref, ts_ref,
                       *, past):
    b = pl.program_id(0)
    n_ch = abk_ref.shape[1]
    nb = km_ref.shape[1]
    row = lax.broadcasted_iota(jnp.int32, (8, LANE), 0)
    mq = mq_ref[pl.ds(b, 1), :]
    sc = jnp.full((8, LANE), NEG, F32)
    for h in range(MOBA_HEADS):
        hs = slice(h * HEAD_DIM, (h + 1) * HEAD_DIM)
        qh = jnp.broadcast_to(mq[:, hs], (8, HEAD_DIM)).astype(BF16)
        s = lax.dot_general(qh, km_ref[0, :, hs].astype(BF16), (((1,), (1,)), ((), ())), preferred_element_type=F32)
        s = jnp.concatenate([s, jnp.full((8, LANE - nb), NEG, F32)], axis=1)
        sc = jnp.where(row == h, s, sc)
    tm_ref[0] = _topk_lanes(sc, MOBA_TOPK)
    abk = abk_ref[0]
    abv = abv_ref[0]
    kc = (abk[:, :HEAD_DIM] + pltpu.roll(abk[:, HEAD_DIM:], n_ch - 1, 0)).astype(BF16)
    vc = (abv[:, :HEAD_DIM] + pltpu.roll(abv[:, HEAD_DIM:], n_ch - 1, 0)).astype(BF16)
    q4 = _head_rows(nq_ref[pl.ds(b, 1), :]).astype(BF16)
    s = lax.dot_general(q4, kc, (((1,), (1,)), ((), ())), preferred_element_type=F32) * SCALE + bias_ref[...]
    n = lax.broadcasted_iota(jnp.int32, (8, n_ch), 1)
    vis = (n < n_ch - 1) & (n * CMP_STRIDE + (CMP_LEN - 1) <= past)
    s = jnp.where(vis, s, NEG)
    m = jnp.max(s, axis=1, keepdims=True)
    e = jnp.where(vis, jnp.exp(s - m), 0.0)
    den = jnp.sum(e, axis=1, keepdims=True)
    pb = (e / jnp.where(den > 0.0, den, 1.0)).astype(BF16)
    ocmp_ref[0] = jnp.dot(pb, vc, preferred_element_type=F32)
    imp = jnp.dot(pb, ovl_ref[...], preferred_element_type=F32)
    rows = lax.broadcasted_iota(jnp.int32, imp.shape, 0)
    imp = jnp.sum(jnp.where(rows < NSA_HEADS, imp, 0.0), axis=0, keepdims=True)
    ts_ref[0] = _topk_lanes(jnp.broadcast_to(imp, (8, imp.shape[1])), SLC_TOPK - 1)[:, :LANE]


def _dec_select(abk, abv, kmean, za_s, bias_cmp, overlap, past):
    B, n_ch, _ = abk.shape
    full = lambda a: pl.BlockSpec(a.shape, lambda b: (0,) * a.ndim)
    per_b = lambda a: pl.BlockSpec((1,) + a.shape[1:], lambda b: (b, 0, 0))
    out = jax.ShapeDtypeStruct((B, 8, LANE), F32)
    outi = jax.ShapeDtypeStruct((B, 8, LANE), jnp.int32)
    return pl.pallas_call(
        functools.partial(_dec_select_kernel, past=past),
        grid=(B,),
        in_specs=[per_b(abk), per_b(abv), per_b(kmean),
                  pl.BlockSpec((DEC_ROWS, MOBA_W), lambda b: (0, 0)),
                  pl.BlockSpec((DEC_ROWS, NSA_W), lambda b: (0, 3)),
                  full(bias_cmp), full(overlap)],
        out_specs=[pl.BlockSpec((1, 8, LANE), lambda b: (b, 0, 0))] * 3,
        out_shape=[out, outi, outi],
        compiler_params=_cparams(("parallel",)),
        name="dec_select",
    )(abk, abv, kmean, za_s, za_s, bias_cmp, overlap)


def _softmax_parts(scores):
    m = functools.reduce(jnp.maximum, [jnp.max(s, axis=1, keepdims=True) for s in scores])
    es = [jnp.exp(s - m) for s in scores]
    den = functools.reduce(lambda a, b: a + b, [jnp.sum(e, axis=1, keepdims=True) for e in es])
    return [e / den for e in es]


def _dec_attend_kernel(pm_ref, ps_ref, hs_ref, *refs, n_moba, n_slc, n_win):
    del pm_ref, ps_ref, hs_ref
    mk, mv = refs[:n_moba], refs[n_moba:2 * n_moba]
    sk, sv = refs[2 * n_moba:2 * n_moba + n_slc], refs[2 * n_moba + n_slc:2 * (n_moba + n_slc)]
    (kw_ref, vw_ref, mq_ref, mkn_ref, mvn_ref, nq_ref, skn_ref, svn_ref, wkn_ref, wvn_ref, bm_ref, bs_ref, bw_ref,
     oa_ref, oslc_ref, owin_ref) = refs[2 * (n_moba + n_slc):]
    b = pl.program_id(0)
    h = pl.program_id(1)
    nt = (((1,), (1,)), ((), ()))
    rnd = lambda a: a.astype(BF16).astype(F32)

    q = mq_ref[pl.ds(b, 1), :]
    q8 = jnp.broadcast_to(q, (8, HEAD_DIM)).astype(BF16)
    bm = bm_ref[0]
    scores = [lax.dot_general(q8, mk[j][0, 0].astype(BF16), nt, preferred_element_type=F32)[0:1] * SCALE
              + bm[j:j + 1] for j in range(n_moba)]
    s_new = jnp.sum(rnd(q) * rnd(mkn_ref[pl.ds(b, 1), :]), axis=1, keepdims=True) * SCALE + bm[n_moba:n_moba + 1, 0:1]
    probs = _softmax_parts(scores + [s_new])
    o = rnd(probs[-1]) * rnd(mvn_ref[pl.ds(b, 1), :])
    for j in range(n_moba):
        pj = jnp.broadcast_to(probs[j], (8, PAGE_SIZE)).astype(BF16)
        o = o + jnp.dot(pj, mv[j][0, 0].astype(BF16), preferred_element_type=F32)[0:1]
    oa_ref[0] = jnp.broadcast_to(o, (8, HEAD_DIM))

    @pl.when(h == 0)
    def _():
        q4f = _head_rows(nq_ref[pl.ds(b, 1), :])
        q4 = q4f.astype(BF16)
        scores = [lax.dot_general(q4, sk[j][0, 0].astype(BF16), nt, preferred_element_type=F32) * SCALE + bs_ref[0, j]
                  for j in range(n_slc)]
        s_new = (jnp.sum(rnd(q4f) * rnd(skn_ref[pl.ds(b, 1), :]), axis=1, keepdims=True) * SCALE
                 + bs_ref[0, n_slc][:, 0:1])
        probs = _softmax_parts(scores + [s_new])
        o = rnd(probs[-1]) * rnd(svn_ref[pl.ds(b, 1), :])
        for j in range(n_slc):
            o = o + jnp.dot(probs[j].astype(BF16), sv[j][0, 0].astype(BF16), preferred_element_type=F32)
        oslc_ref[0] = o
        s = lax.dot_general(q4, kw_ref[0, 0].astype(BF16), nt, preferred_element_type=F32) * SCALE + bw_ref[:, :n_win]
        i = lax.broadcasted_iota(jnp.int32, (8, n_win), 1)
        s = jnp.where(n_win - i < WINDOW, s, NEG)
        s_new = (jnp.sum(rnd(q4f) * rnd(wkn_ref[pl.ds(b, 1), :]), axis=1, keepdims=True) * SCALE
                 + bw_ref[:, n_win:n_win + 1])
        p_win, p_new = _softmax_parts([s, s_new])
        p_win = jnp.where(n_win - i < WINDOW, p_win, 0.0)
        owin_ref[0] = (rnd(p_new) * rnd(wvn_ref[pl.ds(b, 1), :])
                       + jnp.dot(p_win.astype(BF16), vw_ref[0, 0].astype(BF16), preferred_element_type=F32))


def _dec_attend(cache_kv, cache_win, l, za_s, pages_moba, pages_slc, halves_slc, bias_moba, bias_slc, bias_win, B):
    n_moba = MOBA_TOPK * MOBA_BLOCK // PAGE_SIZE
    n_slc = SLC_TOPK - 1
    n_win = cache_win.shape[2]

    def moba_spec(j, col0):
        return pl.BlockSpec((1, 1, PAGE_SIZE, HEAD_DIM),
                            lambda b, h, pm, ps, hs: (l, pm[(b * MOBA_HEADS + h) * n_moba + j], 0, col0 + h))

    def slc_spec(j, col):
        return pl.BlockSpec((1, 1, SLC_BLOCK, HEAD_DIM),
                            lambda b, h, pm, ps, hs: (l, ps[b * n_slc + j], hs[b * n_slc + j], col))

    zcol = lambda w, c: pl.BlockSpec((DEC_ROWS, w), lambda b, h, pm, ps, hs: (0, c))
    zhead = lambda c0: pl.BlockSpec((DEC_ROWS, HEAD_DIM), lambda b, h, pm, ps, hs: (0, c0 + h))
    slc_col = (2 * MOBA_W + 2 * HEAD_DIM) // HEAD_DIM
    in_specs = ([moba_spec(j, 0) for j in range(n_moba)] + [moba_spec(j, MOBA_HEADS) for j in range(n_moba)]
                + [slc_spec(j, slc_col) for j in range(n_slc)] + [slc_spec(j, slc_col + 1) for j in range(n_slc)]
                + [pl.BlockSpec((1, 1, n_win, HEAD_DIM), lambda b, h, pm, ps, hs: (l, b, 0, 0)),
                   pl.BlockSpec((1, 1, n_win, HEAD_DIM), lambda b, h, pm, ps, hs: (l, b, 0, 1)),
                   zhead(0), zhead(MOBA_HEADS), zhead(2 * MOBA_HEADS), zcol(NSA_W, 3),
                   zcol(HEAD_DIM, 18), zcol(HEAD_DIM, 19), zcol(HEAD_DIM, 20), zcol(HEAD_DIM, 21),
                   pl.BlockSpec((1, 8, LANE), lambda b, h, pm, ps, hs: (b * MOBA_HEADS + h, 0, 0)),
                   pl.BlockSpec((1, n_slc + 1, 8, SLC_BLOCK), lambda b, h, pm, ps, hs: (b, 0, 0, 0)),
                   pl.BlockSpec(bias_win.shape, lambda b, h, pm, ps, hs: (0, 0))])
    grid_spec = pltpu.PrefetchScalarGridSpec(
        num_scalar_prefetch=3,
        grid=(B, MOBA_HEADS),
        in_specs=in_specs,
        out_specs=[pl.BlockSpec((1, 8, HEAD_DIM), lambda b, h, pm, ps, hs: (b * MOBA_HEADS + h, 0, 0)),
                   pl.BlockSpec((1, 8, HEAD_DIM), lambda b, h, pm, ps, hs: (b, 0, 0)),
                   pl.BlockSpec((1, 8, HEAD_DIM), lambda b, h, pm, ps, hs: (b, 0, 0))],
    )
    return pl.pallas_call(
        functools.partial(_dec_attend_kernel, n_moba=n_moba, n_slc=n_slc, n_win=n_win),
        grid_spec=grid_spec,
        out_shape=[jax.ShapeDtypeStruct((B * MOBA_HEADS, 8, HEAD_DIM), F32),
                   jax.ShapeDtypeStruct((B, 8, HEAD_DIM), F32), jax.ShapeDtypeStruct((B, 8, HEAD_DIM), F32)],
        compiler_params=_cparams(("parallel", "arbitrary")),
        name="dec_attend",
    )(pages_moba, pages_slc, halves_slc, *([cache_kv] * (2 * n_moba + 2 * n_slc)), cache_win, cache_win,
      za_s, za_s, za_s, za_s, za_s, za_s, za_s, za_s, bias_moba, bias_slc, bias_win)


def _lru_step_kernel(x_ref, g_ref, c0_ref, h0_ref, cw_ref, cb_ref, wr_ref, br_ref, wi_ref, bi_ref, lam_ref,
                     y_ref, h_ref, cl_ref):
    B = h0_ref.shape[0]
    x = x_ref[0:B, :]
    u = cb_ref[...] + c0_ref[0] * cw_ref[0:1, :]
    u = u + c0_ref[1] * cw_ref[1:2, :]
    u = u + c0_ref[2] * cw_ref[2:3, :]
    u = u + x * cw_ref[3:4, :]
    ub = u.astype(BF16)
    r = jax.nn.sigmoid(jnp.dot(ub, wr_ref[0], preferred_element_type=F32) + br_ref[...])
    ig = jax.nn.sigmoid(jnp.dot(ub, wi_ref[0], preferred_element_type=F32) + bi_ref[...])
    log_a = (-LRU_C * r) * _softplus(-lam_ref[...])
    h = jnp.exp(log_a) * h0_ref[...] + jnp.sqrt(1.0 - jnp.exp(2.0 * log_a)) * (ig * u)
    h_ref[...] = h
    y_ref[...] = h * _gelu_tanh(g_ref[0:B, :])
    cl_ref[0] = c0_ref[1]
    cl_ref[1] = c0_ref[2]
    cl_ref[2] = x


def _lru_step(zb_s, conv0_t, h0, conv_w, conv_b, w_r, b_r, w_i, b_i, lam):
    B = h0.shape[0]
    nblk = LRU_BLOCKS
    vec = lambda a: a.reshape(1, LRU_WIDTH)
    row_spec = pl.BlockSpec((1, LRU_BLOCK), lambda n: (0, n))
    bw = lambda: pl.BlockSpec((B, LRU_BLOCK), lambda n: (0, n))
    return pl.pallas_call(
        _lru_step_kernel,
        grid=(nblk,),
        in_specs=[pl.BlockSpec((DEC_ROWS, LRU_BLOCK), lambda n: (0, n)),
                  pl.BlockSpec((DEC_ROWS, LRU_BLOCK), lambda n: (0, nblk + n)),
                  pl.BlockSpec((CONV_W - 1, B, LRU_BLOCK), lambda n: (0, 0, n)), bw(),
                  pl.BlockSpec((CONV_W, LRU_BLOCK), lambda n: (0, n)), row_spec,
                  pl.BlockSpec((1, LRU_BLOCK, LRU_BLOCK), lambda n: (n, 0, 0)), row_spec,
                  pl.BlockSpec((1, LRU_BLOCK, LRU_BLOCK), lambda n: (n, 0, 0)), row_spec, row_spec],
        out_specs=[bw(), bw(), pl.BlockSpec((CONV_W - 1, B, LRU_BLOCK), lambda n: (0, 0, n))],
        out_shape=[jax.ShapeDtypeStruct((B, LRU_WIDTH), F32), jax.ShapeDtypeStruct((B, LRU_WIDTH), F32),
                   jax.ShapeDtypeStruct((CONV_W - 1, B, LRU_WIDTH), F32)],
        compiler_params=_cparams(("parallel",)),
        name="lru_step",
    )(zb_s, zb_s, conv0_t, h0, conv_w, vec(conv_b), w_r, vec(b_r), w_i, vec(b_i), vec(lam))


def _bias_heads(tab, dist):
    bucket = _t5_bucket(dist)
    out = jnp.zeros(bucket.shape, F32)
    shape = (1, -1) + (1,) * (bucket.ndim - 2)
    for bkt in range(T5_BUCKETS):
        out = jnp.where(bucket == bkt, tab[:, bkt].reshape(shape), out)
    return out


def _decode_tables(t5_bias, past, n_win):
    tab_n = t5_bias.T[MOBA_HEADS:]
    n_ch = past // CMP_STRIDE
    n = jnp.arange(n_ch)
    bias_cmp = jnp.pad(_bias_lookup(tab_n, past - (n * CMP_STRIDE + CMP_LEN - 1)), ((0, 8 - NSA_HEADS), (0, 0)))
    s_start = jnp.arange(past // SLC_BLOCK) * SLC_BLOCK
    c_start = n * CMP_STRIDE
    overlap = ((c_start[:, None] < s_start[None, :] + SLC_BLOCK) & (c_start[:, None] + CMP_LEN > s_start[None, :])
               & (n[:, None] < n_ch - 1)).astype(BF16)
    d_win = jnp.concatenate([n_win - jnp.arange(n_win), jnp.zeros((LANE,), jnp.int32)])
    bias_win = jnp.pad(_bias_lookup(tab_n, d_win), ((0, 8 - NSA_HEADS), (0, 0)))
    return bias_cmp, overlap, bias_win


def _decode_attention(l, za_s, cache_kv, cache_win, page_table, lp_cmp, dtabs, t5_bias):
    B, n_pages = page_table.shape
    past = n_pages * PAGE_SIZE
    bias_cmp, overlap, bias_win = dtabs
    pe_k, pe_v, w_k, w_v = lp_cmp
    abk, abv, kmean = _page_scan(cache_kv, l, page_table, pe_k, pe_v, w_k, w_v)
    o_cmp, top_m, top_s = _dec_select(abk, abv, kmean, za_s, bias_cmp, overlap, past)
    top_m = top_m[:, :MOBA_HEADS, :MOBA_TOPK]
    top_s = top_s[:, 0, :SLC_TOPK - 1]
    tab = t5_bias.T
    bidx = jnp.arange(B)
    per_blk = MOBA_BLOCK // PAGE_SIZE
    pg_off = top_m[..., None] * per_blk + jnp.arange(per_blk)
    pages_moba = page_table[bidx[:, None, None, None], pg_off].reshape(-1)
    kpos = (pg_off.reshape(B, MOBA_HEADS, -1, 1) * PAGE_SIZE + jnp.arange(PAGE_SIZE))
    bm = _bias_heads(tab[:MOBA_HEADS], past - kpos)
    bm_new = jnp.zeros((B, MOBA_HEADS, 1, LANE), F32).at[..., 0].set(tab[None, :MOBA_HEADS, 0, None])
    bias_moba = jnp.concatenate([bm, bm_new, jnp.zeros((B, MOBA_HEADS, 1, LANE), F32)], axis=2)
    bias_moba = bias_moba.reshape(B * MOBA_HEADS, 8, LANE)
    per_page = PAGE_SIZE // SLC_BLOCK
    pages_slc = page_table[bidx[:, None], top_s // per_page].reshape(-1)
    halves_slc = (top_s % per_page).reshape(-1)
    spos = top_s[:, None, :, None] * SLC_BLOCK + jnp.arange(SLC_BLOCK)
    bs = _bias_heads(tab[MOBA_HEADS:], jnp.broadcast_to(past - spos, (B, NSA_HEADS, SLC_TOPK - 1, SLC_BLOCK)))
    bs_new = jnp.zeros((B, NSA_HEADS, 1, SLC_BLOCK), F32).at[..., 0].set(tab[None, MOBA_HEADS:, 0, None])
    bias_slc = jnp.concatenate([bs, bs_new], axis=2).transpose(0, 2, 1, 3)
    bias_slc = jnp.pad(bias_slc, ((0, 0), (0, 0), (0, 8 - NSA_HEADS), (0, 0)))
    o_a, o_slc, o_win = _dec_attend(cache_kv, cache_win, l, za_s, pages_moba, pages_slc, halves_slc,
                                    bias_moba, bias_slc, bias_win, B)
    pad = lambda a: jnp.pad(a, ((0, DEC_ROWS - B), (0, 0)))
    heads = lambda a: pad(a[:, :NSA_HEADS].reshape(B, NSA_W))
    return pad(o_a[:, 0].reshape(B, MOBA_W)), heads(o_cmp), heads(o_slc), heads(o_win)


def kernel(x_prompt, x_sample, cache_kv, cache_win, state_lru_h, state_lru_conv, page_table, c_prompt, c_sample,
           w_ada, b_ada, norm_mix, norm_ffn, w_in, cmp_pos_k, cmp_pos_v, w_cmp_k, w_cmp_v, conv_w, conv_b,
           w_rg, b_rg, w_ig, b_ig, lru_lambda, w_br_moba, w_br_nsa, w_br_lru, w_out, w_e_gate, w_e_up, w_e_down,
           t5_bias, w_router, b_router, norm_final):
    n_p, T, D = x_prompt.shape
    n_s = x_sample.shape[0]
    n_tok = n_p * T
    xp = x_prompt.reshape(n_tok, D)
    xs = x_sample.reshape(n_s, D)
    tabs = _prompt_tables(t5_bias, T)
    w_router_pad = jnp.pad(w_router, ((0, 0), (0, LANE - N_EXPERTS))).astype(BF16)
    b_router_pad = jnp.pad(b_router, (0, LANE - N_EXPERTS)).reshape(1, LANE)
    c_all = jax.nn.silu(jnp.concatenate([c_prompt, c_sample], axis=0))
    c_all = jnp.pad(c_all, ((0, 16 - n_p - n_s), (0, 0))).astype(BF16)
    dtabs = _decode_tables(t5_bias, page_table.shape[1] * PAGE_SIZE, cache_win.shape[2])
    moe_rows = LANE
    outs = [[] for _ in range(8)]
    mods = [_matmul_w32(c_all, w_ada, l, 0, 6 * D, 2048, 16) + b_ada[l] for l in range(DEPTH)]
    mod0 = mods[0][:n_p, None, :]
    hp = (_rms(x_prompt, norm_mix[0]) * (1.0 + mod0[..., D:2 * D]) + mod0[..., :D]).astype(BF16).reshape(n_tok, D)
    for l in range(DEPTH):
        lp = {'l': l, 'norm_mix': norm_mix[l], 'norm_ffn': norm_ffn[l], 'w_in': w_in,
              'cmp_pos_k': cmp_pos_k[l], 'cmp_pos_v': cmp_pos_v[l],
              'w_cmp_k': w_cmp_k[l].astype(BF16), 'w_cmp_v': w_cmp_v[l].astype(BF16),
              'conv_w': conv_w[l], 'conv_b': conv_b[l], 'w_rg': w_rg[l].astype(BF16), 'b_rg': b_rg[l],
              'w_ig': w_ig[l].astype(BF16), 'b_ig': b_ig[l], 'lru_lambda': lru_lambda[l],
              'w_br_moba': w_br_moba[l].astype(BF16), 'w_br_nsa': w_br_nsa[l].astype(BF16),
              'w_br_lru': w_br_lru[l].astype(BF16), 'w_out': w_out[l].astype(BF16)}
        mod = mods[l]
        mod3 = mod.reshape(16 * 6, 1, D)
        mod_s = mod[n_p:n_p + n_s]
        xp, h2_p, lg_p, kv_p, win_p, h_p, conv_p = _mixer_prompt(xp, hp, n_p, T, mod3, lp, tabs, w_router_pad,
                                                                 b_router_pad)
        xs, h2_s, lg_s, kv_s, win_s, h_s, conv_s = _mixer_decode(l, xs, mod_s, lp, cache_kv, cache_win, page_table,
                                                                 state_lru_h[l], state_lru_conv[l], dtabs, t5_bias,
                                                                 w_router_pad, b_router_pad)
        tail = ((0, moe_rows - h2_s.shape[0]), (0, 0))
        h2 = jnp.concatenate([h2_p, jnp.pad(h2_s, tail)], axis=0)
        logits = jnp.concatenate([lg_p, jnp.pad(lg_s, tail)], axis=0)[:, :N_EXPERTS]
        moe_a, moe_b = _moe_prompt(h2, logits, l, w_e_gate, w_e_up, w_e_down)
        last = l == DEPTH - 1
        if last:
            y_prompt, = _ffn_residual(xp, moe_a, moe_b, mod3, mod3, norm_final, 256, T, True)
        else:
            mod3_next = mods[l + 1].reshape(16 * 6, 1, D)
            xp, hp = _ffn_residual(xp, moe_a, moe_b, mod3, mod3_next, norm_mix[l + 1], 256, T, False)
        xs = xs + mod_s[:, 5 * D:] * (moe_a[n_tok:n_tok + n_s] + moe_b[n_tok:n_tok + n_s])
        for lst, val in zip(outs, (kv_p, kv_s[:, None, :], win_p, win_s, h_p, h_s, conv_p, conv_s)):
            lst.append(val)
    y_sample = _rms(xs, norm_final).reshape(n_s, 1, D)
    return (y_prompt.reshape(n_p, T, D), y_sample) + tuple(jnp.stack(o) for o in outs)
```

```python
import functools
import math

import jax
import jax.numpy as jnp
from jax import lax
from jax.experimental import pallas as pl
from jax.experimental.pallas import tpu as pltpu

D_MODEL = 2048
DEPTH = 2
PAGE_SIZE = 128
HEAD_DIM = 128
MOBA_HEADS = 4
MOBA_BLOCK = 256
MOBA_TOPK = 3
NSA_HEADS = 4
CMP_STRIDE = 16
CMP_LEN = 32
SLC_BLOCK = 64
SLC_TOPK = 16
WINDOW = 512
LRU_WIDTH = D_MODEL // 2
LRU_BLOCKS = 8
LRU_BLOCK = LRU_WIDTH // LRU_BLOCKS
LRU_C = 8.0
CONV_W = 4
N_EXPERTS = 16
N_GROUPS = 4
EXPERTS_PER_GROUP = N_EXPERTS // N_GROUPS
TOP_K = 2
D_EXPERT = D_MODEL // 2
T5_BUCKETS = 32
T5_MAX_DIST = 128
EPS = 1e-6
NEG = -1e30

MOBA_W = MOBA_HEADS * HEAD_DIM
NSA_W = NSA_HEADS * HEAD_DIM
KV_DIM = 2 * MOBA_W + 4 * HEAD_DIM
WIN_DIM = 2 * HEAD_DIM
IN_WIDTHS = (MOBA_W, MOBA_W, MOBA_W, NSA_W, HEAD_DIM, HEAD_DIM, HEAD_DIM, HEAD_DIM, HEAD_DIM, HEAD_DIM,
             3 * NSA_HEADS, LRU_WIDTH, LRU_WIDTH, D_MODEL, D_MODEL, D_MODEL)
COL_NG = 3 * MOBA_W + NSA_W + 6 * HEAD_DIM
COL_LX = COL_NG + 3 * NSA_HEADS
ZA_TILE = 1024
ZA_W = -(-(COL_NG + 3 * NSA_HEADS) // ZA_TILE) * ZA_TILE
ZB_TILE = 1024
ZB_W = 2 * LRU_WIDTH + 3 * D_MODEL
SCALE = HEAD_DIM ** -0.5

LANE = 128
ATT_TILE = 512
MOE_TILE = 256
VMEM_LIMIT = 56 * 1024 * 1024

F32 = jnp.float32
BF16 = jnp.bfloat16


def _cparams(sem):
    return pltpu.CompilerParams(dimension_semantics=sem, vmem_limit_bytes=VMEM_LIMIT)


def _mm_kernel(x_ref, w_ref, o_ref):
    o_ref[...] = jnp.dot(x_ref[...].astype(BF16), w_ref[...].astype(BF16), preferred_element_type=F32)


def _matmul(x, w, tm, tn):
    M, K = x.shape
    N = w.shape[1]
    assert M % tm == 0 and N % tn == 0
    return pl.pallas_call(
        _mm_kernel,
        grid=(N // tn, M // tm),
        in_specs=[pl.BlockSpec((tm, K), lambda j, i: (i, 0)),
                  pl.BlockSpec((K, tn), lambda j, i: (0, j))],
        out_specs=pl.BlockSpec((tm, tn), lambda j, i: (i, j)),
        out_shape=jax.ShapeDtypeStruct((M, N), F32),
        compiler_params=_cparams(("parallel", "parallel")),
        name="matmul",
    )(x, w)


def _mm_w32_kernel(x_ref, w_ref, *rest, shift):
    if shift:
        wn_ref, o_ref, wb_scr = rest
    else:
        o_ref, wb_scr = rest

    @pl.when(pl.program_id(1) == 0)
    def _():
        if not shift:
            wb_scr[...] = w_ref[0].astype(BF16)
            return
        nblk = w_ref.shape[2] // LANE
        q, r = divmod(shift, LANE)

        def window_block(i):
            ref = w_ref if i < nblk else wn_ref
            i = i % nblk
            return ref[0, :, i * LANE:(i + 1) * LANE]

        for c in range(nblk):
            blk = window_block(q + c)
            if r:
                pair = jnp.concatenate([blk, window_block(q + c + 1)], axis=1)
                blk = pltpu.roll(pair, 2 * LANE - r, 1)[:, :LANE]
            wb_scr[:, c * LANE:(c + 1) * LANE] = blk.astype(BF16)

    o_ref[...] = jnp.dot(x_ref[...], wb_scr[...], preferred_element_type=F32)


def _matmul_w32(x, w3, l, col0, n_out, tn, tm):
    M, K = x.shape
    shift = col0 % tn
    blk0 = col0 // tn
    assert n_out % tn == 0 and M % tm == 0
    mode = {'pipeline_mode': pl.Buffered(1)} if shift else {}
    wspec = lambda off: pl.BlockSpec((1, K, tn), lambda j, i: (l, 0, blk0 + j + off), **mode)
    in_specs = [pl.BlockSpec((tm, K), lambda j, i: (i, 0)), wspec(0)]
    args = [x, w3]
    if shift:
        in_specs.append(wspec(1))
        args.append(w3)
    return pl.pallas_call(
        functools.partial(_mm_w32_kernel, shift=shift),
        grid=(n_out // tn, M // tm),
        in_specs=in_specs,
        out_specs=pl.BlockSpec((tm, tn), lambda j, i: (i, j)),
        out_shape=jax.ShapeDtypeStruct((M, n_out), F32),
        scratch_shapes=[pltpu.VMEM((K, tn), BF16)],
        compiler_params=_cparams(("parallel", "arbitrary")),
        name="matmul_w32",
    )(*args)


def _za_kernel(x_ref, w_ref, *rest):
    o_ref, kv_ref, wb_scr = rest[-3:]

    @pl.when(pl.program_id(1) == 0)
    def _():
        wb_scr[...] = w_ref[0].astype(BF16)

    o = jnp.dot(x_ref[...], wb_scr[...], preferred_element_type=F32)
    o_ref[...] = o
    half = ZA_TILE // 2
    kv_ref[0] = jnp.where(pl.program_id(0) == 0, o[:, half:], o[:, :half])


def _za_proj(x, w3, l, tm, kv_prev):
    M, K = x.shape
    tn, half = ZA_TILE, ZA_TILE // 2
    assert MOBA_W == half and NSA_W == half and 4 * HEAD_DIM == half and KV_DIM == 3 * half and ZA_W == 3 * tn
    in_specs = [pl.BlockSpec((tm, K), lambda j, i: (i, 0)), pl.BlockSpec((1, K, tn), lambda j, i: (l, 0, j))]
    args = [x, w3]
    aliases = {}
    if kv_prev is not None:
        in_specs.append(pl.BlockSpec(memory_space=pl.ANY))
        args.append(kv_prev)
        aliases = {2: 1}
    return pl.pallas_call(
        _za_kernel,
        grid=(ZA_W // tn, M // tm),
        in_specs=in_specs,
        out_specs=[pl.BlockSpec((tm, tn), lambda j, i: (i, j)), pl.BlockSpec((1, tm, half), lambda j, i: (l, i, j))],
        out_shape=[jax.ShapeDtypeStruct((M, ZA_W), F32), jax.ShapeDtypeStruct((DEPTH, M, KV_DIM), F32)],
        scratch_shapes=[pltpu.VMEM((K, tn), BF16)],
        input_output_aliases=aliases,
        compiler_params=_cparams(("parallel", "arbitrary")),
        name="za_proj",
    )(*args)


def _merge_kernel(oa_ref, ocmp_ref, oslc_ref, owin_ref, ng_ref, oc_ref, ga_ref, gb_ref, gc_ref,
                  wa_ref, wb_ref, wc_ref, o_ref):
    gt = jax.nn.sigmoid(ng_ref[...])
    heads = []
    for g in range(NSA_HEADS):
        sl = slice(g * HEAD_DIM, (g + 1) * HEAD_DIM)
        heads.append(gt[:, 3 * g:3 * g + 1] * ocmp_ref[:, sl] + gt[:, 3 * g + 1:3 * g + 2] * oslc_ref[:, sl]
                     + gt[:, 3 * g + 2:3 * g + 3] * owin_ref[:, sl])
    ob = jnp.concatenate(heads, axis=1).astype(BF16)
    pa = jnp.dot(oa_ref[...].astype(BF16), wa_ref[...], preferred_element_type=F32)
    pb = jnp.dot(ob, wb_ref[...], preferred_element_type=F32)
    pc = jnp.dot(oc_ref[...].astype(BF16), wc_ref[...], preferred_element_type=F32)
    merged = (jax.nn.sigmoid(ga_ref[...]) * pa + jax.nn.sigmoid(gb_ref[...]) * pb
              + jax.nn.sigmoid(gc_ref[...]) * pc)
    o_ref[...] = merged.astype(BF16)


def _merge(o_a, o_cmp, o_slc, o_win, za, o_c, zb, w_a, w_b, w_c, tm):
    N = o_a.shape[0]
    D = D_MODEL
    row = lambda w, c: pl.BlockSpec((tm, w), lambda i: (i, c))
    full = lambda a: pl.BlockSpec(a.shape, lambda i: (0, 0))
    return pl.pallas_call(
        _merge_kernel,
        grid=(N // tm,),
        in_specs=[row(MOBA_W, 0), row(NSA_W, 0), row(NSA_W, 0), row(NSA_W, 0), row(LANE, COL_NG // LANE),
                  row(LRU_WIDTH, 0), row(D, 1), row(D, 2), row(D, 3), full(w_a), full(w_b), full(w_c)],
        out_specs=pl.BlockSpec((tm, D), lambda i: (i, 0)),
        out_shape=jax.ShapeDtypeStruct((N, D), BF16),
        compiler_params=_cparams(("parallel",)),
        name="merge",
    )(o_a, o_cmp, o_slc, o_win, za, o_c, zb, zb, zb, w_a, w_b, w_c)


def _outproj_kernel(m_ref, w_ref, x_ref, g1_ref, sc2_ref, sh2_ref, nf_ref, wr_ref, br_ref,
                    xo_ref, h2_ref, lg_ref):
    y = jnp.dot(m_ref[...], w_ref[...], preferred_element_type=F32)
    x = x_ref[...] + g1_ref[0] * y
    xo_ref[...] = x
    h2 = x * lax.rsqrt(jnp.mean(x * x, axis=-1, keepdims=True) + EPS) * nf_ref[...]
    h2 = (h2 * (1.0 + sc2_ref[0]) + sh2_ref[0]).astype(BF16)
    h2_ref[...] = h2
    lg_ref[...] = jnp.dot(h2, wr_ref[...], preferred_element_type=F32) + br_ref[...]


def _outproj(merged, w_out, x, mod3, norm_ffn, w_router_pad, b_router_pad, tm, rows_per_mod, mod_base):
    N, D = x.shape
    R = mod3.shape[1]
    mspec = lambda k: pl.BlockSpec((1, R, D), lambda i: ((mod_base + (i * tm) // rows_per_mod) * 6 + k, 0, 0))
    full = lambda a: pl.BlockSpec(a.shape, lambda i: (0, 0))
    return pl.pallas_call(
        _outproj_kernel,
        grid=(N // tm,),
        in_specs=[pl.BlockSpec((tm, D), lambda i: (i, 0)), full(w_out), pl.BlockSpec((tm, D), lambda i: (i, 0)),
                  mspec(2), mspec(4), mspec(3), full(norm_ffn), full(w_router_pad), full(b_router_pad)],
        out_specs=[pl.BlockSpec((tm, D), lambda i: (i, 0)), pl.BlockSpec((tm, D), lambda i: (i, 0)),
                   pl.BlockSpec((tm, LANE), lambda i: (i, 0))],
        out_shape=[jax.ShapeDtypeStruct((N, D), F32), jax.ShapeDtypeStruct((N, D), BF16),
                   jax.ShapeDtypeStruct((N, LANE), F32)],
        compiler_params=_cparams(("parallel",)),
        name="outproj",
    )(merged, w_out, x, mod3, mod3, mod3, norm_ffn, w_router_pad, b_router_pad)


def _ffn_residual_kernel(x_ref, a_ref, b_ref, g2_ref, gain_ref, sc_ref, sh_ref, *out_refs, last):
    x = x_ref[...] + g2_ref[0] * (a_ref[...] + b_ref[...])
    y = x * lax.rsqrt(jnp.mean(x * x, axis=-1, keepdims=True) + EPS) * gain_ref[...]
    if last:
        out_refs[0][...] = y
    else:
        out_refs[0][...] = x
        out_refs[1][...] = (y * (1.0 + sc_ref[0]) + sh_ref[0]).astype(BF16)


def _ffn_residual(x, a, b, mod3, mod3_next, gain, tm, rows_per_mod, last):
    N, D = x.shape
    mrow = lambda k: pl.BlockSpec((1, 1, D), lambda i: (((i * tm) // rows_per_mod) * 6 + k, 0, 0))
    tile = pl.BlockSpec((tm, D), lambda i: (i, 0))
    out_shape = [jax.ShapeDtypeStruct((N, D), F32)] + ([] if last else [jax.ShapeDtypeStruct((N, D), BF16)])
    return pl.pallas_call(
        functools.partial(_ffn_residual_kernel, last=last),
        grid=(N // tm,),
        in_specs=[tile, tile, tile, mrow(5), pl.BlockSpec((1, D), lambda i: (0, 0)), mrow(1), mrow(0)],
        out_specs=[tile] * len(out_shape),
        out_shape=out_shape,
        compiler_params=_cparams(("parallel",)),
        name="ffn_residual",
    )(x, a, b, mod3, gain.reshape(1, D), mod3_next, mod3_next)


def _rank_select(score, own, nblk, topk):
    lane = lax.broadcasted_iota(jnp.int32, score.shape, 1)
    past = lane < own
    s = jnp.where(past, score, NEG)
    rank = jnp.zeros(score.shape, jnp.int32)
    for j in range(nblk):
        col = s[:, j:j + 1]
        beats = (col > s) | ((col == s) & (lane > j))
        rank = rank + beats.astype(jnp.int32)
    return (past & (rank < topk)) | (lane == own)


def _moba_select_kernel(q_ref, k_ref, sel_ref):
    T = q_ref.shape[1]
    nb = T // MOBA_BLOCK
    k = k_ref[0]
    kmean = jnp.sum(k.reshape(nb, MOBA_BLOCK, HEAD_DIM), axis=1) * (1.0 / MOBA_BLOCK)
    kmean = jnp.concatenate([kmean, jnp.zeros((LANE - nb, HEAD_DIM), F32)], axis=0)
    score = lax.dot_general(q_ref[0].astype(BF16), kmean.astype(BF16), (((1,), (1,)), ((), ())),
                            preferred_element_type=F32)
    own = lax.broadcasted_iota(jnp.int32, (T, 1), 0) // MOBA_BLOCK
    sel = _rank_select(score, own, nb, MOBA_TOPK)
    sel_ref[0, 0] = sel.astype(BF16)


def _moba_select(za, B, T):
    return pl.pallas_call(
        _moba_select_kernel,
        grid=(B, MOBA_HEADS),
        in_specs=[pl.BlockSpec((1, T, HEAD_DIM), lambda b, h: (b, 0, h)),
                  pl.BlockSpec((1, T, HEAD_DIM), lambda b, h: (b, 0, MOBA_HEADS + h))],
        out_specs=pl.BlockSpec((1, 1, T, LANE), lambda b, h: (b, h, 0, 0)),
        out_shape=jax.ShapeDtypeStruct((B, MOBA_HEADS, T, LANE), BF16),
        compiler_params=_cparams(("parallel", "parallel")),
        name="moba_select",
    )(za, za)


def _flash_kernel(*refs, G, window_tiles, use_sel, per_head_kv):
    if use_sel:
        q_ref, k_ref, v_ref, bias_ref, sel_ref, e_ref, o_ref, m_scr, l_scr, acc_scr = refs
    else:
        q_ref, k_ref, v_ref, bias_ref, o_ref, m_scr, l_scr, acc_scr = refs
    qi = pl.program_id(1)
    kj = pl.program_id(2)
    nk = pl.num_programs(2)
    tq, tk = ATT_TILE, ATT_TILE

    @pl.when(kj == 0)
    def _():
        m_scr[...] = jnp.full(m_scr.shape, NEG, F32)
        l_scr[...] = jnp.zeros(l_scr.shape, F32)
        acc_scr[...] = jnp.zeros(acc_scr.shape, F32)

    lo = jnp.maximum(qi - window_tiles, 0) if window_tiles is not None else 0

    @pl.when((kj >= lo) & (kj <= qi))
    def _():
        row = lax.broadcasted_iota(jnp.int32, (tq, tk), 0)
        col = lax.broadcasted_iota(jnp.int32, (tq, tk), 1)
        dist = (qi - kj) * tq + row - col
        band = dist >= 0
        if window_tiles is not None:
            band = band & (dist < WINDOW)
        mask = band
        for g in range(G):
            kv = slice(g * HEAD_DIM, (g + 1) * HEAD_DIM) if per_head_kv else slice(0, HEAD_DIM)
            if g == 0 or per_head_kv:
                k = k_ref[0, :, kv].astype(BF16)
                v = v_ref[0, :, kv].astype(BF16)
                if use_sel:
                    hit = jnp.dot(sel_ref[0, g], e_ref[...], preferred_element_type=F32)
                    mask = band & (hit > 0.5)
            q = q_ref[0, :, g * HEAD_DIM:(g + 1) * HEAD_DIM].astype(BF16)
            s = lax.dot_general(q, k, (((1,), (1,)), ((), ())), preferred_element_type=F32)
            s = s * SCALE + bias_ref[g, 0]
            s = jnp.where(mask, s, NEG)
            m_prev = m_scr[g]
            m_new = jnp.maximum(m_prev, jnp.max(s, axis=1, keepdims=True))
            p = jnp.exp(s - m_new)
            alpha = jnp.exp(m_prev - m_new)
            l_scr[g] = alpha * l_scr[g] + jnp.sum(p, axis=1, keepdims=True)
            acc_scr[g] = alpha * acc_scr[g] + jnp.dot(p.astype(BF16), v, preferred_element_type=F32)
            m_scr[g] = m_new

    @pl.when(kj == nk - 1)
    def _():
        for g in range(G):
            l = l_scr[g]
            o = acc_scr[g] / jnp.where(l > 0.0, l, 1.0)
            o_ref[0, :, g * HEAD_DIM:(g + 1) * HEAD_DIM] = o


def _flash(za, bias, *, B, T, q_col, k_col, v_col, bias_blk, per_head_kv, window_tiles=None, sel=None, emat=None):
    tq = tk = ATT_TILE
    nq, nk = T // tq, T // tk
    G = NSA_HEADS
    use_sel = sel is not None
    kv_w = G * HEAD_DIM if per_head_kv else HEAD_DIM

    def kv_blk(i, j):
        lo = jnp.maximum(i - window_tiles, 0) if window_tiles is not None else 0
        return jnp.clip(j, lo, i)

    in_specs = [
        pl.BlockSpec((1, tq, G * HEAD_DIM), lambda b, i, j: (b, i, q_col)),
        pl.BlockSpec((1, tk, kv_w), lambda b, i, j: (b, kv_blk(i, j), k_col)),
        pl.BlockSpec((1, tk, kv_w), lambda b, i, j: (b, kv_blk(i, j), v_col)),
        pl.BlockSpec((G, 1, tq, tk), lambda b, i, j: (bias_blk, jnp.clip(i - j, 0, 2), 0, 0)),
    ]
    args = [za, za, za, bias]
    if use_sel:
        in_specs += [pl.BlockSpec((1, sel.shape[1], tq, LANE), lambda b, i, j: (b, 0, i, 0)),
                     pl.BlockSpec((LANE, tk), lambda b, i, j: (0, kv_blk(i, j)))]
        args += [sel, emat]
    return pl.pallas_call(
        functools.partial(_flash_kernel, G=G, window_tiles=window_tiles, use_sel=use_sel, per_head_kv=per_head_kv),
        grid=(B, nq, nk),
        in_specs=in_specs,
        out_specs=pl.BlockSpec((1, tq, G * HEAD_DIM), lambda b, i, j: (b, i, 0)),
        out_shape=jax.ShapeDtypeStruct((B, T, G * HEAD_DIM), F32),
        scratch_shapes=[pltpu.VMEM((G, tq, 1), F32), pltpu.VMEM((G, tq, 1), F32),
                        pltpu.VMEM((G, tq, HEAD_DIM), F32)],
        compiler_params=_cparams(("parallel", "parallel", "arbitrary")),
        name="flash",
    )(*args)


def _cmp_kernel(ck_ref, cv_ref, q_ref, pek_ref, pev_ref, wk_ref, wv_ref, bias_ref, ovl_ref,
                o_ref, sel_ref, kc_scr, vc_scr):
    T = ck_ref.shape[1]
    n_ch = T // CMP_STRIDE
    i = pl.program_id(1)
    tq = ATT_TILE

    @pl.when(i == 0)
    def _():
        for src, pe, w, dst in ((ck_ref, pek_ref, wk_ref, kc_scr), (cv_ref, pev_ref, wv_ref, vc_scr)):
            first = jnp.zeros((n_ch, HEAD_DIM), F32)
            second = jnp.zeros((n_ch, HEAD_DIM), F32)
            for l in range(CMP_STRIDE):
                x = src[0, pl.ds(l, n_ch, stride=CMP_STRIDE), :]
                l2 = CMP_STRIDE + l
                first = first + jnp.dot((x + pe[l:l + 1, :]).astype(BF16), w[l * HEAD_DIM:(l + 1) * HEAD_DIM, :],
                                        preferred_element_type=F32)
                second = second + jnp.dot((x + pe[l2:l2 + 1, :]).astype(BF16),
                                          w[l2 * HEAD_DIM:(l2 + 1) * HEAD_DIM, :], preferred_element_type=F32)
            dst[...] = (first + pltpu.roll(second, n_ch - 1, 0)).astype(BF16)

    kc = kc_scr[...]
    vc = vc_scr[...]
    t = i * tq + lax.broadcasted_iota(jnp.int32, (tq, n_ch), 0)
    n = lax.broadcasted_iota(jnp.int32, (tq, n_ch), 1)
    vis = (n * CMP_STRIDE + (CMP_LEN - 1)) <= t
    imp = jnp.zeros((tq, LANE), F32)
    ovl = ovl_ref[...]
    for g in range(NSA_HEADS):
        q = q_ref[0, :, g * HEAD_DIM:(g + 1) * HEAD_DIM].astype(BF16)
        s = lax.dot_general(q, kc, (((1,), (1,)), ((), ())), preferred_element_type=F32)
        s = s * SCALE + bias_ref[g]
        s = jnp.where(vis, s, NEG)
        m = jnp.max(s, axis=1, keepdims=True)
        e = jnp.where(vis, jnp.exp(s - m), 0.0)
        l = jnp.sum(e, axis=1, keepdims=True)
        p = e / jnp.where(l > 0.0, l, 1.0)
        pb = p.astype(BF16)
        o_ref[0, :, g * HEAD_DIM:(g + 1) * HEAD_DIM] = jnp.dot(pb, vc, preferred_element_type=F32)
        imp = imp + jnp.dot(pb, ovl, preferred_element_type=F32)
    own = (i * tq + lax.broadcasted_iota(jnp.int32, (tq, 1), 0)) // SLC_BLOCK
    sel = _rank_select(imp, own, T // SLC_BLOCK, SLC_TOPK - 1)
    sel_ref[0, 0] = sel.astype(BF16)


def _cmp_branch(za, pe_k, pe_v, w_k, w_v, bias_cmp, overlap, B, T):
    assert T // CMP_STRIDE == LANE
    tq = ATT_TILE
    col = 3 * MOBA_HEADS + NSA_HEADS
    return pl.pallas_call(
        _cmp_kernel,
        grid=(B, T // tq),
        in_specs=[
            pl.BlockSpec((1, T, HEAD_DIM), lambda b, i: (b, 0, col)),
            pl.BlockSpec((1, T, HEAD_DIM), lambda b, i: (b, 0, col + 1)),
            pl.BlockSpec((1, tq, NSA_W), lambda b, i: (b, i, 3)),
            pl.BlockSpec((CMP_LEN, HEAD_DIM), lambda b, i: (0, 0)),
            pl.BlockSpec((CMP_LEN, HEAD_DIM), lambda b, i: (0, 0)),
            pl.BlockSpec((CMP_LEN * HEAD_DIM, HEAD_DIM), lambda b, i: (0, 0)),
            pl.BlockSpec((CMP_LEN * HEAD_DIM, HEAD_DIM), lambda b, i: (0, 0)),
            pl.BlockSpec((NSA_HEADS, tq, LANE), lambda b, i: (0, i, 0)),
            pl.BlockSpec((LANE, LANE), lambda b, i: (0, 0)),
        ],
        out_specs=[pl.BlockSpec((1, tq, NSA_W), lambda b, i: (b, i, 0)),
                   pl.BlockSpec((1, 1, tq, LANE), lambda b, i: (b, 0, i, 0))],
        out_shape=[jax.ShapeDtypeStruct((B, T, NSA_W), F32),
                   jax.ShapeDtypeStruct((B, 1, T, LANE), BF16)],
        scratch_shapes=[pltpu.VMEM((LANE, HEAD_DIM), BF16), pltpu.VMEM((LANE, HEAD_DIM), BF16)],
        compiler_params=_cparams(("parallel", "arbitrary")),
        name="nsa_cmp",
    )(za, za, za, pe_k, pe_v, w_k, w_v, bias_cmp, overlap)


def _gelu_tanh(x):
    return 0.5 * x * (1.0 + jnp.tanh(math.sqrt(2.0 / math.pi) * (x + 0.044715 * (x * x * x))))


def _softplus(x):
    return jnp.maximum(x, 0.0) + jnp.log1p(jnp.exp(-jnp.abs(x)))


def _lru_kernel(x_ref, g_ref, c0_ref, h0_ref, cw_ref, cb_ref, wr_ref, br_ref, wi_ref, bi_ref, lam_ref,
                y_ref, hl_ref, cl_ref, a_scr, x_scr):
    T = x_ref.shape[1]
    x = x_ref[0]
    row = lax.broadcasted_iota(jnp.int32, (T, LRU_BLOCK), 0)
    c0 = c0_ref[0]

    def shifted(d):
        r = pltpu.roll(x, d, 0)
        for t in range(d):
            r = jnp.where(row == t, c0[t + CONV_W - 1 - d:t + CONV_W - d, :], r)
        return r

    u = cb_ref[...] + shifted(3) * cw_ref[0:1, :]
    u = u + shifted(2) * cw_ref[1:2, :]
    u = u + shifted(1) * cw_ref[2:3, :]
    u = u + x * cw_ref[3:4, :]
    ub = u.astype(BF16)
    r = jax.nn.sigmoid(jnp.dot(ub, wr_ref[0], preferred_element_type=F32) + br_ref[...])
    ig = jax.nn.sigmoid(jnp.dot(ub, wi_ref[0], preferred_element_type=F32) + bi_ref[...])
    log_a = (-LRU_C * r) * _softplus(-lam_ref[...])
    a = jnp.exp(log_a)
    b = jnp.sqrt(1.0 - jnp.exp(2.0 * log_a)) * (ig * u)
    sub = row % 8
    for d in (1, 2, 4):
        ok = sub >= d
        b = jnp.where(ok, a * pltpu.roll(b, d, 0) + b, b)
        a = jnp.where(ok, a * pltpu.roll(a, d, 0), a)
    a_scr[...] = a
    x_scr[...] = b

    def group(k, carry):
        s = pl.multiple_of(k * 8, 8)
        h = x_scr[pl.ds(s, 8), :] + a_scr[pl.ds(s, 8), :] * carry
        x_scr[pl.ds(s, 8), :] = h
        return h[7:8, :]

    h_last = lax.fori_loop(0, T // 8, group, h0_ref[0], unroll=8)
    y_ref[0] = x_scr[...] * _gelu_tanh(g_ref[0])
    hl_ref[0] = h_last
    cl_ref[0] = x[T - (CONV_W - 1):, :]


def _rglru(zb, conv0, h0, conv_w, conv_b, w_r, b_r, w_i, b_i, lam, B, T):
    nblk = LRU_BLOCKS
    vec = lambda a: a.reshape(1, LRU_WIDTH)
    row_spec = pl.BlockSpec((1, LRU_BLOCK), lambda b, n: (0, n))
    return pl.pallas_call(
        _lru_kernel,
        grid=(B, nblk),
        in_specs=[
            pl.BlockSpec((1, T, LRU_BLOCK), lambda b, n: (b, 0, n)),
            pl.BlockSpec((1, T, LRU_BLOCK), lambda b, n: (b, 0, nblk + n)),
            pl.BlockSpec((1, CONV_W - 1, LRU_BLOCK), lambda b, n: (b, 0, n)),
            pl.BlockSpec((1, 1, LRU_BLOCK), lambda b, n: (b, 0, n)),
            pl.BlockSpec((CONV_W, LRU_BLOCK), lambda b, n: (0, n)),
            row_spec,
            pl.BlockSpec((1, LRU_BLOCK, LRU_BLOCK), lambda b, n: (n, 0, 0)),
            row_spec,
            pl.BlockSpec((1, LRU_BLOCK, LRU_BLOCK), lambda b, n: (n, 0, 0)),
            row_spec,
            row_spec,
        ],
        out_specs=[pl.BlockSpec((1, T, LRU_BLOCK), lambda b, n: (b, 0, n)),
                   pl.BlockSpec((1, 1, LRU_BLOCK), lambda b, n: (b, 0, n)),
                   pl.BlockSpec((1, CONV_W - 1, LRU_BLOCK), lambda b, n: (b, 0, n))],
        out_shape=[jax.ShapeDtypeStruct((B, T, LRU_WIDTH), F32),
                   jax.ShapeDtypeStruct((B, 1, LRU_WIDTH), F32),
                   jax.ShapeDtypeStruct((B, CONV_W - 1, LRU_WIDTH), F32)],
        scratch_shapes=[pltpu.VMEM((T, LRU_BLOCK), F32), pltpu.VMEM((T, LRU_BLOCK), F32)],
        compiler_params=_cparams(("parallel", "parallel")),
        name="rglru",
    )(zb, zb, conv0, h0.reshape(B, 1, LRU_WIDTH), conv_w, vec(conv_b), w_r.astype(BF16), vec(b_r),
      w_i.astype(BF16), vec(b_i), vec(lam))


def _moe_kernel(eid_ref, nt_ref, x_ref, wg_ref, wu_ref, wd_ref, cw_ref, o_ref, wg_scr, wu_scr, wd_scr):
    t = pl.program_id(0)
    live = t < nt_ref[0]
    new_expert = (t == 0) | (eid_ref[t] != eid_ref[jnp.maximum(t - 1, 0)])

    @pl.when(live & new_expert)
    def _():
        wg_scr[...] = wg_ref[0, 0].astype(BF16)
        wu_scr[...] = wu_ref[0, 0].astype(BF16)
        wd_scr[...] = wd_ref[0, 0].astype(BF16)

    @pl.when(live)
    def _():
        x = x_ref[...]
        g = jnp.dot(x, wg_scr[...], preferred_element_type=F32)
        u = jnp.dot(x, wu_scr[...], preferred_element_type=F32)
        act = (g * jax.nn.sigmoid(g)) * u
        y = jnp.dot(act.astype(BF16), wd_scr[...], preferred_element_type=F32)
        o_ref[...] = y * cw_ref[...]

    @pl.when(t >= nt_ref[0])
    def _():
        o_ref[...] = jnp.zeros(o_ref.shape, F32)


def _moe_ffn(x_pad, cw_pad, tile_eid, n_tiles, l, w_g, w_u, w_d):
    P, D = x_pad.shape
    tm = MOE_TILE
    grid_spec = pltpu.PrefetchScalarGridSpec(
        num_scalar_prefetch=2,
        grid=(P // tm,),
        in_specs=[
            pl.BlockSpec((tm, D), lambda t, eid, nt: (t, 0)),
            pl.BlockSpec((1, 1, D, D_EXPERT), lambda t, eid, nt: (l, eid[t], 0, 0)),
            pl.BlockSpec((1, 1, D, D_EXPERT), lambda t, eid, nt: (l, eid[t], 0, 0), pipeline_mode=pl.Buffered(1)),
            pl.BlockSpec((1, 1, D_EXPERT, D), lambda t, eid, nt: (l, eid[t], 0, 0), pipeline_mode=pl.Buffered(1)),
            pl.BlockSpec((tm, 1), lambda t, eid, nt: (t, 0)),
        ],
        out_specs=pl.BlockSpec((tm, D), lambda t, eid, nt: (t, 0)),
        scratch_shapes=[pltpu.VMEM((D, D_EXPERT), BF16), pltpu.VMEM((D, D_EXPERT), BF16),
                        pltpu.VMEM((D_EXPERT, D), BF16)],
    )
    return pl.pallas_call(
        _moe_kernel,
        grid_spec=grid_spec,
        out_shape=jax.ShapeDtypeStruct((P, D), F32),
        compiler_params=_cparams(("arbitrary",)),
        name="moe_ffn",
    )(tile_eid, n_tiles, x_pad, w_g, w_u, w_d, cw_pad)


def _route(logits):
    probs = jax.nn.softmax(logits.astype(F32), axis=-1)
    grp = probs.reshape(-1, N_GROUPS, EXPERTS_PER_GROUP)
    pairs = [grp[..., a] + grp[..., b] for a in range(EXPERTS_PER_GROUP) for b in range(a + 1, EXPERTS_PER_GROUP)]
    g_score = functools.reduce(jnp.maximum, pairs)
    g_best = jnp.argmax(g_score, axis=-1)
    pick = g_best[:, None, None] == jnp.arange(N_GROUPS)[None, :, None]
    in_grp = jnp.sum(jnp.where(pick, grp, 0.0), axis=1)
    i1 = jnp.argmax(in_grp, axis=-1)
    first = jnp.arange(EXPERTS_PER_GROUP)[None, :] == i1[:, None]
    w1 = jnp.max(in_grp, axis=-1)
    rest = jnp.where(first, -jnp.inf, in_grp)
    i2 = jnp.argmax(rest, axis=-1)
    w2 = jnp.max(rest, axis=-1)
    w_top = jnp.stack([w1, w2], axis=-1)
    w_top = w_top / jnp.sum(w_top, axis=-1, keepdims=True)
    e_idx = g_best[:, None] * EXPERTS_PER_GROUP + jnp.stack([i1, i2], axis=-1)
    return e_idx.astype(jnp.int32), w_top


def _moe_prompt(h2, logits, l, w_g, w_u, w_d):
    N, D = h2.shape
    tm = MOE_TILE
    e_idx, w_top = _route(logits)
    flat_e = e_idx.reshape(-1)
    onehot = (flat_e[:, None] == jnp.arange(N_EXPERTS)[None, :]).astype(jnp.int32)
    within = jnp.sum(onehot * jnp.cumsum(onehot, axis=0), axis=1) - 1
    sizes = jnp.sum(onehot, axis=0)
    padded = ((sizes + tm - 1) // tm) * tm
    pend = jnp.cumsum(padded)
    pstart = pend - padded
    dest = jnp.sum(onehot * pstart[None, :], axis=1) + within
    P = 2 * N + N_EXPERTS * tm
    fields = jnp.stack([(jnp.arange(2 * N, dtype=jnp.int32) // TOP_K).astype(F32), w_top.reshape(-1)], axis=1)
    packed = jnp.zeros((P, 2), F32).at[dest].set(fields)
    src_tok = packed[:, 0].astype(jnp.int32)
    cw_pad = packed[:, 1]
    x_pad = h2[src_tok]
    tile_start = jnp.arange(P // tm, dtype=jnp.int32) * tm
    tile_eid = jnp.minimum(jnp.sum(tile_start[:, None] >= pend[None, :], axis=1), N_EXPERTS - 1).astype(jnp.int32)
    n_tiles = (pend[-1] // tm).astype(jnp.int32).reshape(1)
    out = _moe_ffn(x_pad, cw_pad.reshape(P, 1), tile_eid, n_tiles, l, w_g, w_u, w_d)
    pos = dest.reshape(N, TOP_K)
    return out[pos[:, 0]], out[pos[:, 1]]


def _t5_bucket(dist):
    n = jnp.maximum(dist, 0)
    exact = T5_BUCKETS // 2
    nf = jnp.maximum(n, 1).astype(F32)
    large = exact + (jnp.log(nf / exact) / math.log(T5_MAX_DIST / exact) * (T5_BUCKETS - exact)).astype(jnp.int32)
    return jnp.where(n < exact, n, jnp.minimum(large, T5_BUCKETS - 1))


def _bias_lookup(tab, dist):
    bucket = _t5_bucket(dist)
    out = jnp.zeros((tab.shape[0],) + bucket.shape, F32)
    for b in range(T5_BUCKETS):
        out = jnp.where(bucket[None] == b, tab[:, b].reshape((-1,) + (1,) * bucket.ndim), out)
    return out


def _rms(x, g):
    return x * lax.rsqrt(jnp.mean(x * x, axis=-1, keepdims=True) + EPS) * g


def _split(z, widths):
    outs, s = [], 0
    for w in widths:
        outs.append(z[..., s:s + w])
        s += w
    return outs


def _prompt_tables(t5_bias, T):
    t = ATT_TILE
    i = jnp.arange(t)
    dist = jnp.arange(2)[:, None, None] * t + i[None, :, None] - i[None, None, :]
    assert t >= T5_MAX_DIST
    far = jnp.broadcast_to(t5_bias.T[:, None, None, None, T5_BUCKETS - 1], (t5_bias.shape[1], 1, t, t))
    bias_tiles = jnp.concatenate([_bias_lookup(t5_bias.T, dist), far], axis=1)
    pos = jnp.arange(T)
    lane = jnp.arange(LANE)
    e_moba = (pos[None, :] // MOBA_BLOCK == lane[:, None]).astype(BF16)
    e_slc = (pos[None, :] // SLC_BLOCK == lane[:, None]).astype(BF16)
    d_cmp = pos[:, None] - (lane[None, :] * CMP_STRIDE + CMP_LEN - 1)
    bias_cmp = _bias_lookup(t5_bias.T[MOBA_HEADS:], d_cmp)
    c_start = lane * CMP_STRIDE
    s_start = lane * SLC_BLOCK
    n_cmp = T // CMP_STRIDE - 1
    overlap = ((c_start[:, None] < s_start[None, :] + SLC_BLOCK) & (c_start[:, None] + CMP_LEN > s_start[None, :])
               & (lane[:, None] < n_cmp) & (lane[None, :] < T // SLC_BLOCK)).astype(BF16)
    return bias_tiles, e_moba, e_slc, bias_cmp, overlap


def _mixer_prompt(x, h, B, T, mod3, lp, tabs, w_router_pad, b_router_pad, kv_prev):
    N, D = x.shape
    bias_tiles, e_moba, e_slc, bias_cmp, overlap = tabs
    za, new_kv = _za_proj(h, lp['w_in'], lp['l'], 1024, kv_prev)
    za = za.reshape(B, T, ZA_W)
    zb = _matmul_w32(h, lp['w_in'], lp['l'], COL_LX, ZB_W, ZB_TILE, 1024).reshape(B, T, ZB_W)
    win = za[:, T - WINDOW:, COL_NG - WIN_DIM:COL_NG]

    sel_moba = _moba_select(za, B, T)
    o_a = _flash(za, bias_tiles, B=B, T=T, q_col=0, k_col=1, v_col=2, bias_blk=0, per_head_kv=True,
                 sel=sel_moba, emat=e_moba)
    o_cmp, sel_slc = _cmp_branch(za, lp['cmp_pos_k'], lp['cmp_pos_v'], lp['w_cmp_k'].reshape(-1, HEAD_DIM),
                                 lp['w_cmp_v'].reshape(-1, HEAD_DIM), bias_cmp, overlap, B, T)
    o_slc = _flash(za, bias_tiles, B=B, T=T, q_col=3, k_col=18, v_col=19, bias_blk=1, per_head_kv=False,
                   sel=sel_slc, emat=e_slc)
    o_win = _flash(za, bias_tiles, B=B, T=T, q_col=3, k_col=20, v_col=21, bias_blk=1, per_head_kv=False,
                   window_tiles=WINDOW // ATT_TILE)
    o_c, h_last, conv_last = _rglru(zb, jnp.zeros((B, CONV_W - 1, LRU_WIDTH), F32), jnp.zeros((B, LRU_WIDTH), F32),
                                    lp['conv_w'], lp['conv_b'], lp['w_rg'], lp['b_rg'], lp['w_ig'], lp['b_ig'],
                                    lp['lru_lambda'], B, T)
    flat = lambda a: a.reshape(N, a.shape[-1])
    merged = _merge(flat(o_a), flat(o_cmp), flat(o_slc), flat(o_win), flat(za), flat(o_c), flat(zb),
                    lp['w_br_moba'], lp['w_br_nsa'], lp['w_br_lru'], 256)
    x, h2, logits = _outproj(merged, lp['w_out'], x, mod3, lp['norm_ffn'].reshape(1, D),
                             w_router_pad, b_router_pad, 256, T, 0)
    return x, h2, logits, new_kv, win, h_last.reshape(B, LRU_WIDTH), conv_last


def _mixer_decode(l, x, mod, lp, cache_kv, cache_win, page_table, h0, conv0, dtabs, t5_bias, w_router_pad,
                  b_router_pad):
    B, D = x.shape
    pad = lambda a: jnp.pad(a, ((0, DEC_ROWS - B),) + ((0, 0),) * (a.ndim - 1))
    sh1, sc1 = mod[:, :D], mod[:, D:2 * D]
    h = pad((_rms(x, lp['norm_mix']) * (1.0 + sc1) + sh1).astype(BF16))
    za = _matmul_w32(h, lp['w_in'], l, 0, ZA_W, ZA_TILE, DEC_ROWS)
    zb = _matmul_w32(h, lp['w_in'], l, COL_LX, ZB_W, ZB_TILE, DEC_ROWS)
    new_kv = jnp.concatenate([za[:B, MOBA_W:3 * MOBA_W], za[:B, 3 * MOBA_W + NSA_W:COL_NG - WIN_DIM]], axis=-1)
    n_win = cache_win.shape[2]
    win = jnp.concatenate([cache_win[l], za[:B, None, COL_NG - WIN_DIM:COL_NG]], axis=1)
    win = win[:, win.shape[1] - min(WINDOW, n_win + 1):]
    cmp_w = (lp['cmp_pos_k'], lp['cmp_pos_v'], lp['w_cmp_k'], lp['w_cmp_v'])
    o_a, o_cmp, o_slc, o_win = _decode_attention(l, za, cache_kv, cache_win, page_table, cmp_w, dtabs, t5_bias)
    o_c, h_new, conv_t = _lru_step(zb, conv0.transpose(1, 0, 2), h0, lp['conv_w'], lp['conv_b'], lp['w_rg'],
                                   lp['b_rg'], lp['w_ig'], lp['b_ig'], lp['lru_lambda'])
    merged = _merge(o_a, o_cmp, o_slc, o_win, za, pad(o_c), zb, lp['w_br_moba'], lp['w_br_nsa'], lp['w_br_lru'],
                    DEC_ROWS)
    mod3 = pad(mod).reshape(DEC_ROWS, 6, D).transpose(1, 0, 2)
    x, h2, logits = _outproj(merged, lp['w_out'], pad(x), mod3, lp['norm_ffn'].reshape(1, D), w_router_pad,
                             b_router_pad, DEC_ROWS, DEC_ROWS, 0)
    return x[:B], h2, logits, new_kv, win, h_new, conv_t.transpose(1, 0, 2)


SCAN_PAGES = 32
DEC_ROWS = 16


def _page_scan_kernel(pt_ref, *refs):
    del pt_ref
    ck_refs, cv_refs, mk_refs = refs[:SCAN_PAGES], refs[SCAN_PAGES:2 * SCAN_PAGES], refs[2 * SCAN_PAGES:3 * SCAN_PAGES]
    pek_ref, pev_ref, wk_ref, wv_ref, abk_ref, abv_ref, km_ref = refs[3 * SCAN_PAGES:]
    chunks = PAGE_SIZE // CMP_STRIDE
    for src, pe, w, dst in ((ck_refs, pek_ref, wk_ref, abk_ref), (cv_refs, pev_ref, wv_ref, abv_ref)):
        first = jnp.zeros((SCAN_PAGES * chunks, HEAD_DIM), F32)
        second = jnp.zeros((SCAN_PAGES * chunks, HEAD_DIM), F32)
        for l in range(CMP_STRIDE):
            x = jnp.concatenate([r[0, 0, pl.ds(l, chunks, stride=CMP_STRIDE), :] for r in src], axis=0)
            l2 = CMP_STRIDE + l
            first = first + jnp.dot((x + pe[l:l + 1, :]).astype(BF16), w[l], preferred_element_type=F32)
            second = second + jnp.dot((x + pe[l2:l2 + 1, :]).astype(BF16), w[l2], preferred_element_type=F32)
        dst[0, :, 0:HEAD_DIM] = first
        dst[0, :, HEAD_DIM:2 * HEAD_DIM] = second
    sums = [jnp.sum(r[0, 0], axis=0, keepdims=True) for r in mk_refs]
    per_blk = MOBA_BLOCK // PAGE_SIZE
    means = [functools.reduce(lambda a, b: a + b, sums[j * per_blk:(j + 1) * per_blk])
             for j in range(SCAN_PAGES // per_blk)]
    km_ref[0] = jnp.concatenate(means, axis=0) * (1.0 / MOBA_BLOCK)


def _page_scan(cache_kv, l, page_table, pe_k, pe_v, w_k, w_v):
    B, n_pages = page_table.shape
    steps = n_pages // SCAN_PAGES
    chunks = PAGE_SIZE // CMP_STRIDE
    blocks = SCAN_PAGES * PAGE_SIZE // MOBA_BLOCK
    cmp_col = 2 * MOBA_W // HEAD_DIM

    def cmp_spec(j, col):
        return pl.BlockSpec((1, 1, PAGE_SIZE, HEAD_DIM), lambda b, s, pt: (l, pt[b, s * SCAN_PAGES + j], 0, col))

    def mk_spec(j):
        return pl.BlockSpec((1, 1, PAGE_SIZE, MOBA_W), lambda b, s, pt: (l, pt[b, s * SCAN_PAGES + j], 0, 0))

    const = lambda a: pl.BlockSpec(a.shape, lambda b, s, pt: (0,) * a.ndim)
    grid_spec = pltpu.PrefetchScalarGridSpec(
        num_scalar_prefetch=1,
        grid=(B, steps),
        in_specs=([cmp_spec(j, cmp_col) for j in range(SCAN_PAGES)]
                  + [cmp_spec(j, cmp_col + 1) for j in range(SCAN_PAGES)] + [mk_spec(j) for j in range(SCAN_PAGES)]
                  + [const(pe_k), const(pe_v), const(w_k), const(w_v)]),
        out_specs=[pl.BlockSpec((1, SCAN_PAGES * chunks, 2 * HEAD_DIM), lambda b, s, pt: (b, s, 0)),
                   pl.BlockSpec((1, SCAN_PAGES * chunks, 2 * HEAD_DIM), lambda b, s, pt: (b, s, 0)),
                   pl.BlockSpec((1, blocks, MOBA_W), lambda b, s, pt: (b, s, 0))],
    )
    n_ch = n_pages * chunks
    return pl.pallas_call(
        _page_scan_kernel,
        grid_spec=grid_spec,
        out_shape=[jax.ShapeDtypeStruct((B, n_ch, 2 * HEAD_DIM), F32),
                   jax.ShapeDtypeStruct((B, n_ch, 2 * HEAD_DIM), F32),
                   jax.ShapeDtypeStruct((B, steps * blocks, MOBA_W), F32)],
        compiler_params=_cparams(("parallel", "arbitrary")),
        name="page_scan",
    )(page_table, *([cache_kv] * (3 * SCAN_PAGES)), pe_k, pe_v, w_k, w_v)


def _topk_lanes(score, k):
    lane = lax.broadcasted_iota(jnp.int32, score.shape, 1)
    out = jnp.zeros(score.shape, jnp.int32)
    for r in range(k):
        m = jnp.max(score, axis=1, keepdims=True)
        idx = jnp.min(jnp.where(score == m, lane, score.shape[1]), axis=1, keepdims=True)
        out = jnp.where(lane == r, idx, out)
        score = jnp.where(lane == idx, -jnp.inf, score)
    return out


def _head_rows(q):
    rows = [q[:, g * HEAD_DIM:(g + 1) * HEAD_DIM] for g in range(NSA_HEADS)]
    return jnp.concatenate(rows + [jnp.zeros((8 - NSA_HEADS, HEAD_DIM), q.dtype)], axis=0)


def _dec_select_kernel(abk_ref, abv_ref, km_ref, mq_ref, nq_ref, bias_ref, ovl_ref, ocmp_ref, tm_ref, ts_ref,
                       *, past):
    b = pl.program_id(0)
    n_ch = abk_ref.shape[1]
    nb = km_ref.shape[1]
    row = lax.broadcasted_iota(jnp.int32, (8, LANE), 0)
    mq = mq_ref[pl.ds(b, 1), :]
    sc = jnp.full((8, LANE), NEG, F32)
    for h in range(MOBA_HEADS):
        hs = slice(h * HEAD_DIM, (h + 1) * HEAD_DIM)
        qh = jnp.broadcast_to(mq[:, hs], (8, HEAD_DIM)).astype(BF16)
        s = lax.dot_general(qh, km_ref[0, :, hs].astype(BF16), (((1,), (1,)), ((), ())), preferred_element_type=F32)
        s = jnp.concatenate([s, jnp.full((8, LANE - nb), NEG, F32)], axis=1)
        sc = jnp.where(row == h, s, sc)
    tm_ref[0] = _topk_lanes(sc, MOBA_TOPK)
    abk = abk_ref[0]
    abv = abv_ref[0]
    kc = (abk[:, :HEAD_DIM] + pltpu.roll(abk[:, HEAD_DIM:], n_ch - 1, 0)).astype(BF16)
    vc = (abv[:, :HEAD_DIM] + pltpu.roll(abv[:, HEAD_DIM:], n_ch - 1, 0)).astype(BF16)
    q4 = _head_rows(nq_ref[pl.ds(b, 1), :]).astype(BF16)
    s = lax.dot_general(q4, kc, (((1,), (1,)), ((), ())), preferred_element_type=F32) * SCALE + bias_ref[...]
    n = lax.broadcasted_iota(jnp.int32, (8, n_ch), 1)
    vis = (n < n_ch - 1) & (n * CMP_STRIDE + (CMP_LEN - 1) <= past)
    s = jnp.where(vis, s, NEG)
    m = jnp.max(s, axis=1, keepdims=True)
    e = jnp.where(vis, jnp.exp(s - m), 0.0)
    den = jnp.sum(e, axis=1, keepdims=True)
    pb = (e / jnp.where(den > 0.0, den, 1.0)).astype(BF16)
    ocmp_ref[0] = jnp.dot(pb, vc, preferred_element_type=F32)
    imp = jnp.dot(pb, ovl_ref[...], preferred_element_type=F32)
    rows = lax.broadcasted_iota(jnp.int32, imp.shape, 0)
    imp = jnp.sum(jnp.where(rows < NSA_HEADS, imp, 0.0), axis=0, keepdims=True)
    ts_ref[0] = _topk_lanes(jnp.broadcast_to(imp, (8, imp.shape[1])), SLC_TOPK - 1)[:, :LANE]


def _dec_select(abk, abv, kmean, za_s, bias_cmp, overlap, past):
    B, n_ch, _ = abk.shape
    full = lambda a: pl.BlockSpec(a.shape, lambda b: (0,) * a.ndim)
    per_b = lambda a: pl.BlockSpec((1,) + a.shape[1:], lambda b: (b, 0, 0))
    out = jax.ShapeDtypeStruct((B, 8, LANE), F32)
    outi = jax.ShapeDtypeStruct((B, 8, LANE), jnp.int32)
    return pl.pallas_call(
        functools.partial(_dec_select_kernel, past=past),
        grid=(B,),
        in_specs=[per_b(abk), per_b(abv), per_b(kmean),
                  pl.BlockSpec((DEC_ROWS, MOBA_W), lambda b: (0, 0)),
                  pl.BlockSpec((DEC_ROWS, NSA_W), lambda b: (0, 3)),
                  full(bias_cmp), full(overlap)],
        out_specs=[pl.BlockSpec((1, 8, LANE), lambda b: (b, 0, 0))] * 3,
        out_shape=[out, outi, outi],
        compiler_params=_cparams(("parallel",)),
        name="dec_select",
    )(abk, abv, kmean, za_s, za_s, bias_cmp, overlap)


def _softmax_parts(scores):
    m = functools.reduce(jnp.maximum, [jnp.max(s, axis=1, keepdims=True) for s in scores])
    es = [jnp.exp(s - m) for s in scores]
    den = functools.reduce(lambda a, b: a + b, [jnp.sum(e, axis=1, keepdims=True) for e in es])
    return [e / den for e in es]


def _dec_attend_kernel(pm_ref, ps_ref, hs_ref, *refs, n_moba, n_slc, n_win):
    del pm_ref, ps_ref, hs_ref
    mk, mv = refs[:n_moba], refs[n_moba:2 * n_moba]
    sk, sv = refs[2 * n_moba:2 * n_moba + n_slc], refs[2 * n_moba + n_slc:2 * (n_moba + n_slc)]
    (kw_ref, vw_ref, mq_ref, mkn_ref, mvn_ref, nq_ref, skn_ref, svn_ref, wkn_ref, wvn_ref, bm_ref, bs_ref, bw_ref,
     oa_ref, oslc_ref, owin_ref) = refs[2 * (n_moba + n_slc):]
    b = pl.program_id(0)
    h = pl.program_id(1)
    nt = (((1,), (1,)), ((), ()))
    rnd = lambda a: a.astype(BF16).astype(F32)

    q = mq_ref[pl.ds(b, 1), :]
    q8 = jnp.broadcast_to(q, (8, HEAD_DIM)).astype(BF16)
    bm = bm_ref[0]
    scores = [lax.dot_general(q8, mk[j][0, 0].astype(BF16), nt, preferred_element_type=F32)[0:1] * SCALE
              + bm[j:j + 1] for j in range(n_moba)]
    s_new = jnp.sum(rnd(q) * rnd(mkn_ref[pl.ds(b, 1), :]), axis=1, keepdims=True) * SCALE + bm[n_moba:n_moba + 1, 0:1]
    probs = _softmax_parts(scores + [s_new])
    o = rnd(probs[-1]) * rnd(mvn_ref[pl.ds(b, 1), :])
    for j in range(n_moba):
        pj = jnp.broadcast_to(probs[j], (8, PAGE_SIZE)).astype(BF16)
        o = o + jnp.dot(pj, mv[j][0, 0].astype(BF16), preferred_element_type=F32)[0:1]
    oa_ref[0] = jnp.broadcast_to(o, (8, HEAD_DIM))

    @pl.when(h == 0)
    def _():
        q4f = _head_rows(nq_ref[pl.ds(b, 1), :])
        q4 = q4f.astype(BF16)
        scores = [lax.dot_general(q4, sk[j][0, 0].astype(BF16), nt, preferred_element_type=F32) * SCALE + bs_ref[0, j]
                  for j in range(n_slc)]
        s_new = (jnp.sum(rnd(q4f) * rnd(skn_ref[pl.ds(b, 1), :]), axis=1, keepdims=True) * SCALE
                 + bs_ref[0, n_slc][:, 0:1])
        probs = _softmax_parts(scores + [s_new])
        o = rnd(probs[-1]) * rnd(svn_ref[pl.ds(b, 1), :])
        for j in range(n_slc):
            o = o + jnp.dot(probs[j].astype(BF16), sv[j][0, 0].astype(BF16), preferred_element_type=F32)
        oslc_ref[0] = o
        s = lax.dot_general(q4, kw_ref[0, 0].astype(BF16), nt, preferred_element_type=F32) * SCALE + bw_ref[:, :n_win]
        i = lax.broadcasted_iota(jnp.int32, (8, n_win), 1)
        s = jnp.where(n_win - i < WINDOW, s, NEG)
        s_new = (jnp.sum(rnd(q4f) * rnd(wkn_ref[pl.ds(b, 1), :]), axis=1, keepdims=True) * SCALE
                 + bw_ref[:, n_win:n_win + 1])
        p_win, p_new = _softmax_parts([s, s_new])
        p_win = jnp.where(n_win - i < WINDOW, p_win, 0.0)
        owin_ref[0] = (rnd(p_new) * rnd(wvn_ref[pl.ds(b, 1), :])
                       + jnp.dot(p_win.astype(BF16), vw_ref[0, 0].astype(BF16), preferred_element_type=F32))


def _dec_attend(cache_kv, cache_win, l, za_s, pages_moba, pages_slc, halves_slc, bias_moba, bias_slc, bias_win, B):
    n_moba = MOBA_TOPK * MOBA_BLOCK // PAGE_SIZE
    n_slc = SLC_TOPK - 1
    n_win = cache_win.shape[2]

    def moba_spec(j, col0):
        return pl.BlockSpec((1, 1, PAGE_SIZE, HEAD_DIM),
                            lambda b, h, pm, ps, hs: (l, pm[(b * MOBA_HEADS + h) * n_moba + j], 0, col0 + h))

    def slc_spec(j, col):
        return pl.BlockSpec((1, 1, SLC_BLOCK, HEAD_DIM),
                            lambda b, h, pm, ps, hs: (l, ps[b * n_slc + j], hs[b * n_slc + j], col))

    zcol = lambda w, c: pl.BlockSpec((DEC_ROWS, w), lambda b, h, pm, ps, hs: (0, c))
    zhead = lambda c0: pl.BlockSpec((DEC_ROWS, HEAD_DIM), lambda b, h, pm, ps, hs: (0, c0 + h))
    slc_col = (2 * MOBA_W + 2 * HEAD_DIM) // HEAD_DIM
    in_specs = ([moba_spec(j, 0) for j in range(n_moba)] + [moba_spec(j, MOBA_HEADS) for j in range(n_moba)]
                + [slc_spec(j, slc_col) for j in range(n_slc)] + [slc_spec(j, slc_col + 1) for j in range(n_slc)]
                + [pl.BlockSpec((1, 1, n_win, HEAD_DIM), lambda b, h, pm, ps, hs: (l, b, 0, 0)),
                   pl.BlockSpec((1, 1, n_win, HEAD_DIM), lambda b, h, pm, ps, hs: (l, b, 0, 1)),
                   zhead(0), zhead(MOBA_HEADS), zhead(2 * MOBA_HEADS), zcol(NSA_W, 3),
                   zcol(HEAD_DIM, 18), zcol(HEAD_DIM, 19), zcol(HEAD_DIM, 20), zcol(HEAD_DIM, 21),
                   pl.BlockSpec((1, 8, LANE), lambda b, h, pm, ps, hs: (b * MOBA_HEADS + h, 0, 0)),
                   pl.BlockSpec((1, n_slc + 1, 8, SLC_BLOCK), lambda b, h, pm, ps, hs: (b, 0, 0, 0)),
                   pl.BlockSpec(bias_win.shape, lambda b, h, pm, ps, hs: (0, 0))])
    grid_spec = pltpu.PrefetchScalarGridSpec(
        num_scalar_prefetch=3,
        grid=(B, MOBA_HEADS),
        in_specs=in_specs,
        out_specs=[pl.BlockSpec((1, 8, HEAD_DIM), lambda b, h, pm, ps, hs: (b * MOBA_HEADS + h, 0, 0)),
                   pl.BlockSpec((1, 8, HEAD_DIM), lambda b, h, pm, ps, hs: (b, 0, 0)),
                   pl.BlockSpec((1, 8, HEAD_DIM), lambda b, h, pm, ps, hs: (b, 0, 0))],
    )
    return pl.pallas_call(
        functools.partial(_dec_attend_kernel, n_moba=n_moba, n_slc=n_slc, n_win=n_win),
        grid_spec=grid_spec,
        out_shape=[jax.ShapeDtypeStruct((B * MOBA_HEADS, 8, HEAD_DIM), F32),
                   jax.ShapeDtypeStruct((B, 8, HEAD_DIM), F32), jax.ShapeDtypeStruct((B, 8, HEAD_DIM), F32)],
        compiler_params=_cparams(("parallel", "arbitrary")),
        name="dec_attend",
    )(pages_moba, pages_slc, halves_slc, *([cache_kv] * (2 * n_moba + 2 * n_slc)), cache_win, cache_win,
      za_s, za_s, za_s, za_s, za_s, za_s, za_s, za_s, bias_moba, bias_slc, bias_win)


def _lru_step_kernel(x_ref, g_ref, c0_ref, h0_ref, cw_ref, cb_ref, wr_ref, br_ref, wi_ref, bi_ref, lam_ref,
                     y_ref, h_ref, cl_ref):
    B = h0_ref.shape[0]
    x = x_ref[0:B, :]
    u = cb_ref[...] + c0_ref[0] * cw_ref[0:1, :]
    u = u + c0_ref[1] * cw_ref[1:2, :]
    u = u + c0_ref[2] * cw_ref[2:3, :]
    u = u + x * cw_ref[3:4, :]
    ub = u.astype(BF16)
    r = jax.nn.sigmoid(jnp.dot(ub, wr_ref[0], preferred_element_type=F32) + br_ref[...])
    ig = jax.nn.sigmoid(jnp.dot(ub, wi_ref[0], preferred_element_type=F32) + bi_ref[...])
    log_a = (-LRU_C * r) * _softplus(-lam_ref[...])
    h = jnp.exp(log_a) * h0_ref[...] + jnp.sqrt(1.0 - jnp.exp(2.0 * log_a)) * (ig * u)
    h_ref[...] = h
    y_ref[...] = h * _gelu_tanh(g_ref[0:B, :])
    cl_ref[0] = c0_ref[1]
    cl_ref[1] = c0_ref[2]
    cl_ref[2] = x


def _lru_step(zb_s, conv0_t, h0, conv_w, conv_b, w_r, b_r, w_i, b_i, lam):
    B = h0.shape[0]
    nblk = LRU_BLOCKS
    vec = lambda a: a.reshape(1, LRU_WIDTH)
    row_spec = pl.BlockSpec((1, LRU_BLOCK), lambda n: (0, n))
    bw = lambda: pl.BlockSpec((B, LRU_BLOCK), lambda n: (0, n))
    return pl.pallas_call(
        _lru_step_kernel,
        grid=(nblk,),
        in_specs=[pl.BlockSpec((DEC_ROWS, LRU_BLOCK), lambda n: (0, n)),
                  pl.BlockSpec((DEC_ROWS, LRU_BLOCK), lambda n: (0, nblk + n)),
                  pl.BlockSpec((CONV_W - 1, B, LRU_BLOCK), lambda n: (0, 0, n)), bw(),
                  pl.BlockSpec((CONV_W, LRU_BLOCK), lambda n: (0, n)), row_spec,
                  pl.BlockSpec((1, LRU_BLOCK, LRU_BLOCK), lambda n: (n, 0, 0)), row_spec,
                  pl.BlockSpec((1, LRU_BLOCK, LRU_BLOCK), lambda n: (n, 0, 0)), row_spec, row_spec],
        out_specs=[bw(), bw(), pl.BlockSpec((CONV_W - 1, B, LRU_BLOCK), lambda n: (0, 0, n))],
        out_shape=[jax.ShapeDtypeStruct((B, LRU_WIDTH), F32), jax.ShapeDtypeStruct((B, LRU_WIDTH), F32),
                   jax.ShapeDtypeStruct((CONV_W - 1, B, LRU_WIDTH), F32)],
        compiler_params=_cparams(("parallel",)),
        name="lru_step",
    )(zb_s, zb_s, conv0_t, h0, conv_w, vec(conv_b), w_r, vec(b_r), w_i, vec(b_i), vec(lam))


def _bias_heads(tab, dist):
    bucket = _t5_bucket(dist)
    out = jnp.zeros(bucket.shape, F32)
    shape = (1, -1) + (1,) * (bucket.ndim - 2)
    for bkt in range(T5_BUCKETS):
        out = jnp.where(bucket == bkt, tab[:, bkt].reshape(shape), out)
    return out


def _decode_tables(t5_bias, past, n_win):
    tab_n = t5_bias.T[MOBA_HEADS:]
    n_ch = past // CMP_STRIDE
    n = jnp.arange(n_ch)
    bias_cmp = jnp.pad(_bias_lookup(tab_n, past - (n * CMP_STRIDE + CMP_LEN - 1)), ((0, 8 - NSA_HEADS), (0, 0)))
    s_start = jnp.arange(past // SLC_BLOCK) * SLC_BLOCK
    c_start = n * CMP_STRIDE
    overlap = ((c_start[:, None] < s_start[None, :] + SLC_BLOCK) & (c_start[:, None] + CMP_LEN > s_start[None, :])
               & (n[:, None] < n_ch - 1)).astype(BF16)
    d_win = jnp.concatenate([n_win - jnp.arange(n_win), jnp.zeros((LANE,), jnp.int32)])
    bias_win = jnp.pad(_bias_lookup(tab_n, d_win), ((0, 8 - NSA_HEADS), (0, 0)))
    return bias_cmp, overlap, bias_win


def _decode_attention(l, za_s, cache_kv, cache_win, page_table, lp_cmp, dtabs, t5_bias):
    B, n_pages = page_table.shape
    past = n_pages * PAGE_SIZE
    bias_cmp, overlap, bias_win = dtabs
    pe_k, pe_v, w_k, w_v = lp_cmp
    abk, abv, kmean = _page_scan(cache_kv, l, page_table, pe_k, pe_v, w_k, w_v)
    o_cmp, top_m, top_s = _dec_select(abk, abv, kmean, za_s, bias_cmp, overlap, past)
    top_m = top_m[:, :MOBA_HEADS, :MOBA_TOPK]
    top_s = top_s[:, 0, :SLC_TOPK - 1]
    tab = t5_bias.T
    bidx = jnp.arange(B)
    per_blk = MOBA_BLOCK // PAGE_SIZE
    pg_off = top_m[..., None] * per_blk + jnp.arange(per_blk)
    pages_moba = page_table[bidx[:, None, None, None], pg_off].reshape(-1)
    kpos = (pg_off.reshape(B, MOBA_HEADS, -1, 1) * PAGE_SIZE + jnp.arange(PAGE_SIZE))
    bm = _bias_heads(tab[:MOBA_HEADS], past - kpos)
    bm_new = jnp.zeros((B, MOBA_HEADS, 1, LANE), F32).at[..., 0].set(tab[None, :MOBA_HEADS, 0, None])
    bias_moba = jnp.concatenate([bm, bm_new, jnp.zeros((B, MOBA_HEADS, 1, LANE), F32)], axis=2)
    bias_moba = bias_moba.reshape(B * MOBA_HEADS, 8, LANE)
    per_page = PAGE_SIZE // SLC_BLOCK
    pages_slc = page_table[bidx[:, None], top_s // per_page].reshape(-1)
    halves_slc = (top_s % per_page).reshape(-1)
    spos = top_s[:, None, :, None] * SLC_BLOCK + jnp.arange(SLC_BLOCK)
    bs = _bias_heads(tab[MOBA_HEADS:], jnp.broadcast_to(past - spos, (B, NSA_HEADS, SLC_TOPK - 1, SLC_BLOCK)))
    bs_new = jnp.zeros((B, NSA_HEADS, 1, SLC_BLOCK), F32).at[..., 0].set(tab[None, MOBA_HEADS:, 0, None])
    bias_slc = jnp.concatenate([bs, bs_new], axis=2).transpose(0, 2, 1, 3)
    bias_slc = jnp.pad(bias_slc, ((0, 0), (0, 0), (0, 8 - NSA_HEADS), (0, 0)))
    o_a, o_slc, o_win = _dec_attend(cache_kv, cache_win, l, za_s, pages_moba, pages_slc, halves_slc,
                                    bias_moba, bias_slc, bias_win, B)
    pad = lambda a: jnp.pad(a, ((0, DEC_ROWS - B), (0, 0)))
    heads = lambda a: pad(a[:, :NSA_HEADS].reshape(B, NSA_W))
    return pad(o_a[:, 0].reshape(B, MOBA_W)), heads(o_cmp), heads(o_slc), heads(o_win)


def kernel(x_prompt, x_sample, cache_kv, cache_win, state_lru_h, state_lru_conv, page_table, c_prompt, c_sample,
           w_ada, b_ada, norm_mix, norm_ffn, w_in, cmp_pos_k, cmp_pos_v, w_cmp_k, w_cmp_v, conv_w, conv_b,
           w_rg, b_rg, w_ig, b_ig, lru_lambda, w_br_moba, w_br_nsa, w_br_lru, w_out, w_e_gate, w_e_up, w_e_down,
           t5_bias, w_router, b_router, norm_final):
    n_p, T, D = x_prompt.shape
    n_s = x_sample.shape[0]
    n_tok = n_p * T
    xp = x_prompt.reshape(n_tok, D)
    xs = x_sample.reshape(n_s, D)
    tabs = _prompt_tables(t5_bias, T)
    w_router_pad = jnp.pad(w_router, ((0, 0), (0, LANE - N_EXPERTS))).astype(BF16)
    b_router_pad = jnp.pad(b_router, (0, LANE - N_EXPERTS)).reshape(1, LANE)
    c_all = jax.nn.silu(jnp.concatenate([c_prompt, c_sample], axis=0))
    c_all = jnp.pad(c_all, ((0, 16 - n_p - n_s), (0, 0))).astype(BF16)
    dtabs = _decode_tables(t5_bias, page_table.shape[1] * PAGE_SIZE, cache_win.shape[2])
    moe_rows = LANE
    outs = [[] for _ in range(7)]
    kv_p = None
    mods =[_matmul_w32(c_all, w_ada, l, 0, 6 * D, 2048, 16) + b_ada[l] for l in range(DEPTH)]
    mod0 = mods[0][:n_p, None, :]
    hp = (_rms(x_prompt, norm_mix[0]) * (1.0 + mod0[..., D:2 * D]) + mod0[..., :D]).astype(BF16).reshape(n_tok, D)
    for l in range(DEPTH):
        lp = {'l': l, 'norm_mix': norm_mix[l], 'norm_ffn': norm_ffn[l], 'w_in': w_in,
              'cmp_pos_k': cmp_pos_k[l], 'cmp_pos_v': cmp_pos_v[l],
              'w_cmp_k': w_cmp_k[l].astype(BF16), 'w_cmp_v': w_cmp_v[l].astype(BF16),
              'conv_w': conv_w[l], 'conv_b': conv_b[l], 'w_rg': w_rg[l].astype(BF16), 'b_rg': b_rg[l],
              'w_ig': w_ig[l].astype(BF16), 'b_ig': b_ig[l], 'lru_lambda': lru_lambda[l],
              'w_br_moba': w_br_moba[l].astype(BF16), 'w_br_nsa': w_br_nsa[l].astype(BF16),
              'w_br_lru': w_br_lru[l].astype(BF16), 'w_out': w_out[l].astype(BF16)}
        mod = mods[l]
        mod3 = mod.reshape(16 * 6, 1, D)
        mod_s = mod[n_p:n_p + n_s]
        xp, h2_p, lg_p, kv_p, win_p, h_p, conv_p = _mixer_prompt(xp, hp, n_p, T, mod3, lp, tabs, w_router_pad,
                                                                 b_router_pad, kv_p)
        xs, h2_s, lg_s, kv_s, win_s, h_s, conv_s = _mixer_decode(l, xs, mod_s, lp, cache_kv, cache_win, page_table,
                                                                 state_lru_h[l], state_lru_conv[l], dtabs, t5_bias,
                                                                 w_router_pad, b_router_pad)
        tail = ((0, moe_rows - h2_s.shape[0]), (0, 0))
        h2 = jnp.concatenate([h2_p, jnp.pad(h2_s, tail)], axis=0)
        logits = jnp.concatenate([lg_p, jnp.pad(lg_s, tail)], axis=0)[:, :N_EXPERTS]
        moe_a, moe_b = _moe_prompt(h2, logits, l, w_e_gate, w_e_up, w_e_down)
        last = l == DEPTH - 1
        if last:
            y_prompt, = _ffn_residual(xp, moe_a, moe_b, mod3, mod3, norm_final, 256, T, True)
        else:
            mod3_next = mods[l + 1].reshape(16 * 6, 1, D)
            xp, hp = _ffn_residual(xp, moe_a, moe_b, mod3, mod3_next, norm_mix[l + 1], 256, T, False)
        xs = xs + mod_s[:, 5 * D:] * (moe_a[n_tok:n_tok + n_s] + moe_b[n_tok:n_tok + n_s])
        for lst, val in zip(outs, (kv_s[:, None, :], win_p, win_s, h_p, h_s, conv_p, conv_s)):
            lst.append(val)
    y_sample = _rms(xs, norm_final).reshape(n_s, 1, D)
    kv_s, *rest = [jnp.stack(o) for o in outs]
    return (y_prompt.reshape(n_p, T, D), y_sample, kv_p.reshape(DEPTH, n_p, T, KV_DIM), kv_s) + tuple(rest)
```

```python
import functools
import math

import jax
import jax.numpy as jnp
from jax import lax
from jax.experimental import pallas as pl
from jax.experimental.pallas import tpu as pltpu

D_MODEL = 2048
DEPTH = 2
PAGE_SIZE = 128
HEAD_DIM = 128
MOBA_HEADS = 4
MOBA_BLOCK = 256
MOBA_TOPK = 3
NSA_HEADS = 4
CMP_STRIDE = 16
CMP_LEN = 32
SLC_BLOCK = 64
SLC_TOPK = 16
WINDOW = 512
LRU_WIDTH = D_MODEL // 2
LRU_BLOCKS = 8
LRU_BLOCK = LRU_WIDTH // LRU_BLOCKS
LRU_C = 8.0
CONV_W = 4
N_EXPERTS = 16
N_GROUPS = 4
EXPERTS_PER_GROUP = N_EXPERTS // N_GROUPS
TOP_K = 2
D_EXPERT = D_MODEL // 2
T5_BUCKETS = 32
T5_MAX_DIST = 128
EPS = 1e-6
NEG = -1e30

MOBA_W = MOBA_HEADS * HEAD_DIM
NSA_W = NSA_HEADS * HEAD_DIM
KV_DIM = 2 * MOBA_W + 4 * HEAD_DIM
WIN_DIM = 2 * HEAD_DIM
IN_WIDTHS = (MOBA_W, MOBA_W, MOBA_W, NSA_W, HEAD_DIM, HEAD_DIM, HEAD_DIM, HEAD_DIM, HEAD_DIM, HEAD_DIM,
             3 * NSA_HEADS, LRU_WIDTH, LRU_WIDTH, D_MODEL, D_MODEL, D_MODEL)
COL_NG = 3 * MOBA_W + NSA_W + 6 * HEAD_DIM
COL_LX = COL_NG + 3 * NSA_HEADS
ZA_TILE = 1024
ZA_W = -(-(COL_NG + 3 * NSA_HEADS) // ZA_TILE) * ZA_TILE
ZB_TILE = 1024
ZB_W = 2 * LRU_WIDTH + 3 * D_MODEL
SCALE = HEAD_DIM ** -0.5

LANE = 128
ATT_TILE = 512
MOE_TILE = 256
VMEM_LIMIT = 56 * 1024 * 1024

F32 = jnp.float32
BF16 = jnp.bfloat16


def _cparams(sem):
    return pltpu.CompilerParams(dimension_semantics=sem, vmem_limit_bytes=VMEM_LIMIT)


def _mm_kernel(x_ref, w_ref, o_ref):
    o_ref[...] = jnp.dot(x_ref[...].astype(BF16), w_ref[...].astype(BF16), preferred_element_type=F32)


def _matmul(x, w, tm, tn):
    M, K = x.shape
    N = w.shape[1]
    assert M % tm == 0 and N % tn == 0
    return pl.pallas_call(
        _mm_kernel,
        grid=(N // tn, M // tm),
        in_specs=[pl.BlockSpec((tm, K), lambda j, i: (i, 0)),
                  pl.BlockSpec((K, tn), lambda j, i: (0, j))],
        out_specs=pl.BlockSpec((tm, tn), lambda j, i: (i, j)),
        out_shape=jax.ShapeDtypeStruct((M, N), F32),
        compiler_params=_cparams(("parallel", "parallel")),
        name="matmul",
    )(x, w)


def _mm_w32_kernel(x_ref, w_ref, *rest, shift):
    if shift:
        wn_ref, o_ref, wb_scr = rest
    else:
        o_ref, wb_scr = rest

    @pl.when(pl.program_id(1) == 0)
    def _():
        if not shift:
            wb_scr[...] = w_ref[0].astype(BF16)
            return
        nblk = w_ref.shape[2] // LANE
        q, r = divmod(shift, LANE)

        def window_block(i):
            ref = w_ref if i < nblk else wn_ref
            i = i % nblk
            return ref[0, :, i * LANE:(i + 1) * LANE]

        for c in range(nblk):
            blk = window_block(q + c)
            if r:
                pair = jnp.concatenate([blk, window_block(q + c + 1)], axis=1)
                blk = pltpu.roll(pair, 2 * LANE - r, 1)[:, :LANE]
            wb_scr[:, c * LANE:(c + 1) * LANE] = blk.astype(BF16)

    o_ref[...] = jnp.dot(x_ref[...], wb_scr[...], preferred_element_type=F32)


def _matmul_w32(x, w3, l, col0, n_out, tn, tm):
    M, K = x.shape
    shift = col0 % tn
    blk0 = col0 // tn
    assert n_out % tn == 0 and M % tm == 0
    mode = {'pipeline_mode': pl.Buffered(1)} if shift else {}
    wspec = lambda off: pl.BlockSpec((1, K, tn), lambda j, i: (l, 0, blk0 + j + off), **mode)
    in_specs = [pl.BlockSpec((tm, K), lambda j, i: (i, 0)), wspec(0)]
    args = [x, w3]
    if shift:
        in_specs.append(wspec(1))
        args.append(w3)
    return pl.pallas_call(
        functools.partial(_mm_w32_kernel, shift=shift),
        grid=(n_out // tn, M // tm),
        in_specs=in_specs,
        out_specs=pl.BlockSpec((tm, tn), lambda j, i: (i, j)),
        out_shape=jax.ShapeDtypeStruct((M, n_out), F32),
        scratch_shapes=[pltpu.VMEM((K, tn), BF16)],
        compiler_params=_cparams(("parallel", "arbitrary")),
        name="matmul_w32",
    )(*args)


def _za_kernel(x_ref, w_ref, *rest):
    o_ref, kv_ref, wb_scr = rest[-3:]

    @pl.when(pl.program_id(1) == 0)
    def _():
        wb_scr[...] = w_ref[0].astype(BF16)

    o = jnp.dot(x_ref[...], wb_scr[...], preferred_element_type=F32)
    o_ref[...] = o
    half = ZA_TILE // 2
    kv_ref[0] = jnp.where(pl.program_id(0) == 0, o[:, half:], o[:, :half])


def _za_proj(x, w3, l, tm, kv_prev):
    M, K = x.shape
    tn, half = ZA_TILE, ZA_TILE // 2
    assert MOBA_W == half and NSA_W == half and 4 * HEAD_DIM == half and KV_DIM == 3 * half and ZA_W == 3 * tn
    in_specs = [pl.BlockSpec((tm, K), lambda j, i: (i, 0)), pl.BlockSpec((1, K, tn), lambda j, i: (l, 0, j))]
    args = [x, w3]
    aliases = {}
    if kv_prev is not None:
        in_specs.append(pl.BlockSpec(memory_space=pl.ANY))
        args.append(kv_prev)
        aliases = {2: 1}
    return pl.pallas_call(
        _za_kernel,
        grid=(ZA_W // tn, M // tm),
        in_specs=in_specs,
        out_specs=[pl.BlockSpec((tm, tn), lambda j, i: (i, j)), pl.BlockSpec((1, tm, half), lambda j, i: (l, i, j))],
        out_shape=[jax.ShapeDtypeStruct((M, ZA_W), F32), jax.ShapeDtypeStruct((DEPTH, M, KV_DIM), F32)],
        scratch_shapes=[pltpu.VMEM((K, tn), BF16)],
        input_output_aliases=aliases,
        compiler_params=_cparams(("parallel", "arbitrary")),
        name="za_proj",
    )(*args)


def _merge_kernel(oa_ref, ocmp_ref, oslc_ref, owin_ref, ng_ref, oc_ref, ga_ref, gb_ref, gc_ref,
                  wa_ref, wb_ref, wc_ref, o_ref):
    gt = jax.nn.sigmoid(ng_ref[...])
    heads = []
    for g in range(NSA_HEADS):
        sl = slice(g * HEAD_DIM, (g + 1) * HEAD_DIM)
        heads.append(gt[:, 3 * g:3 * g + 1] * ocmp_ref[:, sl] + gt[:, 3 * g + 1:3 * g + 2] * oslc_ref[:, sl]
                     + gt[:, 3 * g + 2:3 * g + 3] * owin_ref[:, sl])
    ob = jnp.concatenate(heads, axis=1).astype(BF16)
    pa = jnp.dot(oa_ref[...].astype(BF16), wa_ref[...], preferred_element_type=F32)
    pb = jnp.dot(ob, wb_ref[...], preferred_element_type=F32)
    pc = jnp.dot(oc_ref[...].astype(BF16), wc_ref[...], preferred_element_type=F32)
    merged = (jax.nn.sigmoid(ga_ref[...]) * pa + jax.nn.sigmoid(gb_ref[...]) * pb
              + jax.nn.sigmoid(gc_ref[...]) * pc)
    o_ref[...] = merged.astype(BF16)


def _merge(o_a, o_cmp, o_slc, o_win, za, o_c, zb, w_a, w_b, w_c, tm):
    N = o_a.shape[0]
    D = D_MODEL
    row = lambda w, c: pl.BlockSpec((tm, w), lambda i: (i, c))
    full = lambda a: pl.BlockSpec(a.shape, lambda i: (0, 0))
    return pl.pallas_call(
        _merge_kernel,
        grid=(N // tm,),
        in_specs=[row(MOBA_W, 0), row(NSA_W, 0), row(NSA_W, 0), row(NSA_W, 0), row(LANE, COL_NG // LANE),
                  row(LRU_WIDTH, 0), row(D, 1), row(D, 2), row(D, 3), full(w_a), full(w_b), full(w_c)],
        out_specs=pl.BlockSpec((tm, D), lambda i: (i, 0)),
        out_shape=jax.ShapeDtypeStruct((N, D), BF16),
        compiler_params=_cparams(("parallel",)),
        name="merge",
    )(o_a, o_cmp, o_slc, o_win, za, o_c, zb, zb, zb, w_a, w_b, w_c)


def _outproj_kernel(m_ref, w_ref, x_ref, g1_ref, sc2_ref, sh2_ref, nf_ref, wr_ref, br_ref,
                    xo_ref, h2_ref, lg_ref):
    y = jnp.dot(m_ref[...], w_ref[...], preferred_element_type=F32)
    x = x_ref[...] + g1_ref[0] * y
    xo_ref[...] = x
    h2 = x * lax.rsqrt(jnp.mean(x * x, axis=-1, keepdims=True) + EPS) * nf_ref[...]
    h2 = (h2 * (1.0 + sc2_ref[0]) + sh2_ref[0]).astype(BF16)
    h2_ref[...] = h2
    lg_ref[...] = jnp.dot(h2, wr_ref[...], preferred_element_type=F32) + br_ref[...]


def _outproj(merged, w_out, x, mod3, norm_ffn, w_router_pad, b_router_pad, tm, rows_per_mod, mod_base):
    N, D = x.shape
    R = mod3.shape[1]
    mspec = lambda k: pl.BlockSpec((1, R, D), lambda i: ((mod_base + (i * tm) // rows_per_mod) * 6 + k, 0, 0))
    full = lambda a: pl.BlockSpec(a.shape, lambda i: (0, 0))
    return pl.pallas_call(
        _outproj_kernel,
        grid=(N // tm,),
        in_specs=[pl.BlockSpec((tm, D), lambda i: (i, 0)), full(w_out), pl.BlockSpec((tm, D), lambda i: (i, 0)),
                  mspec(2), mspec(4), mspec(3), full(norm_ffn), full(w_router_pad), full(b_router_pad)],
        out_specs=[pl.BlockSpec((tm, D), lambda i: (i, 0)), pl.BlockSpec((tm, D), lambda i: (i, 0)),
                   pl.BlockSpec((tm, LANE), lambda i: (i, 0))],
        out_shape=[jax.ShapeDtypeStruct((N, D), F32), jax.ShapeDtypeStruct((N, D), BF16),
                   jax.ShapeDtypeStruct((N, LANE), F32)],
        compiler_params=_cparams(("parallel",)),
        name="outproj",
    )(merged, w_out, x, mod3, mod3, mod3, norm_ffn, w_router_pad, b_router_pad)


def _ffn_residual_kernel(x_ref, a_ref, b_ref, g2_ref, gain_ref, sc_ref, sh_ref, *out_refs, last):
    x = x_ref[...] + g2_ref[0] * (a_ref[...] + b_ref[...])
    y = x * lax.rsqrt(jnp.mean(x * x, axis=-1, keepdims=True) + EPS) * gain_ref[...]
    if last:
        out_refs[0][...] = y
    else:
        out_refs[0][...] = x
        out_refs[1][...] = (y * (1.0 + sc_ref[0]) + sh_ref[0]).astype(BF16)


def _ffn_residual(x, a, b, mod3, mod3_next, gain, tm, rows_per_mod, last):
    N, D = x.shape
    mrow = lambda k: pl.BlockSpec((1, 1, D), lambda i: (((i * tm) // rows_per_mod) * 6 + k, 0, 0))
    tile = pl.BlockSpec((tm, D), lambda i: (i, 0))
    out_shape = [jax.ShapeDtypeStruct((N, D), F32)] + ([] if last else [jax.ShapeDtypeStruct((N, D), BF16)])
    return pl.pallas_call(
        functools.partial(_ffn_residual_kernel, last=last),
        grid=(N // tm,),
        in_specs=[tile, tile, tile, mrow(5), pl.BlockSpec((1, D), lambda i: (0, 0)), mrow(1), mrow(0)],
        out_specs=[tile] * len(out_shape),
        out_shape=out_shape,
        compiler_params=_cparams(("parallel",)),
        name="ffn_residual",
    )(x, a, b, mod3, gain.reshape(1, D), mod3_next, mod3_next)


def _rank_select(score, own, nblk, topk):
    lane = lax.broadcasted_iota(jnp.int32, score.shape, 1)
    past = lane < own
    s = jnp.where(past, score, NEG)
    rank = jnp.zeros(score.shape, jnp.int32)
    for j in range(nblk):
        col = s[:, j:j + 1]
        beats = (col > s) | ((col == s) & (lane > j))
        rank = rank + beats.astype(jnp.int32)
    return (past & (rank < topk)) | (lane == own)


def _moba_select_kernel(q_ref, k_ref, sel_ref):
    T = q_ref.shape[1]
    nb = T // MOBA_BLOCK
    k = k_ref[0]
    kmean = jnp.sum(k.reshape(nb, MOBA_BLOCK, HEAD_DIM), axis=1) * (1.0 / MOBA_BLOCK)
    kmean = jnp.concatenate([kmean, jnp.zeros((LANE - nb, HEAD_DIM), F32)], axis=0)
    score = lax.dot_general(q_ref[0].astype(BF16), kmean.astype(BF16), (((1,), (1,)), ((), ())),
                            preferred_element_type=F32)
    own = lax.broadcasted_iota(jnp.int32, (T, 1), 0) // MOBA_BLOCK
    sel = _rank_select(score, own, nb, MOBA_TOPK)
    sel_ref[0, 0] = sel.astype(BF16)


def _moba_select(za, B, T):
    return pl.pallas_call(
        _moba_select_kernel,
        grid=(B, MOBA_HEADS),
        in_specs=[pl.BlockSpec((1, T, HEAD_DIM), lambda b, h: (b, 0, h)),
                  pl.BlockSpec((1, T, HEAD_DIM), lambda b, h: (b, 0, MOBA_HEADS + h))],
        out_specs=pl.BlockSpec((1, 1, T, LANE), lambda b, h: (b, h, 0, 0)),
        out_shape=jax.ShapeDtypeStruct((B, MOBA_HEADS, T, LANE), BF16),
        compiler_params=_cparams(("parallel", "parallel")),
        name="moba_select",
    )(za, za)


def _flash_kernel(*refs, G, window_tiles, use_sel, per_head_kv):
    if use_sel:
        q_ref, k_ref, vt_ref, bias_ref, sel_ref, et_ref, o_ref, m_scr, l_scr, acc_scr = refs
    else:
        q_ref, k_ref, vt_ref, bias_ref, o_ref, m_scr, l_scr, acc_scr = refs
    qi = pl.program_id(1)
    kj = pl.program_id(2)
    nk = pl.num_programs(2)
    tq, tk = ATT_TILE, ATT_TILE
    nt = (((1,), (1,)), ((), ()))

    @pl.when(kj == 0)
    def _():
        m_scr[...] = jnp.full(m_scr.shape, NEG, F32)
        l_scr[...] = jnp.zeros(l_scr.shape, F32)
        acc_scr[...] = jnp.zeros(acc_scr.shape, F32)

    lo = jnp.maximum(qi - window_tiles, 0) if window_tiles is not None else 0

    @pl.when((kj >= lo) & (kj <= qi))
    def _():
        key = lax.broadcasted_iota(jnp.int32, (tk, tq), 0)
        qry = lax.broadcasted_iota(jnp.int32, (tk, tq), 1)
        dist = (qi - kj) * tq + qry - key
        band = dist >= 0
        if window_tiles is not None:
            band = band & (dist < WINDOW)
        mask = band
        for g in range(G):
            kv = slice(g * HEAD_DIM, (g + 1) * HEAD_DIM) if per_head_kv else slice(0, HEAD_DIM)
            if g == 0 or per_head_kv:
                k = k_ref[0, :, kv].astype(BF16)
                vt = vt_ref[0, kv, :].astype(BF16)
                if use_sel:
                    hit = lax.dot_general(et_ref[...], sel_ref[0, g], nt, preferred_element_type=F32)
                    mask = band & (hit > 0.5)
            q = q_ref[0, :, g * HEAD_DIM:(g + 1) * HEAD_DIM].astype(BF16)
            s = lax.dot_general(k, q, nt, preferred_element_type=F32)
            s = s * SCALE + bias_ref[g, 0]
            s = jnp.where(mask, s, NEG)
            m_prev = m_scr[g]
            m_new = jnp.maximum(m_prev, jnp.max(s, axis=0, keepdims=True))
            p = jnp.exp(s - m_new)
            alpha = jnp.exp(m_prev - m_new)
            l_scr[g] = alpha * l_scr[g] + jnp.sum(p, axis=0, keepdims=True)
            acc_scr[g] = alpha * acc_scr[g] + jnp.dot(vt, p.astype(BF16), preferred_element_type=F32)
            m_scr[g] = m_new

    @pl.when(kj == nk - 1)
    def _():
        for g in range(G):
            l = l_scr[g]
            o = acc_scr[g] / jnp.where(l > 0.0, l, 1.0)
            o_ref[0, :, g * HEAD_DIM:(g + 1) * HEAD_DIM] = o.T


def _flash(za, bias, *, B, T, q_col, k_col, v_col, bias_blk, per_head_kv, window_tiles=None, sel=None, emat=None):
    tq = tk = ATT_TILE
    nq, nk = T // tq, T // tk
    G = NSA_HEADS
    use_sel = sel is not None
    kv_w = G * HEAD_DIM if per_head_kv else HEAD_DIM
    vt = jnp.swapaxes(za[..., v_col:v_col + kv_w], 1, 2)

    def kv_blk(i, j):
        lo = jnp.maximum(i - window_tiles, 0) if window_tiles is not None else 0
        return jnp.clip(j, lo, i)

    in_specs = [
        pl.BlockSpec((1, tq, G * HEAD_DIM), lambda b, i, j: (b, i, q_col)),
        pl.BlockSpec((1, tk, kv_w), lambda b, i, j: (b, kv_blk(i, j), k_col)),
        pl.BlockSpec((1, kv_w, tk), lambda b, i, j: (b, 0, kv_blk(i, j))),
        pl.BlockSpec((G, 1, tk, tq), lambda b, i, j: (bias_blk, jnp.clip(i - j, 0, 2), 0, 0)),
    ]
    args = [za, za, vt, bias]
    if use_sel:
        in_specs += [pl.BlockSpec((1, sel.shape[1], tq, LANE), lambda b, i, j: (b, 0, i, 0)),
                     pl.BlockSpec((tk, LANE), lambda b, i, j: (kv_blk(i, j), 0))]
        args += [sel, emat]
    return pl.pallas_call(
        functools.partial(_flash_kernel, G=G, window_tiles=window_tiles, use_sel=use_sel, per_head_kv=per_head_kv),
        grid=(B, nq, nk),
        in_specs=in_specs,
        out_specs=pl.BlockSpec((1, tq, G * HEAD_DIM), lambda b, i, j: (b, i, 0)),
        out_shape=jax.ShapeDtypeStruct((B, T, G * HEAD_DIM), F32),
        scratch_shapes=[pltpu.VMEM((G, 1, tq), F32), pltpu.VMEM((G, 1, tq), F32),
                        pltpu.VMEM((G, HEAD_DIM, tq), F32)],
        compiler_params=_cparams(("parallel", "parallel", "arbitrary")),
        name="flash",
    )(*args)


def _cmp_kernel(ck_ref, cv_ref, q_ref, pek_ref, pev_ref, wk_ref, wv_ref, bias_ref, ovl_ref,
                o_ref, sel_ref, kc_scr, vc_scr):
    T = ck_ref.shape[1]
    n_ch = T // CMP_STRIDE
    i = pl.program_id(1)
    tq = ATT_TILE

    @pl.when(i == 0)
    def _():
        for src, pe, w, dst in ((ck_ref, pek_ref, wk_ref, kc_scr), (cv_ref, pev_ref, wv_ref, vc_scr)):
            first = jnp.zeros((n_ch, HEAD_DIM), F32)
            second = jnp.zeros((n_ch, HEAD_DIM), F32)
            for l in range(CMP_STRIDE):
                x = src[0, pl.ds(l, n_ch, stride=CMP_STRIDE), :]
                l2 = CMP_STRIDE + l
                first = first + jnp.dot((x + pe[l:l + 1, :]).astype(BF16), w[l * HEAD_DIM:(l + 1) * HEAD_DIM, :],
                                        preferred_element_type=F32)
                second = second + jnp.dot((x + pe[l2:l2 + 1, :]).astype(BF16),
                                          w[l2 * HEAD_DIM:(l2 + 1) * HEAD_DIM, :], preferred_element_type=F32)
            dst[...] = (first + pltpu.roll(second, n_ch - 1, 0)).astype(BF16)

    kc = kc_scr[...]
    vc = vc_scr[...]
    t = i * tq + lax.broadcasted_iota(jnp.int32, (tq, n_ch), 0)
    n = lax.broadcasted_iota(jnp.int32, (tq, n_ch), 1)
    vis = (n * CMP_STRIDE + (CMP_LEN - 1)) <= t
    imp = jnp.zeros((tq, LANE), F32)
    ovl = ovl_ref[...]
    for g in range(NSA_HEADS):
        q = q_ref[0, :, g * HEAD_DIM:(g + 1) * HEAD_DIM].astype(BF16)
        s = lax.dot_general(q, kc, (((1,), (1,)), ((), ())), preferred_element_type=F32)
        s = s * SCALE + bias_ref[g]
        s = jnp.where(vis, s, NEG)
        m = jnp.max(s, axis=1, keepdims=True)
        e = jnp.where(vis, jnp.exp(s - m), 0.0)
        l = jnp.sum(e, axis=1, keepdims=True)
        p = e / jnp.where(l > 0.0, l, 1.0)
        pb = p.astype(BF16)
        o_ref[0, :, g * HEAD_DIM:(g + 1) * HEAD_DIM] = jnp.dot(pb, vc, preferred_element_type=F32)
        imp = imp + jnp.dot(pb, ovl, preferred_element_type=F32)
    own = (i * tq + lax.broadcasted_iota(jnp.int32, (tq, 1), 0)) // SLC_BLOCK
    sel = _rank_select(imp, own, T // SLC_BLOCK, SLC_TOPK - 1)
    sel_ref[0, 0] = sel.astype(BF16)


def _cmp_branch(za, pe_k, pe_v, w_k, w_v, bias_cmp, overlap, B, T):
    assert T // CMP_STRIDE == LANE
    tq = ATT_TILE
    col = 3 * MOBA_HEADS + NSA_HEADS
    return pl.pallas_call(
        _cmp_kernel,
        grid=(B, T // tq),
        in_specs=[
            pl.BlockSpec((1, T, HEAD_DIM), lambda b, i: (b, 0, col)),
            pl.BlockSpec((1, T, HEAD_DIM), lambda b, i: (b, 0, col + 1)),
            pl.BlockSpec((1, tq, NSA_W), lambda b, i: (b, i, 3)),
            pl.BlockSpec((CMP_LEN, HEAD_DIM), lambda b, i: (0, 0)),
            pl.BlockSpec((CMP_LEN, HEAD_DIM), lambda b, i: (0, 0)),
            pl.BlockSpec((CMP_LEN * HEAD_DIM, HEAD_DIM), lambda b, i: (0, 0)),
            pl.BlockSpec((CMP_LEN * HEAD_DIM, HEAD_DIM), lambda b, i: (0, 0)),
            pl.BlockSpec((NSA_HEADS, tq, LANE), lambda b, i: (0, i, 0)),
            pl.BlockSpec((LANE, LANE), lambda b, i: (0, 0)),
        ],
        out_specs=[pl.BlockSpec((1, tq, NSA_W), lambda b, i: (b, i, 0)),
                   pl.BlockSpec((1, 1, tq, LANE), lambda b, i: (b, 0, i, 0))],
        out_shape=[jax.ShapeDtypeStruct((B, T, NSA_W), F32),
                   jax.ShapeDtypeStruct((B, 1, T, LANE), BF16)],
        scratch_shapes=[pltpu.VMEM((LANE, HEAD_DIM), BF16), pltpu.VMEM((LANE, HEAD_DIM), BF16)],
        compiler_params=_cparams(("parallel", "arbitrary")),
        name="nsa_cmp",
    )(za, za, za, pe_k, pe_v, w_k, w_v, bias_cmp, overlap)


def _gelu_tanh(x):
    return 0.5 * x * (1.0 + jnp.tanh(math.sqrt(2.0 / math.pi) * (x + 0.044715 * (x * x * x))))


def _softplus(x):
    return jnp.maximum(x, 0.0) + jnp.log1p(jnp.exp(-jnp.abs(x)))


def _lru_kernel(x_ref, g_ref, c0_ref, h0_ref, cw_ref, cb_ref, wr_ref, br_ref, wi_ref, bi_ref, lam_ref,
                y_ref, hl_ref, cl_ref, a_scr, x_scr):
    T = x_ref.shape[1]
    x = x_ref[0]
    row = lax.broadcasted_iota(jnp.int32, (T, LRU_BLOCK), 0)
    c0 = c0_ref[0]

    def shifted(d):
        r = pltpu.roll(x, d, 0)
        for t in range(d):
            r = jnp.where(row == t, c0[t + CONV_W - 1 - d:t + CONV_W - d, :], r)
        return r

    u = cb_ref[...] + shifted(3) * cw_ref[0:1, :]
    u = u + shifted(2) * cw_ref[1:2, :]
    u = u + shifted(1) * cw_ref[2:3, :]
    u = u + x * cw_ref[3:4, :]
    ub = u.astype(BF16)
    r = jax.nn.sigmoid(jnp.dot(ub, wr_ref[0], preferred_element_type=F32) + br_ref[...])
    ig = jax.nn.sigmoid(jnp.dot(ub, wi_ref[0], preferred_element_type=F32) + bi_ref[...])
    log_a = (-LRU_C * r) * _softplus(-lam_ref[...])
    a = jnp.exp(log_a)
    b = jnp.sqrt(1.0 - jnp.exp(2.0 * log_a)) * (ig * u)
    sub = row % 8
    for d in (1, 2, 4):
        ok = sub >= d
        b = jnp.where(ok, a * pltpu.roll(b, d, 0) + b, b)
        a = jnp.where(ok, a * pltpu.roll(a, d, 0), a)
    a_scr[...] = a
    x_scr[...] = b

    def group(k, carry):
        s = pl.multiple_of(k * 8, 8)
        h = x_scr[pl.ds(s, 8), :] + a_scr[pl.ds(s, 8), :] * carry
        x_scr[pl.ds(s, 8), :] = h
        return h[7:8, :]

    h_last = lax.fori_loop(0, T // 8, group, h0_ref[0], unroll=8)
    y_ref[0] = x_scr[...] * _gelu_tanh(g_ref[0])
    hl_ref[0] = h_last
    cl_ref[0] = x[T - (CONV_W - 1):, :]


def _rglru(zb, conv0, h0, conv_w, conv_b, w_r, b_r, w_i, b_i, lam, B, T):
    nblk = LRU_BLOCKS
    vec = lambda a: a.reshape(1, LRU_WIDTH)
    row_spec = pl.BlockSpec((1, LRU_BLOCK), lambda b, n: (0, n))
    return pl.pallas_call(
        _lru_kernel,
        grid=(B, nblk),
        in_specs=[
            pl.BlockSpec((1, T, LRU_BLOCK), lambda b, n: (b, 0, n)),
            pl.BlockSpec((1, T, LRU_BLOCK), lambda b, n: (b, 0, nblk + n)),
            pl.BlockSpec((1, CONV_W - 1, LRU_BLOCK), lambda b, n: (b, 0, n)),
            pl.BlockSpec((1, 1, LRU_BLOCK), lambda b, n: (b, 0, n)),
            pl.BlockSpec((CONV_W, LRU_BLOCK), lambda b, n: (0, n)),
            row_spec,
            pl.BlockSpec((1, LRU_BLOCK, LRU_BLOCK), lambda b, n: (n, 0, 0)),
            row_spec,
            pl.BlockSpec((1, LRU_BLOCK, LRU_BLOCK), lambda b, n: (n, 0, 0)),
            row_spec,
            row_spec,
        ],
        out_specs=[pl.BlockSpec((1, T, LRU_BLOCK), lambda b, n: (b, 0, n)),
                   pl.BlockSpec((1, 1, LRU_BLOCK), lambda b, n: (b, 0, n)),
                   pl.BlockSpec((1, CONV_W - 1, LRU_BLOCK), lambda b, n: (b, 0, n))],
        out_shape=[jax.ShapeDtypeStruct((B, T, LRU_WIDTH), F32),
                   jax.ShapeDtypeStruct((B, 1, LRU_WIDTH), F32),
                   jax.ShapeDtypeStruct((B, CONV_W - 1, LRU_WIDTH), F32)],
        scratch_shapes=[pltpu.VMEM((T, LRU_BLOCK), F32), pltpu.VMEM((T, LRU_BLOCK), F32)],
        compiler_params=_cparams(("parallel", "parallel")),
        name="rglru",
    )(zb, zb, conv0, h0.reshape(B, 1, LRU_WIDTH), conv_w, vec(conv_b), w_r.astype(BF16), vec(b_r),
      w_i.astype(BF16), vec(b_i), vec(lam))


def _moe_kernel(eid_ref, nt_ref, x_ref, wg_ref, wu_ref, wd_ref, cw_ref, o_ref, wg_scr, wu_scr, wd_scr):
    t = pl.program_id(0)
    live = t < nt_ref[0]
    new_expert = (t == 0) | (eid_ref[t] != eid_ref[jnp.maximum(t - 1, 0)])

    @pl.when(live & new_expert)
    def _():
        wg_scr[...] = wg_ref[0, 0].astype(BF16)
        wu_scr[...] = wu_ref[0, 0].astype(BF16)
        wd_scr[...] = wd_ref[0, 0].astype(BF16)

    @pl.when(live)
    def _():
        x = x_ref[...]
        g = jnp.dot(x, wg_scr[...], preferred_element_type=F32)
        u = jnp.dot(x, wu_scr[...], preferred_element_type=F32)
        act = (g * jax.nn.sigmoid(g)) * u
        y = jnp.dot(act.astype(BF16), wd_scr[...], preferred_element_type=F32)
        o_ref[...] = y * cw_ref[...]

    @pl.when(t >= nt_ref[0])
    def _():
        o_ref[...] = jnp.zeros(o_ref.shape, F32)


def _moe_ffn(x_pad, cw_pad, tile_eid, n_tiles, l, w_g, w_u, w_d):
    P, D = x_pad.shape
    tm = MOE_TILE
    grid_spec = pltpu.PrefetchScalarGridSpec(
        num_scalar_prefetch=2,
        grid=(P // tm,),
        in_specs=[
            pl.BlockSpec((tm, D), lambda t, eid, nt: (t, 0)),
            pl.BlockSpec((1, 1, D, D_EXPERT), lambda t, eid, nt: (l, eid[t], 0, 0)),
            pl.BlockSpec((1, 1, D, D_EXPERT), lambda t, eid, nt: (l, eid[t], 0, 0), pipeline_mode=pl.Buffered(1)),
            pl.BlockSpec((1, 1, D_EXPERT, D), lambda t, eid, nt: (l, eid[t], 0, 0), pipeline_mode=pl.Buffered(1)),
            pl.BlockSpec((tm, 1), lambda t, eid, nt: (t, 0)),
        ],
        out_specs=pl.BlockSpec((tm, D), lambda t, eid, nt: (t, 0)),
        scratch_shapes=[pltpu.VMEM((D, D_EXPERT), BF16), pltpu.VMEM((D, D_EXPERT), BF16),
                        pltpu.VMEM((D_EXPERT, D), BF16)],
    )
    return pl.pallas_call(
        _moe_kernel,
        grid_spec=grid_spec,
        out_shape=jax.ShapeDtypeStruct((P, D), F32),
        compiler_params=_cparams(("arbitrary",)),
        name="moe_ffn",
    )(tile_eid, n_tiles, x_pad, w_g, w_u, w_d, cw_pad)


def _route(logits):
    probs = jax.nn.softmax(logits.astype(F32), axis=-1)
    grp = probs.reshape(-1, N_GROUPS, EXPERTS_PER_GROUP)
    pairs = [grp[..., a] + grp[..., b] for a in range(EXPERTS_PER_GROUP) for b in range(a + 1, EXPERTS_PER_GROUP)]
    g_score = functools.reduce(jnp.maximum, pairs)
    g_best = jnp.argmax(g_score, axis=-1)
    pick = g_best[:, None, None] == jnp.arange(N_GROUPS)[None, :, None]
    in_grp = jnp.sum(jnp.where(pick, grp, 0.0), axis=1)
    i1 = jnp.argmax(in_grp, axis=-1)
    first = jnp.arange(EXPERTS_PER_GROUP)[None, :] == i1[:, None]
    w1 = jnp.max(in_grp, axis=-1)
    rest = jnp.where(first, -jnp.inf, in_grp)
    i2 = jnp.argmax(rest, axis=-1)
    w2 = jnp.max(rest, axis=-1)
    w_top = jnp.stack([w1, w2], axis=-1)
    w_top = w_top / jnp.sum(w_top, axis=-1, keepdims=True)
    e_idx = g_best[:, None] * EXPERTS_PER_GROUP + jnp.stack([i1, i2], axis=-1)
    return e_idx.astype(jnp.int32), w_top


def _moe_prompt(h2, logits, l, w_g, w_u, w_d):
    N, D = h2.shape
    tm = MOE_TILE
    e_idx, w_top = _route(logits)
    flat_e = e_idx.reshape(-1)
    onehot = (flat_e[:, None] == jnp.arange(N_EXPERTS)[None, :]).astype(jnp.int32)
    within = jnp.sum(onehot * jnp.cumsum(onehot, axis=0), axis=1) - 1
    sizes = jnp.sum(onehot, axis=0)
    padded = ((sizes + tm - 1) // tm) * tm
    pend = jnp.cumsum(padded)
    pstart = pend - padded
    dest = jnp.sum(onehot * pstart[None, :], axis=1) + within
    P = 2 * N + N_EXPERTS * tm
    fields = jnp.stack([(jnp.arange(2 * N, dtype=jnp.int32) // TOP_K).astype(F32), w_top.reshape(-1)], axis=1)
    packed = jnp.zeros((P, 2), F32).at[dest].set(fields)
    src_tok = packed[:, 0].astype(jnp.int32)
    cw_pad = packed[:, 1]
    x_pad = h2[src_tok]
    tile_start = jnp.arange(P // tm, dtype=jnp.int32) * tm
    tile_eid = jnp.minimum(jnp.sum(tile_start[:, None] >= pend[None, :], axis=1), N_EXPERTS - 1).astype(jnp.int32)
    n_tiles = (pend[-1] // tm).astype(jnp.int32).reshape(1)
    out = _moe_ffn(x_pad, cw_pad.reshape(P, 1), tile_eid, n_tiles, l, w_g, w_u, w_d)
    pos = dest.reshape(N, TOP_K)
    return out[pos[:, 0]], out[pos[:, 1]]


def _t5_bucket(dist):
    n = jnp.maximum(dist, 0)
    exact = T5_BUCKETS // 2
    nf = jnp.maximum(n, 1).astype(F32)
    large = exact + (jnp.log(nf / exact) / math.log(T5_MAX_DIST / exact) * (T5_BUCKETS - exact)).astype(jnp.int32)
    return jnp.where(n < exact, n, jnp.minimum(large, T5_BUCKETS - 1))


def _bias_lookup(tab, dist):
    bucket = _t5_bucket(dist)
    out = jnp.zeros((tab.shape[0],) + bucket.shape, F32)
    for b in range(T5_BUCKETS):
        out = jnp.where(bucket[None] == b, tab[:, b].reshape((-1,) + (1,) * bucket.ndim), out)
    return out


def _rms(x, g):
    return x * lax.rsqrt(jnp.mean(x * x, axis=-1, keepdims=True) + EPS) * g


def _split(z, widths):
    outs, s = [], 0
    for w in widths:
        outs.append(z[..., s:s + w])
        s += w
    return outs


def _prompt_tables(t5_bias, T):
    t = ATT_TILE
    i = jnp.arange(t)
    dist = jnp.arange(2)[:, None, None] * t + i[None, None, :] - i[None, :, None]
    assert t >= T5_MAX_DIST
    far = jnp.broadcast_to(t5_bias.T[:, None, None, None, T5_BUCKETS - 1], (t5_bias.shape[1], 1, t, t))
    bias_tiles = jnp.concatenate([_bias_lookup(t5_bias.T, dist), far], axis=1)
    pos = jnp.arange(T)
    lane = jnp.arange(LANE)
    e_moba = (pos[:, None] // MOBA_BLOCK == lane[None, :]).astype(BF16)
    e_slc = (pos[:, None] // SLC_BLOCK == lane[None, :]).astype(BF16)
    d_cmp = pos[:, None] - (lane[None, :] * CMP_STRIDE + CMP_LEN - 1)
    bias_cmp = _bias_lookup(t5_bias.T[MOBA_HEADS:], d_cmp)
    c_start = lane * CMP_STRIDE
    s_start = lane * SLC_BLOCK
    n_cmp = T // CMP_STRIDE - 1
    overlap = ((c_start[:, None] < s_start[None, :] + SLC_BLOCK) & (c_start[:, None] + CMP_LEN > s_start[None, :])
               & (lane[:, None] < n_cmp) & (lane[None, :] < T // SLC_BLOCK)).astype(BF16)
    return bias_tiles, e_moba, e_slc, bias_cmp, overlap


def _mixer_prompt(x, h, B, T, mod3, lp, tabs, w_router_pad, b_router_pad, kv_prev):
    N, D = x.shape
    bias_tiles, e_moba, e_slc, bias_cmp, overlap = tabs
    za, new_kv = _za_proj(h, lp['w_in'], lp['l'], 1024, kv_prev)
    za = za.reshape(B, T, ZA_W)
    zb = _matmul_w32(h, lp['w_in'], lp['l'], COL_LX, ZB_W, ZB_TILE, 1024).reshape(B, T, ZB_W)
    win = za[:, T - WINDOW:, COL_NG - WIN_DIM:COL_NG]

    sel_moba = _moba_select(za, B, T)
    o_a = _flash(za, bias_tiles, B=B, T=T, q_col=0, k_col=1, v_col=2 * MOBA_W, bias_blk=0, per_head_kv=True,
                 sel=sel_moba, emat=e_moba)
    o_cmp, sel_slc = _cmp_branch(za, lp['cmp_pos_k'], lp['cmp_pos_v'], lp['w_cmp_k'].reshape(-1, HEAD_DIM),
                                 lp['w_cmp_v'].reshape(-1, HEAD_DIM), bias_cmp, overlap, B, T)
    o_slc = _flash(za, bias_tiles, B=B, T=T, q_col=3, k_col=18, v_col=19 * HEAD_DIM, bias_blk=1, per_head_kv=False,
                   sel=sel_slc, emat=e_slc)
    o_win = _flash(za, bias_tiles, B=B, T=T, q_col=3, k_col=20, v_col=21 * HEAD_DIM, bias_blk=1, per_head_kv=False,
                   window_tiles=WINDOW // ATT_TILE)
    o_c, h_last, conv_last = _rglru(zb, jnp.zeros((B, CONV_W - 1, LRU_WIDTH), F32), jnp.zeros((B, LRU_WIDTH), F32),
                                    lp['conv_w'], lp['conv_b'], lp['w_rg'], lp['b_rg'], lp['w_ig'], lp['b_ig'],
                                    lp['lru_lambda'], B, T)
    flat = lambda a: a.reshape(N, a.shape[-1])
    merged = _merge(flat(o_a), flat(o_cmp), flat(o_slc), flat(o_win), flat(za), flat(o_c), flat(zb),
                    lp['w_br_moba'], lp['w_br_nsa'], lp['w_br_lru'], 256)
    x, h2, logits = _outproj(merged, lp['w_out'], x, mod3, lp['norm_ffn'].reshape(1, D),
                             w_router_pad, b_router_pad, 256, T, 0)
    return x, h2, logits, new_kv, win, h_last.reshape(B, LRU_WIDTH), conv_last


def _mixer_decode(l, x, mod, lp, cache_kv, cache_win, page_table, h0, conv0, dtabs, t5_bias, w_router_pad,
                  b_router_pad):
    B, D = x.shape
    pad = lambda a: jnp.pad(a, ((0, DEC_ROWS - B),) + ((0, 0),) * (a.ndim - 1))
    sh1, sc1 = mod[:, :D], mod[:, D:2 * D]
    h = pad((_rms(x, lp['norm_mix']) * (1.0 + sc1) + sh1).astype(BF16))
    za = _matmul_w32(h, lp['w_in'], l, 0, ZA_W, ZA_TILE, DEC_ROWS)
    zb = _matmul_w32(h, lp['w_in'], l, COL_LX, ZB_W, ZB_TILE, DEC_ROWS)
    new_kv = jnp.concatenate([za[:B, MOBA_W:3 * MOBA_W], za[:B, 3 * MOBA_W + NSA_W:COL_NG - WIN_DIM]], axis=-1)
    n_win = cache_win.shape[2]
    win = jnp.concatenate([cache_win[l], za[:B, None, COL_NG - WIN_DIM:COL_NG]], axis=1)
    win = win[:, win.shape[1] - min(WINDOW, n_win + 1):]
    cmp_w = (lp['cmp_pos_k'], lp['cmp_pos_v'], lp['w_cmp_k'], lp['w_cmp_v'])
    o_a, o_cmp, o_slc, o_win = _decode_attention(l, za, cache_kv, cache_win, page_table, cmp_w, dtabs, t5_bias)
    o_c, h_new, conv_t = _lru_step(zb, conv0.transpose(1, 0, 2), h0, lp['conv_w'], lp['conv_b'], lp['w_rg'],
                                   lp['b_rg'], lp['w_ig'], lp['b_ig'], lp['lru_lambda'])
    merged = _merge(o_a, o_cmp, o_slc, o_win, za, pad(o_c), zb, lp['w_br_moba'], lp['w_br_nsa'], lp['w_br_lru'],
                    DEC_ROWS)
    mod3 = pad(mod).reshape(DEC_ROWS, 6, D).transpose(1, 0, 2)
    x, h2, logits = _outproj(merged, lp['w_out'], pad(x), mod3, lp['norm_ffn'].reshape(1, D), w_router_pad,
                             b_router_pad, DEC_ROWS, DEC_ROWS, 0)
    return x[:B], h2, logits, new_kv, win, h_new, conv_t.transpose(1, 0, 2)


SCAN_PAGES = 32
DEC_ROWS = 16


def _page_scan_kernel(pt_ref, *refs):
    del pt_ref
    ck_refs, cv_refs, mk_refs = refs[:SCAN_PAGES], refs[SCAN_PAGES:2 * SCAN_PAGES], refs[2 * SCAN_PAGES:3 * SCAN_PAGES]
    pek_ref, pev_ref, wk_ref, wv_ref, abk_ref, abv_ref, km_ref = refs[3 * SCAN_PAGES:]
    chunks = PAGE_SIZE // CMP_STRIDE
    for src, pe, w, dst in ((ck_refs, pek_ref, wk_ref, abk_ref), (cv_refs, pev_ref, wv_ref, abv_ref)):
        first = jnp.zeros((SCAN_PAGES * chunks, HEAD_DIM), F32)
        second = jnp.zeros((SCAN_PAGES * chunks, HEAD_DIM), F32)
        for l in range(CMP_STRIDE):
            x = jnp.concatenate([r[0, 0, pl.ds(l, chunks, stride=CMP_STRIDE), :] for r in src], axis=0)
            l2 = CMP_STRIDE + l
            first = first + jnp.dot((x + pe[l:l + 1, :]).astype(BF16), w[l], preferred_element_type=F32)
            second = second + jnp.dot((x + pe[l2:l2 + 1, :]).astype(BF16), w[l2], preferred_element_type=F32)
        dst[0, :, 0:HEAD_DIM] = first
        dst[0, :, HEAD_DIM:2 * HEAD_DIM] = second
    sums = [jnp.sum(r[0, 0], axis=0, keepdims=True) for r in mk_refs]
    per_blk = MOBA_BLOCK // PAGE_SIZE
    means = [functools.reduce(lambda a, b: a + b, sums[j * per_blk:(j + 1) * per_blk])
             for j in range(SCAN_PAGES // per_blk)]
    km_ref[0] = jnp.concatenate(means, axis=0) * (1.0 / MOBA_BLOCK)


def _page_scan(cache_kv, l, page_table, pe_k, pe_v, w_k, w_v):
    B, n_pages = page_table.shape
    steps = n_pages // SCAN_PAGES
    chunks = PAGE_SIZE // CMP_STRIDE
    blocks = SCAN_PAGES * PAGE_SIZE // MOBA_BLOCK
    cmp_col = 2 * MOBA_W // HEAD_DIM

    def cmp_spec(j, col):
        return pl.BlockSpec((1, 1, PAGE_SIZE, HEAD_DIM), lambda b, s, pt: (l, pt[b, s * SCAN_PAGES + j], 0, col))

    def mk_spec(j):
        return pl.BlockSpec((1, 1, PAGE_SIZE, MOBA_W), lambda b, s, pt: (l, pt[b, s * SCAN_PAGES + j], 0, 0))

    const = lambda a: pl.BlockSpec(a.shape, lambda b, s, pt: (0,) * a.ndim)
    grid_spec = pltpu.PrefetchScalarGridSpec(
        num_scalar_prefetch=1,
        grid=(B, steps),
        in_specs=([cmp_spec(j, cmp_col) for j in range(SCAN_PAGES)]
                  + [cmp_spec(j, cmp_col + 1) for j in range(SCAN_PAGES)] + [mk_spec(j) for j in range(SCAN_PAGES)]
                  + [const(pe_k), const(pe_v), const(w_k), const(w_v)]),
        out_specs=[pl.BlockSpec((1, SCAN_PAGES * chunks, 2 * HEAD_DIM), lambda b, s, pt: (b, s, 0)),
                   pl.BlockSpec((1, SCAN_PAGES * chunks, 2 * HEAD_DIM), lambda b, s, pt: (b, s, 0)),
                   pl.BlockSpec((1, blocks, MOBA_W), lambda b, s, pt: (b, s, 0))],
    )
    n_ch = n_pages * chunks
    return pl.pallas_call(
        _page_scan_kernel,
        grid_spec=grid_spec,
        out_shape=[jax.ShapeDtypeStruct((B, n_ch, 2 * HEAD_DIM), F32),
                   jax.ShapeDtypeStruct((B, n_ch, 2 * HEAD_DIM), F32),
                   jax.ShapeDtypeStruct((B, steps * blocks, MOBA_W), F32)],
        compiler_params=_cparams(("parallel", "arbitrary")),
        name="page_scan",
    )(page_table, *([cache_kv] * (3 * SCAN_PAGES)), pe_k, pe_v, w_k, w_v)


def _topk_lanes(score, k):
    lane = lax.broadcasted_iota(jnp.int32, score.shape, 1)
    out = jnp.zeros(score.shape, jnp.int32)
    for r in range(k):
        m = jnp.max(score, axis=1, keepdims=True)
        idx = jnp.min(jnp.where(score == m, lane, score.shape[1]), axis=1, keepdims=True)
        out = jnp.where(lane == r, idx, out)
        score = jnp.where(lane == idx, -jnp.inf, score)
    return out


def _head_rows(q):
    rows = [q[:, g * HEAD_DIM:(g + 1) * HEAD_DIM] for g in range(NSA_HEADS)]
    return jnp.concatenate(rows + [jnp.zeros((8 - NSA_HEADS, HEAD_DIM), q.dtype)], axis=0)


def _dec_select_kernel(abk_ref, abv_ref, km_ref, mq_ref, nq_ref, bias_ref, ovl_ref, ocmp_ref, tm_ref, ts_ref,
                       *, past):
    b = pl.program_id(0)
    n_ch = abk_ref.shape[1]
    nb = km_ref.shape[1]
    row = lax.broadcasted_iota(jnp.int32, (8, LANE), 0)
    mq = mq_ref[pl.ds(b, 1), :]
    sc = jnp.full((8, LANE), NEG, F32)
    for h in range(MOBA_HEADS):
        hs = slice(h * HEAD_DIM, (h + 1) * HEAD_DIM)
        qh = jnp.broadcast_to(mq[:, hs], (8, HEAD_DIM)).astype(BF16)
        s = lax.dot_general(qh, km_ref[0, :, hs].astype(BF16), (((1,), (1,)), ((), ())), preferred_element_type=F32)
        s = jnp.concatenate([s, jnp.full((8, LANE - nb), NEG, F32)], axis=1)
        sc = jnp.where(row == h, s, sc)
    tm_ref[0] = _topk_lanes(sc, MOBA_TOPK)
    abk = abk_ref[0]
    abv = abv_ref[0]
    kc = (abk[:, :HEAD_DIM] + pltpu.roll(abk[:, HEAD_DIM:], n_ch - 1, 0)).astype(BF16)
    vc = (abv[:, :HEAD_DIM] + pltpu.roll(abv[:, HEAD_DIM:], n_ch - 1, 0)).astype(BF16)
    q4 = _head_rows(nq_ref[pl.ds(b, 1), :]).astype(BF16)
    s = lax.dot_general(q4, kc, (((1,), (1,)), ((), ())), preferred_element_type=F32) * SCALE + bias_ref[...]
    n = lax.broadcasted_iota(jnp.int32, (8, n_ch), 1)
    vis = (n < n_ch - 1) & (n * CMP_STRIDE + (CMP_LEN - 1) <= past)
    s = jnp.where(vis, s, NEG)
    m = jnp.max(s, axis=1, keepdims=True)
    e = jnp.where(vis, jnp.exp(s - m), 0.0)
    den = jnp.sum(e, axis=1, keepdims=True)
    pb = (e / jnp.where(den > 0.0, den, 1.0)).astype(BF16)
    ocmp_ref[0] = jnp.dot(pb, vc, preferred_element_type=F32)
    imp = jnp.dot(pb, ovl_ref[...], preferred_element_type=F32)
    rows = lax.broadcasted_iota(jnp.int32, imp.shape, 0)
    imp = jnp.sum(jnp.where(rows < NSA_HEADS, imp, 0.0), axis=0, keepdims=True)
    ts_ref[0] = _topk_lanes(jnp.broadcast_to(imp, (8, imp.shape[1])), SLC_TOPK - 1)[:, :LANE]


def _dec_select(abk, abv, kmean, za_s, bias_cmp, overlap, past):
    B, n_ch, _ = abk.shape
    full = lambda a: pl.BlockSpec(a.shape, lambda b: (0,) * a.ndim)
    per_b = lambda a: pl.BlockSpec((1,) + a.shape[1:], lambda b: (b, 0, 0))
    out = jax.ShapeDtypeStruct((B, 8, LANE), F32)
    outi = jax.ShapeDtypeStruct((B, 8, LANE), jnp.int32)
    return pl.pallas_call(
        functools.partial(_dec_select_kernel, past=past),
        grid=(B,),
        in_specs=[per_b(abk), per_b(abv), per_b(kmean),
                  pl.BlockSpec((DEC_ROWS, MOBA_W), lambda b: (0, 0)),
                  pl.BlockSpec((DEC_ROWS, NSA_W), lambda b: (0, 3)),
                  full(bias_cmp), full(overlap)],
        out_specs=[pl.BlockSpec((1, 8, LANE), lambda b: (b, 0, 0))] * 3,
        out_shape=[out, outi, outi],
        compiler_params=_cparams(("parallel",)),
        name="dec_select",
    )(abk, abv, kmean, za_s, za_s, bias_cmp, overlap)


def _softmax_parts(scores):
    m = functools.reduce(jnp.maximum, [jnp.max(s, axis=1, keepdims=True) for s in scores])
    es = [jnp.exp(s - m) for s in scores]
    den = functools.reduce(lambda a, b: a + b, [jnp.sum(e, axis=1, keepdims=True) for e in es])
    return [e / den for e in es]


def _dec_attend_kernel(pm_ref, ps_ref, hs_ref, *refs, n_moba, n_slc, n_win):
    del pm_ref, ps_ref, hs_ref
    mk, mv = refs[:n_moba], refs[n_moba:2 * n_moba]
    sk, sv = refs[2 * n_moba:2 * n_moba + n_slc], refs[2 * n_moba + n_slc:2 * (n_moba + n_slc)]
    (kw_ref, vw_ref, mq_ref, mkn_ref, mvn_ref, nq_ref, skn_ref, svn_ref, wkn_ref, wvn_ref, bm_ref, bs_ref, bw_ref,
     oa_ref, oslc_ref, owin_ref) = refs[2 * (n_moba + n_slc):]
    b = pl.program_id(0)
    h = pl.program_id(1)
    nt = (((1,), (1,)), ((), ()))
    rnd = lambda a: a.astype(BF16).astype(F32)

    q = mq_ref[pl.ds(b, 1), :]
    q8 = jnp.broadcast_to(q, (8, HEAD_DIM)).astype(BF16)
    bm = bm_ref[0]
    scores = [lax.dot_general(q8, mk[j][0, 0].astype(BF16), nt, preferred_element_type=F32)[0:1] * SCALE
              + bm[j:j + 1] for j in range(n_moba)]
    s_new = jnp.sum(rnd(q) * rnd(mkn_ref[pl.ds(b, 1), :]), axis=1, keepdims=True) * SCALE + bm[n_moba:n_moba + 1, 0:1]
    probs = _softmax_parts(scores + [s_new])
    o = rnd(probs[-1]) * rnd(mvn_ref[pl.ds(b, 1), :])
    for j in range(n_moba):
        pj = jnp.broadcast_to(probs[j], (8, PAGE_SIZE)).astype(BF16)
        o = o + jnp.dot(pj, mv[j][0, 0].astype(BF16), preferred_element_type=F32)[0:1]
    oa_ref[0] = jnp.broadcast_to(o, (8, HEAD_DIM))

    @pl.when(h == 0)
    def _():
        q4f = _head_rows(nq_ref[pl.ds(b, 1), :])
        q4 = q4f.astype(BF16)
        scores = [lax.dot_general(q4, sk[j][0, 0].astype(BF16), nt, preferred_element_type=F32) * SCALE + bs_ref[0, j]
                  for j in range(n_slc)]
        s_new = (jnp.sum(rnd(q4f) * rnd(skn_ref[pl.ds(b, 1), :]), axis=1, keepdims=True) * SCALE
                 + bs_ref[0, n_slc][:, 0:1])
        probs = _softmax_parts(scores + [s_new])
        o = rnd(probs[-1]) * rnd(svn_ref[pl.ds(b, 1), :])
        for j in range(n_slc):
            o = o + jnp.dot(probs[j].astype(BF16), sv[j][0, 0].astype(BF16), preferred_element_type=F32)
        oslc_ref[0] = o
        s = lax.dot_general(q4, kw_ref[0, 0].astype(BF16), nt, preferred_element_type=F32) * SCALE + bw_ref[:, :n_win]
        i = lax.broadcasted_iota(jnp.int32, (8, n_win), 1)
        s = jnp.where(n_win - i < WINDOW, s, NEG)
        s_new = (jnp.sum(rnd(q4f) * rnd(wkn_ref[pl.ds(b, 1), :]), axis=1, keepdims=True) * SCALE
                 + bw_ref[:, n_win:n_win + 1])
        p_win, p_new = _softmax_parts([s, s_new])
        p_win = jnp.where(n_win - i < WINDOW, p_win, 0.0)
        owin_ref[0] = (rnd(p_new) * rnd(wvn_ref[pl.ds(b, 1), :])
                       + jnp.dot(p_win.astype(BF16), vw_ref[0, 0].astype(BF16), preferred_element_type=F32))


def _dec_attend(cache_kv, cache_win, l, za_s, pages_moba, pages_slc, halves_slc, bias_moba, bias_slc, bias_win, B):
    n_moba = MOBA_TOPK * MOBA_BLOCK // PAGE_SIZE
    n_slc = SLC_TOPK - 1
    n_win = cache_win.shape[2]

    def moba_spec(j, col0):
        return pl.BlockSpec((1, 1, PAGE_SIZE, HEAD_DIM),
                            lambda b, h, pm, ps, hs: (l, pm[(b * MOBA_HEADS + h) * n_moba + j], 0, col0 + h))

    def slc_spec(j, col):
        return pl.BlockSpec((1, 1, SLC_BLOCK, HEAD_DIM),
                            lambda b, h, pm, ps, hs: (l, ps[b * n_slc + j], hs[b * n_slc + j], col))

    zcol = lambda w, c: pl.BlockSpec((DEC_ROWS, w), lambda b, h, pm, ps, hs: (0, c))
    zhead = lambda c0: pl.BlockSpec((DEC_ROWS, HEAD_DIM), lambda b, h, pm, ps, hs: (0, c0 + h))
    slc_col = (2 * MOBA_W + 2 * HEAD_DIM) // HEAD_DIM
    in_specs = ([moba_spec(j, 0) for j in range(n_moba)] + [moba_spec(j, MOBA_HEADS) for j in range(n_moba)]
                + [slc_spec(j, slc_col) for j in range(n_slc)] + [slc_spec(j, slc_col + 1) for j in range(n_slc)]
                + [pl.BlockSpec((1, 1, n_win, HEAD_DIM), lambda b, h, pm, ps, hs: (l, b, 0, 0)),
                   pl.BlockSpec((1, 1, n_win, HEAD_DIM), lambda b, h, pm, ps, hs: (l, b, 0, 1)),
                   zhead(0), zhead(MOBA_HEADS), zhead(2 * MOBA_HEADS), zcol(NSA_W, 3),
                   zcol(HEAD_DIM, 18), zcol(HEAD_DIM, 19), zcol(HEAD_DIM, 20), zcol(HEAD_DIM, 21),
                   pl.BlockSpec((1, 8, LANE), lambda b, h, pm, ps, hs: (b * MOBA_HEADS + h, 0, 0)),
                   pl.BlockSpec((1, n_slc + 1, 8, SLC_BLOCK), lambda b, h, pm, ps, hs: (b, 0, 0, 0)),
                   pl.BlockSpec(bias_win.shape, lambda b, h, pm, ps, hs: (0, 0))])
    grid_spec = pltpu.PrefetchScalarGridSpec(
        num_scalar_prefetch=3,
        grid=(B, MOBA_HEADS),
        in_specs=in_specs,
        out_specs=[pl.BlockSpec((1, 8, HEAD_DIM), lambda b, h, pm, ps, hs: (b * MOBA_HEADS + h, 0, 0)),
                   pl.BlockSpec((1, 8, HEAD_DIM), lambda b, h, pm, ps, hs: (b, 0, 0)),
                   pl.BlockSpec((1, 8, HEAD_DIM), lambda b, h, pm, ps, hs: (b, 0, 0))],
    )
    return pl.pallas_call(
        functools.partial(_dec_attend_kernel, n_moba=n_moba, n_slc=n_slc, n_win=n_win),
        grid_spec=grid_spec,
        out_shape=[jax.ShapeDtypeStruct((B * MOBA_HEADS, 8, HEAD_DIM), F32),
                   jax.ShapeDtypeStruct((B, 8, HEAD_DIM), F32), jax.ShapeDtypeStruct((B, 8, HEAD_DIM), F32)],
        compiler_params=_cparams(("parallel", "arbitrary")),
        name="dec_attend",
    )(pages_moba, pages_slc, halves_slc, *([cache_kv] * (2 * n_moba + 2 * n_slc)), cache_win, cache_win,
      za_s, za_s, za_s, za_s, za_s, za_s, za_s, za_s, bias_moba, bias_slc, bias_win)


def _lru_step_kernel(x_ref, g_ref, c0_ref, h0_ref, cw_ref, cb_ref, wr_ref, br_ref, wi_ref, bi_ref, lam_ref,
                     y_ref, h_ref, cl_ref):
    B = h0_ref.shape[0]
    x = x_ref[0:B, :]
    u = cb_ref[...] + c0_ref[0] * cw_ref[0:1, :]
    u = u + c0_ref[1] * cw_ref[1:2, :]
    u = u + c0_ref[2] * cw_ref[2:3, :]
    u = u + x * cw_ref[3:4, :]
    ub = u.astype(BF16)
    r = jax.nn.sigmoid(jnp.dot(ub, wr_ref[0], preferred_element_type=F32) + br_ref[...])
    ig = jax.nn.sigmoid(jnp.dot(ub, wi_ref[0], preferred_element_type=F32) + bi_ref[...])
    log_a = (-LRU_C * r) * _softplus(-lam_ref[...])
    h = jnp.exp(log_a) * h0_ref[...] + jnp.sqrt(1.0 - jnp.exp(2.0 * log_a)) * (ig * u)
    h_ref[...] = h
    y_ref[...] = h * _gelu_tanh(g_ref[0:B, :])
    cl_ref[0] = c0_ref[1]
    cl_ref[1] = c0_ref[2]
    cl_ref[2] = x


def _lru_step(zb_s, conv0_t, h0, conv_w, conv_b, w_r, b_r, w_i, b_i, lam):
    B = h0.shape[0]
    nblk = LRU_BLOCKS
    vec = lambda a: a.reshape(1, LRU_WIDTH)
    row_spec = pl.BlockSpec((1, LRU_BLOCK), lambda n: (0, n))
    bw = lambda: pl.BlockSpec((B, LRU_BLOCK), lambda n: (0, n))
    return pl.pallas_call(
        _lru_step_kernel,
        grid=(nblk,),
        in_specs=[pl.BlockSpec((DEC_ROWS, LRU_BLOCK), lambda n: (0, n)),
                  pl.BlockSpec((DEC_ROWS, LRU_BLOCK), lambda n: (0, nblk + n)),
                  pl.BlockSpec((CONV_W - 1, B, LRU_BLOCK), lambda n: (0, 0, n)), bw(),
                  pl.BlockSpec((CONV_W, LRU_BLOCK), lambda n: (0, n)), row_spec,
                  pl.BlockSpec((1, LRU_BLOCK, LRU_BLOCK), lambda n: (n, 0, 0)), row_spec,
                  pl.BlockSpec((1, LRU_BLOCK, LRU_BLOCK), lambda n: (n, 0, 0)), row_spec, row_spec],
        out_specs=[bw(), bw(), pl.BlockSpec((CONV_W - 1, B, LRU_BLOCK), lambda n: (0, 0, n))],
        out_shape=[jax.ShapeDtypeStruct((B, LRU_WIDTH), F32), jax.ShapeDtypeStruct((B, LRU_WIDTH), F32),
                   jax.ShapeDtypeStruct((CONV_W - 1, B, LRU_WIDTH), F32)],
        compiler_params=_cparams(("parallel",)),
        name="lru_step",
    )(zb_s, zb_s, conv0_t, h0, conv_w, vec(conv_b), w_r, vec(b_r), w_i, vec(b_i), vec(lam))


def _bias_heads(tab, dist):
    bucket = _t5_bucket(dist)
    out = jnp.zeros(bucket.shape, F32)
    shape = (1, -1) + (1,) * (bucket.ndim - 2)
    for bkt in range(T5_BUCKETS):
        out = jnp.where(bucket == bkt, tab[:, bkt].reshape(shape), out)
    return out


def _decode_tables(t5_bias, past, n_win):
    tab_n = t5_bias.T[MOBA_HEADS:]
    n_ch = past // CMP_STRIDE
    n = jnp.arange(n_ch)
    bias_cmp = jnp.pad(_bias_lookup(tab_n, past - (n * CMP_STRIDE + CMP_LEN - 1)), ((0, 8 - NSA_HEADS), (0, 0)))
    s_start = jnp.arange(past // SLC_BLOCK) * SLC_BLOCK
    c_start = n * CMP_STRIDE
    overlap = ((c_start[:, None] < s_start[None, :] + SLC_BLOCK) & (c_start[:, None] + CMP_LEN > s_start[None, :])
               & (n[:, None] < n_ch - 1)).astype(BF16)
    d_win = jnp.concatenate([n_win - jnp.arange(n_win), jnp.zeros((LANE,), jnp.int32)])
    bias_win = jnp.pad(_bias_lookup(tab_n, d_win), ((0, 8 - NSA_HEADS), (0, 0)))
    return bias_cmp, overlap, bias_win


def _decode_attention(l, za_s, cache_kv, cache_win, page_table, lp_cmp, dtabs, t5_bias):
    B, n_pages = page_table.shape
    past = n_pages * PAGE_SIZE
    bias_cmp, overlap, bias_win = dtabs
    pe_k, pe_v, w_k, w_v = lp_cmp
    abk, abv, kmean = _page_scan(cache_kv, l, page_table, pe_k, pe_v, w_k, w_v)
    o_cmp, top_m, top_s = _dec_select(abk, abv, kmean, za_s, bias_cmp, overlap, past)
    top_m = top_m[:, :MOBA_HEADS, :MOBA_TOPK]
    top_s = top_s[:, 0, :SLC_TOPK - 1]
    tab = t5_bias.T
    bidx = jnp.arange(B)
    per_blk = MOBA_BLOCK // PAGE_SIZE
    pg_off = top_m[..., None] * per_blk + jnp.arange(per_blk)
    pages_moba = page_table[bidx[:, None, None, None], pg_off].reshape(-1)
    kpos = (pg_off.reshape(B, MOBA_HEADS, -1, 1) * PAGE_SIZE + jnp.arange(PAGE_SIZE))
    bm = _bias_heads(tab[:MOBA_HEADS], past - kpos)
    bm_new = jnp.zeros((B, MOBA_HEADS, 1, LANE), F32).at[..., 0].set(tab[None, :MOBA_HEADS, 0, None])
    bias_moba = jnp.concatenate([bm, bm_new, jnp.zeros((B, MOBA_HEADS, 1, LANE), F32)], axis=2)
    bias_moba = bias_moba.reshape(B * MOBA_HEADS, 8, LANE)
    per_page = PAGE_SIZE // SLC_BLOCK
    pages_slc = page_table[bidx[:, None], top_s // per_page].reshape(-1)
    halves_slc = (top_s % per_page).reshape(-1)
    spos = top_s[:, None, :, None] * SLC_BLOCK + jnp.arange(SLC_BLOCK)
    bs = _bias_heads(tab[MOBA_HEADS:], jnp.broadcast_to(past - spos, (B, NSA_HEADS, SLC_TOPK - 1, SLC_BLOCK)))
    bs_new = jnp.zeros((B, NSA_HEADS, 1, SLC_BLOCK), F32).at[..., 0].set(tab[None, MOBA_HEADS:, 0, None])
    bias_slc = jnp.concatenate([bs, bs_new], axis=2).transpose(0, 2, 1, 3)
    bias_slc = jnp.pad(bias_slc, ((0, 0), (0, 0), (0, 8 - NSA_HEADS), (0, 0)))
    o_a, o_slc, o_win = _dec_attend(cache_kv, cache_win, l, za_s, pages_moba, pages_slc, halves_slc,
                                    bias_moba, bias_slc, bias_win, B)
    pad = lambda a: jnp.pad(a, ((0, DEC_ROWS - B), (0, 0)))
    heads = lambda a: pad(a[:, :NSA_HEADS].reshape(B, NSA_W))
    return pad(o_a[:, 0].reshape(B, MOBA_W)), heads(o_cmp), heads(o_slc), heads(o_win)


def kernel(x_prompt, x_sample, cache_kv, cache_win, state_lru_h, state_lru_conv, page_table, c_prompt, c_sample,
           w_ada, b_ada, norm_mix, norm_ffn, w_in, cmp_pos_k, cmp_pos_v, w_cmp_k, w_cmp_v, conv_w, conv_b,
           w_rg, b_rg, w_ig, b_ig, lru_lambda, w_br_moba, w_br_nsa, w_br_lru, w_out, w_e_gate, w_e_up, w_e_down,
           t5_bias, w_router, b_router, norm_final):
    n_p, T, D = x_prompt.shape
    n_s = x_sample.shape[0]
    n_tok = n_p * T
    xp = x_prompt.reshape(n_tok, D)
    xs = x_sample.reshape(n_s, D)
    tabs = _prompt_tables(t5_bias, T)
    w_router_pad = jnp.pad(w_router, ((0, 0), (0, LANE - N_EXPERTS))).astype(BF16)
    b_router_pad = jnp.pad(b_router, (0, LANE - N_EXPERTS)).reshape(1, LANE)
    c_all = jax.nn.silu(jnp.concatenate([c_prompt, c_sample], axis=0))
    c_all = jnp.pad(c_all, ((0, 16 - n_p - n_s), (0, 0))).astype(BF16)
    dtabs = _decode_tables(t5_bias, page_table.shape[1] * PAGE_SIZE, cache_win.shape[2])
    moe_rows = LANE
    outs = [[] for _ in range(7)]
    kv_p = None
    mods =[_matmul_w32(c_all, w_ada, l, 0, 6 * D, 2048, 16) + b_ada[l] for l in range(DEPTH)]
    mod0 = mods[0][:n_p, None, :]
    hp = (_rms(x_prompt, norm_mix[0]) * (1.0 + mod0[..., D:2 * D]) + mod0[..., :D]).astype(BF16).reshape(n_tok, D)
    for l in range(DEPTH):
        lp = {'l': l, 'norm_mix': norm_mix[l], 'norm_ffn': norm_ffn[l], 'w_in': w_in,
              'cmp_pos_k': cmp_pos_k[l], 'cmp_pos_v': cmp_pos_v[l],
              'w_cmp_k': w_cmp_k[l].astype(BF16), 'w_cmp_v': w_cmp_v[l].astype(BF16),
              'conv_w': conv_w[l], 'conv_b': conv_b[l], 'w_rg': w_rg[l].astype(BF16), 'b_rg': b_rg[l],
              'w_ig': w_ig[l].astype(BF16), 'b_ig': b_ig[l], 'lru_lambda': lru_lambda[l],
              'w_br_moba': w_br_moba[l].astype(BF16), 'w_br_nsa': w_br_nsa[l].astype(BF16),
              'w_br_lru': w_br_lru[l].astype(BF16), 'w_out': w_out[l].astype(BF16)}
        mod = mods[l]
        mod3 = mod.reshape(16 * 6, 1, D)
        mod_s = mod[n_p:n_p + n_s]
        xp, h2_p, lg_p, kv_p, win_p, h_p, conv_p = _mixer_prompt(xp, hp, n_p, T, mod3, lp, tabs, w_router_pad,
                                                                 b_router_pad, kv_p)
        xs, h2_s, lg_s, kv_s, win_s, h_s, conv_s = _mixer_decode(l, xs, mod_s, lp, cache_kv, cache_win, page_table,
                                                                 state_lru_h[l], state_lru_conv[l], dtabs, t5_bias,
                                                                 w_router_pad, b_router_pad)
        tail = ((0, moe_rows - h2_s.shape[0]), (0, 0))
        h2 = jnp.concatenate([h2_p, jnp.pad(h2_s, tail)], axis=0)
        logits = jnp.concatenate([lg_p, jnp.pad(lg_s, tail)], axis=0)[:, :N_EXPERTS]
        moe_a, moe_b = _moe_prompt(h2, logits, l, w_e_gate, w_e_up, w_e_down)
        last = l == DEPTH - 1
        if last:
            y_prompt, = _ffn_residual(xp, moe_a, moe_b, mod3, mod3, norm_final, 256, T, True)
        else:
            mod3_next = mods[l + 1].reshape(16 * 6, 1, D)
            xp, hp = _ffn_residual(xp, moe_a, moe_b, mod3, mod3_next, norm_mix[l + 1], 256, T, False)
        xs = xs + mod_s[:, 5 * D:] * (moe_a[n_tok:n_tok + n_s] + moe_b[n_tok:n_tok + n_s])
        for lst, val in zip(outs, (kv_s[:, None, :], win_p, win_s, h_p, h_s, conv_p, conv_s)):
            lst.append(val)
    y_sample = _rms(xs, norm_final).reshape(n_s, 1, D)
    kv_s, *rest = [jnp.stack(o) for o in outs]
    return (y_prompt.reshape(n_p, T, D), y_sample, kv_p.reshape(DEPTH, n_p, T, KV_DIM), kv_s) + tuple(rest)
```
